```python
import jax, jax.numpy as jnp
from jax import lax
import numpy as np

D_MODEL = 4096
BATCH = 16
SEQ = 2048
DEPTH = 2

N_MIXERS = 2
POOL_WINDOWS = (2, 4, 8, 16)
N_POOL_GROUPS = len(POOL_WINDOWS)
POOL_GROUP = D_MODEL // N_POOL_GROUPS
GLA_HEADS = 4
GLA_KEY_DIM = D_MODEL // 2
GLA_VAL_DIM = D_MODEL
GLA_HEAD_K = GLA_KEY_DIM // GLA_HEADS
GLA_HEAD_V = GLA_VAL_DIM // GLA_HEADS
GATE_RANK = 16
GATE_TAU = 16.0
CHUNK = 64
D_FF = 4 * D_MODEL
EPS = 1e-6
N_POOL_LAYERS = (DEPTH + 1) // 2
N_GLA_LAYERS = DEPTH // 2
PROJ_WIDTH = 2 * GLA_KEY_DIM + 2 * GLA_VAL_DIM + 2 * GATE_RANK

kernel_name = "hybrid_pool_gla_encoder"


def rmsnorm(x, g):
    xf = x.astype(jnp.float32)
    y = xf * lax.rsqrt(jnp.mean(xf * xf, axis=-1, keepdims=True) + EPS)
    return (y * g.astype(jnp.float32)).astype(x.dtype)


def pool_mixer(h, w_group, scale):
    B, S, D = h.shape
    hf = h.astype(jnp.float32)
    csum = jnp.concatenate([jnp.zeros((B, 1, D), jnp.float32), jnp.cumsum(hf, axis=1)], axis=1)
    pos = jnp.arange(S)
    diffs = []
    for g, w in enumerate(POOL_WINDOWS):
        lo = jnp.clip(pos - w // 2, 0, S)
        hi = jnp.clip(pos + w // 2, 0, S)
        sl = slice(g * POOL_GROUP, (g + 1) * POOL_GROUP)
        cs = csum[..., sl]
        count = (hi - lo).astype(jnp.float32)[None, :, None]
        mean = (cs[:, hi] - cs[:, lo]) / count
        diffs.append(mean - hf[..., sl])
    d = jnp.stack(diffs, axis=2).astype(h.dtype)
    y = jnp.einsum('bsgc,gcd->bsgd', d, w_group).reshape(B, S, D)
    return y * scale


def gla_chunked(q, k, v, log_a):
    B, S, H, dk = q.shape
    dv = v.shape[-1]
    n_chunks = S // CHUNK

    def to_chunks(t):
        return t.astype(jnp.float32).reshape(B, n_chunks, CHUNK, H, -1).transpose(1, 0, 3, 2, 4)

    qc, kc, vc, gc = to_chunks(q), to_chunks(k), to_chunks(v), to_chunks(log_a)
    b = jnp.cumsum(gc, axis=3)
    b_last = b[..., CHUNK - 1:, :]
    b_mid = b[..., CHUNK // 2 - 1:CHUNK // 2, :]
    scores = jnp.einsum('nbhik,nbhjk->nbhij', qc * jnp.exp(b - b_mid), kc * jnp.exp(b_mid - b))
    mask = jnp.tril(jnp.ones((CHUNK, CHUNK), dtype=bool))
    scores = jnp.where(mask, scores, 0.0)
    o_intra = jnp.einsum('nbhij,nbhjv->nbhiv', scores, vc)
    q_inter = qc * jnp.exp(b)
    k_state = kc * jnp.exp(b_last - b)
    a_chunk = jnp.exp(b_last[..., 0, :])

    def step(state, inp):
        qi, ki, vi, ai = inp
        o = jnp.einsum('bhik,bhkv->bhiv', qi, state)
        state = ai[..., None] * state + jnp.einsum('bhjk,bhjv->bhkv', ki, vi)
        return state, o

    state0 = jnp.zeros((B, H, dk, dv), jnp.float32)
    _, o_inter = lax.scan(step, state0, (q_inter, k_state, vc, a_chunk))
    o = o_intra + o_inter
    return o.transpose(1, 0, 3, 2, 4).reshape(B, S, H, dv)


def gla_mixer(h, w_in, w_up_f, b_up_f, w_up_b, b_up_b, g_norm, w_out):
    B, S, D = h.shape
    p = h @ w_in
    c1 = GLA_KEY_DIM
    c2 = 2 * GLA_KEY_DIM
    c3 = c2 + GLA_VAL_DIM
    c4 = c3 + GLA_VAL_DIM
    c5 = c4 + GATE_RANK
    q, k, v, gate, r_f, r_b = jnp.split(p, [c1, c2, c3, c4, c5], axis=-1)
    q = q.reshape(B, S, GLA_HEADS, GLA_HEAD_K) * (GLA_HEAD_K ** -0.5)
    k = k.reshape(B, S, GLA_HEADS, GLA_HEAD_K)
    v = v.reshape(B, S, GLA_HEADS, GLA_HEAD_V)
    log_a_f = jax.nn.log_sigmoid((r_f @ w_up_f + b_up_f).astype(jnp.float32)) / GATE_TAU
    log_a_b = jax.nn.log_sigmoid((r_b @ w_up_b + b_up_b).astype(jnp.float32)) / GATE_TAU
    log_a_f = log_a_f.reshape(B, S, GLA_HEADS, GLA_HEAD_K)
    log_a_b = log_a_b.reshape(B, S, GLA_HEADS, GLA_HEAD_K)
    o_fwd = gla_chunked(q, k, v, log_a_f)
    flip = lambda t: jnp.flip(t, axis=1)
    o_bwd = flip(gla_chunked(flip(q), flip(k), flip(v), flip(log_a_b)))
    o = o_fwd + o_bwd
    o = o * lax.rsqrt(jnp.mean(o * o, axis=-1, keepdims=True) + EPS) * g_norm.astype(jnp.float32)
    o = o.reshape(B, S, GLA_VAL_DIM) * jax.nn.silu(gate.astype(jnp.float32))
    return o.astype(h.dtype) @ w_out


def relu2_mlp(h, w1, w2):
    a = jax.nn.relu(h @ w1)
    return (a * a) @ w2


def _fwd_setup_inputs(seed: int = 0) -> dict:
    key = jax.random.key(seed)
    ks = jax.random.split(key, 16)
    f32 = jnp.float32
    nrm = lambda k, shape, s: (jax.random.normal(k, shape, f32) * s).astype(f32)
    return {
        "x": nrm(ks[0], (BATCH, SEQ, D_MODEL), 1.0),
        "norm_mix": 1.0 + nrm(ks[1], (DEPTH, D_MODEL), 0.02),
        "norm_mlp": 1.0 + nrm(ks[2], (DEPTH, D_MODEL), 0.02),
        "norm_final": 1.0 + nrm(ks[3], (D_MODEL,), 0.02),
        "pool_w": nrm(ks[4], (N_POOL_LAYERS, N_POOL_GROUPS, POOL_GROUP, POOL_GROUP), POOL_GROUP ** -0.5),
        "pool_scale": 1.0 + nrm(ks[5], (N_POOL_LAYERS, D_MODEL), 0.02),
        "gla_w_in": nrm(ks[6], (N_GLA_LAYERS, D_MODEL, PROJ_WIDTH), D_MODEL ** -0.5),
        "gla_w_up_f": nrm(ks[7], (N_GLA_LAYERS, GATE_RANK, GLA_KEY_DIM), GATE_RANK ** -0.5),
        "gla_b_up_f": nrm(ks[8], (N_GLA_LAYERS, GLA_KEY_DIM), 0.1),
        "gla_w_up_b": nrm(ks[9], (N_GLA_LAYERS, GATE_RANK, GLA_KEY_DIM), GATE_RANK ** -0.5),
        "gla_b_up_b": nrm(ks[10], (N_GLA_LAYERS, GLA_KEY_DIM), 0.1),
        "gla_g_norm": 1.0 + nrm(ks[11], (N_GLA_LAYERS, GLA_HEAD_V), 0.02),
        "gla_w_out": nrm(ks[12], (N_GLA_LAYERS, GLA_VAL_DIM, D_MODEL), GLA_VAL_DIM ** -0.5),
        "mlp_w_in": nrm(ks[13], (DEPTH, D_MODEL, D_FF), D_MODEL ** -0.5),
        "mlp_w_out": nrm(ks[14], (DEPTH, D_FF, D_MODEL), D_FF ** -0.5),
    }


def _fwd_reference(x, norm_mix, norm_mlp, norm_final, pool_w, pool_scale, gla_w_in, gla_w_up_f,
              gla_b_up_f, gla_w_up_b, gla_b_up_b, gla_g_norm, gla_w_out, mlp_w_in, mlp_w_out):
    h = x
    for layer in range(DEPTH):
        j = layer // N_MIXERS
        hn = rmsnorm(h, norm_mix[layer])
        if layer % N_MIXERS == 0:
            h = h + pool_mixer(hn, pool_w[j], pool_scale[j])
        else:
            h = h + gla_mixer(hn, gla_w_in[j], gla_w_up_f[j], gla_b_up_f[j], gla_w_up_b[j],
                              gla_b_up_b[j], gla_g_norm[j], gla_w_out[j])
        hn = rmsnorm(h, norm_mlp[layer])
        h = h + relu2_mlp(hn, mlp_w_in[layer], mlp_w_out[layer])
    return rmsnorm(h, norm_final)


import jax as _jax
import jax.numpy as _jnp

TWIN_FORMAT = 'train_step'
FWD_PARAMS = ['x', 'norm_mix', 'norm_mlp', 'norm_final', 'pool_w', 'pool_scale', 'gla_w_in', 'gla_w_up_f', 'gla_b_up_f', 'gla_w_up_b', 'gla_b_up_b', 'gla_g_norm', 'gla_w_out', 'mlp_w_in', 'mlp_w_out']
TWIN_WEIGHTS = ['norm_mix', 'norm_mlp', 'norm_final', 'pool_w', 'pool_scale', 'gla_w_in', 'gla_w_up_f', 'gla_b_up_f', 'gla_w_up_b', 'gla_b_up_b', 'gla_g_norm', 'gla_w_out', 'mlp_w_in', 'mlp_w_out']
TWIN_DIFF_INPUT = 'x'
TWIN_INPUTS = ['x', 'norm_mix', 'norm_mlp', 'norm_final', 'pool_w', 'pool_scale', 'gla_w_in', 'gla_w_up_f', 'gla_b_up_f', 'gla_w_up_b', 'gla_b_up_b', 'gla_g_norm', 'gla_w_out', 'mlp_w_in', 'mlp_w_out', 'loss_target', 'm_norm_mix', 'm_norm_mlp', 'm_norm_final', 'm_pool_w', 'm_pool_scale', 'm_gla_w_in', 'm_gla_w_up_f', 'm_gla_b_up_f', 'm_gla_w_up_b', 'm_gla_b_up_b', 'm_gla_g_norm', 'm_gla_w_out', 'm_mlp_w_in', 'm_mlp_w_out', 'v_norm_mix', 'v_norm_mlp', 'v_norm_final', 'v_pool_w', 'v_pool_scale', 'v_gla_w_in', 'v_gla_w_up_f', 'v_gla_b_up_f', 'v_gla_w_up_b', 'v_gla_b_up_b', 'v_gla_g_norm', 'v_gla_w_out', 'v_mlp_w_in', 'v_mlp_w_out']
TWIN_OUTPUTS = ['loss', 'grad_x', 'grad_norm_mix', 'grad_norm_mlp', 'grad_norm_final', 'grad_pool_w', 'grad_pool_scale', 'grad_gla_w_in', 'grad_gla_w_up_f', 'grad_gla_b_up_f', 'grad_gla_w_up_b', 'grad_gla_b_up_b', 'grad_gla_g_norm', 'grad_gla_w_out', 'grad_mlp_w_in', 'grad_mlp_w_out', 'delta_norm_mix', 'delta_norm_mlp', 'delta_norm_final', 'delta_pool_w', 'delta_pool_scale', 'delta_gla_w_in', 'delta_gla_w_up_f', 'delta_gla_b_up_f', 'delta_gla_w_up_b', 'delta_gla_b_up_b', 'delta_gla_g_norm', 'delta_gla_w_out', 'delta_mlp_w_in', 'delta_mlp_w_out', 'new_m_norm_mix', 'new_m_norm_mlp', 'new_m_norm_final', 'new_m_pool_w', 'new_m_pool_scale', 'new_m_gla_w_in', 'new_m_gla_w_up_f', 'new_m_gla_b_up_f', 'new_m_gla_w_up_b', 'new_m_gla_b_up_b', 'new_m_gla_g_norm', 'new_m_gla_w_out', 'new_m_mlp_w_in', 'new_m_mlp_w_out', 'new_v_norm_mix', 'new_v_norm_mlp', 'new_v_norm_final', 'new_v_pool_w', 'new_v_pool_scale', 'new_v_gla_w_in', 'new_v_gla_w_up_f', 'new_v_gla_b_up_f', 'new_v_gla_w_up_b', 'new_v_gla_b_up_b', 'new_v_gla_g_norm', 'new_v_gla_w_out', 'new_v_mlp_w_in', 'new_v_mlp_w_out']
TWIN_LEAF_KINDS = {'loss': 'loss', 'grad_x': 'grad_x', 'grad_norm_mix': 'grad_w', 'grad_norm_mlp': 'grad_w', 'grad_norm_final': 'grad_w', 'grad_pool_w': 'grad_w', 'grad_pool_scale': 'grad_w', 'grad_gla_w_in': 'grad_w', 'grad_gla_w_up_f': 'grad_w', 'grad_gla_b_up_f': 'grad_w', 'grad_gla_w_up_b': 'grad_w', 'grad_gla_b_up_b': 'grad_w', 'grad_gla_g_norm': 'grad_w', 'grad_gla_w_out': 'grad_w', 'grad_mlp_w_in': 'grad_w', 'grad_mlp_w_out': 'grad_w', 'delta_norm_mix': 'delta_w', 'delta_norm_mlp': 'delta_w', 'delta_norm_final': 'delta_w', 'delta_pool_w': 'delta_w', 'delta_pool_scale': 'delta_w', 'delta_gla_w_in': 'delta_w', 'delta_gla_w_up_f': 'delta_w', 'delta_gla_b_up_f': 'delta_w', 'delta_gla_w_up_b': 'delta_w', 'delta_gla_b_up_b': 'delta_w', 'delta_gla_g_norm': 'delta_w', 'delta_gla_w_out': 'delta_w', 'delta_mlp_w_in': 'delta_w', 'delta_mlp_w_out': 'delta_w', 'new_m_norm_mix': 'new_m', 'new_m_norm_mlp': 'new_m', 'new_m_norm_final': 'new_m', 'new_m_pool_w': 'new_m', 'new_m_pool_scale': 'new_m', 'new_m_gla_w_in': 'new_m', 'new_m_gla_w_up_f': 'new_m', 'new_m_gla_b_up_f': 'new_m', 'new_m_gla_w_up_b': 'new_m', 'new_m_gla_b_up_b': 'new_m', 'new_m_gla_g_norm': 'new_m', 'new_m_gla_w_out': 'new_m', 'new_m_mlp_w_in': 'new_m', 'new_m_mlp_w_out': 'new_m', 'new_v_norm_mix': 'new_v', 'new_v_norm_mlp': 'new_v', 'new_v_norm_final': 'new_v', 'new_v_pool_w': 'new_v', 'new_v_pool_scale': 'new_v', 'new_v_gla_w_in': 'new_v', 'new_v_gla_w_up_f': 'new_v', 'new_v_gla_b_up_f': 'new_v', 'new_v_gla_w_up_b': 'new_v', 'new_v_gla_b_up_b': 'new_v', 'new_v_gla_g_norm': 'new_v', 'new_v_gla_w_out': 'new_v', 'new_v_mlp_w_in': 'new_v', 'new_v_mlp_w_out': 'new_v'}


def _forward(args):
    return _fwd_reference(*[args[k] for k in FWD_PARAMS])


def _output_shape():
    def fwd():
        inp = _fwd_setup_inputs(0)
        return _fwd_reference(*[inp[k] for k in FWD_PARAMS])
    out = _jax.eval_shape(fwd)
    return out.shape, out.dtype

N_MICROBATCH = 1
ADAM_LR = 0.001
ADAM_B1 = 0.9
ADAM_B2 = 0.999
ADAM_EPS = 1e-08
ADAM_WD = 0.01
ADAM_STEP = 10
PER_EXAMPLE_BATCH_AXIS = {'x': 0, 'loss_target': 0}
SHARED_INPUTS = []
_WEIGHT_DTYPES = {'norm_mix': _jnp.float32, 'norm_mlp': _jnp.float32, 'norm_final': _jnp.float32, 'pool_w': _jnp.float32, 'pool_scale': _jnp.float32, 'gla_w_in': _jnp.float32, 'gla_w_up_f': _jnp.float32, 'gla_b_up_f': _jnp.float32, 'gla_w_up_b': _jnp.float32, 'gla_b_up_b': _jnp.float32, 'gla_g_norm': _jnp.float32, 'gla_w_out': _jnp.float32, 'mlp_w_in': _jnp.float32, 'mlp_w_out': _jnp.float32}
MOMENT_SCALE = {'norm_mix': 3.396860e-02, 'norm_mlp': 3.550500e-02, 'norm_final': 8.108377e+00, 'pool_w': 3.715245e-02, 'pool_scale': 5.093796e-02, 'gla_w_in': 1.755299e-02, 'gla_w_up_f': 1.641412e-03, 'gla_b_up_f': 7.003931e-03, 'gla_w_up_b': 2.473689e-03, 'gla_b_up_b': 7.050600e-03, 'gla_g_norm': 2.899145e-02, 'gla_w_out': 1.463437e-02, 'mlp_w_in': 1.764391e-02, 'mlp_w_out': 3.544094e-02}


def _to_microbatches(a, axis):
    t = _jnp.moveaxis(a, axis, 0)
    t = t.reshape((N_MICROBATCH, t.shape[0] // N_MICROBATCH) + t.shape[1:])
    return _jnp.moveaxis(t, 1, axis + 1)


def setup_inputs(seed: int = 0) -> dict:
    inp = _fwd_setup_inputs(seed)
    key = _jax.random.fold_in(_jax.random.key(seed), 7919)
    shape, _ = _output_shape()
    out = dict(inp)
    out["loss_target"] = _jax.random.normal(_jax.random.fold_in(key, 0), shape, _jnp.float32)
    for i, name in enumerate(TWIN_WEIGHTS):
        w = inp[name].astype(_jnp.float32)
        if MOMENT_SCALE is None:
            s = _jnp.sqrt(_jnp.mean(_jnp.square(w)) + 1e-30)
        else:
            s = MOMENT_SCALE[name]
        km, kv = _jax.random.split(_jax.random.fold_in(key, i + 1))
        out[name] = w
        out["m_" + name] = s * _jax.random.normal(km, w.shape, _jnp.float32)
        out["v_" + name] = (s * s) * _jax.random.uniform(kv, w.shape, _jnp.float32, 0.5, 1.5)
    if N_MICROBATCH > 1:
        for name, axis in PER_EXAMPLE_BATCH_AXIS.items():
            out[name] = _to_microbatches(out[name], axis)
    return {'x': out['x'], 'norm_mix': out['norm_mix'], 'norm_mlp': out['norm_mlp'], 'norm_final': out['norm_final'], 'pool_w': out['pool_w'], 'pool_scale': out['pool_scale'], 'gla_w_in': out['gla_w_in'], 'gla_w_up_f': out['gla_w_up_f'], 'gla_b_up_f': out['gla_b_up_f'], 'gla_w_up_b': out['gla_w_up_b'], 'gla_b_up_b': out['gla_b_up_b'], 'gla_g_norm': out['gla_g_norm'], 'gla_w_out': out['gla_w_out'], 'mlp_w_in': out['mlp_w_in'], 'mlp_w_out': out['mlp_w_out'], 'loss_target': out['loss_target'], 'm_norm_mix': out['m_norm_mix'], 'm_norm_mlp': out['m_norm_mlp'], 'm_norm_final': out['m_norm_final'], 'm_pool_w': out['m_pool_w'], 'm_pool_scale': out['m_pool_scale'], 'm_gla_w_in': out['m_gla_w_in'], 'm_gla_w_up_f': out['m_gla_w_up_f'], 'm_gla_b_up_f': out['m_gla_b_up_f'], 'm_gla_w_up_b': out['m_gla_w_up_b'], 'm_gla_b_up_b': out['m_gla_b_up_b'], 'm_gla_g_norm': out['m_gla_g_norm'], 'm_gla_w_out': out['m_gla_w_out'], 'm_mlp_w_in': out['m_mlp_w_in'], 'm_mlp_w_out': out['m_mlp_w_out'], 'v_norm_mix': out['v_norm_mix'], 'v_norm_mlp': out['v_norm_mlp'], 'v_norm_final': out['v_norm_final'], 'v_pool_w': out['v_pool_w'], 'v_pool_scale': out['v_pool_scale'], 'v_gla_w_in': out['v_gla_w_in'], 'v_gla_w_up_f': out['v_gla_w_up_f'], 'v_gla_b_up_f': out['v_gla_b_up_f'], 'v_gla_w_up_b': out['v_gla_w_up_b'], 'v_gla_b_up_b': out['v_gla_b_up_b'], 'v_gla_g_norm': out['v_gla_g_norm'], 'v_gla_w_out': out['v_gla_w_out'], 'v_mlp_w_in': out['v_mlp_w_in'], 'v_mlp_w_out': out['v_mlp_w_out']}


def _loss(weights, diff, rest, loss_target):
    with _jax.named_scope("forward"):
        args = {**rest, TWIN_DIFF_INPUT: diff, **{k: w.astype(_WEIGHT_DTYPES[k]) for k, w in weights.items()}}
        y = _forward(args)
    with _jax.named_scope("loss_head"):
        err = _jnp.square(y.astype(_jnp.float32) - loss_target)
        return 0.5 * _jnp.sum(_jnp.mean(err, axis=-1)) if err.ndim else 0.5 * err


def _adamw(w, g, m, v):
    m = ADAM_B1 * m + (1.0 - ADAM_B1) * g
    v = ADAM_B2 * v + (1.0 - ADAM_B2) * _jnp.square(g)
    m_hat = m / (1.0 - ADAM_B1 ** ADAM_STEP)
    v_hat = v / (1.0 - ADAM_B2 ** ADAM_STEP)
    delta = -ADAM_LR * (m_hat / (_jnp.sqrt(v_hat) + ADAM_EPS) + ADAM_WD * w)
    return delta, m, v


def reference(x, norm_mix, norm_mlp, norm_final, pool_w, pool_scale, gla_w_in, gla_w_up_f, gla_b_up_f, gla_w_up_b, gla_b_up_b, gla_g_norm, gla_w_out, mlp_w_in, mlp_w_out, loss_target, m_norm_mix, m_norm_mlp, m_norm_final, m_pool_w, m_pool_scale, m_gla_w_in, m_gla_w_up_f, m_gla_b_up_f, m_gla_w_up_b, m_gla_b_up_b, m_gla_g_norm, m_gla_w_out, m_mlp_w_in, m_mlp_w_out, v_norm_mix, v_norm_mlp, v_norm_final, v_pool_w, v_pool_scale, v_gla_w_in, v_gla_w_up_f, v_gla_b_up_f, v_gla_w_up_b, v_gla_b_up_b, v_gla_g_norm, v_gla_w_out, v_mlp_w_in, v_mlp_w_out):
    given = dict(x=x, norm_mix=norm_mix, norm_mlp=norm_mlp, norm_final=norm_final, pool_w=pool_w, pool_scale=pool_scale, gla_w_in=gla_w_in, gla_w_up_f=gla_w_up_f, gla_b_up_f=gla_b_up_f, gla_w_up_b=gla_w_up_b, gla_b_up_b=gla_b_up_b, gla_g_norm=gla_g_norm, gla_w_out=gla_w_out, mlp_w_in=mlp_w_in, mlp_w_out=mlp_w_out, loss_target=loss_target, m_norm_mix=m_norm_mix, m_norm_mlp=m_norm_mlp, m_norm_final=m_norm_final, m_pool_w=m_pool_w, m_pool_scale=m_pool_scale, m_gla_w_in=m_gla_w_in, m_gla_w_up_f=m_gla_w_up_f, m_gla_b_up_f=m_gla_b_up_f, m_gla_w_up_b=m_gla_w_up_b, m_gla_b_up_b=m_gla_b_up_b, m_gla_g_norm=m_gla_g_norm, m_gla_w_out=m_gla_w_out, m_mlp_w_in=m_mlp_w_in, m_mlp_w_out=m_mlp_w_out, v_norm_mix=v_norm_mix, v_norm_mlp=v_norm_mlp, v_norm_final=v_norm_final, v_pool_w=v_pool_w, v_pool_scale=v_pool_scale, v_gla_w_in=v_gla_w_in, v_gla_w_up_f=v_gla_w_up_f, v_gla_b_up_f=v_gla_b_up_f, v_gla_w_up_b=v_gla_w_up_b, v_gla_b_up_b=v_gla_b_up_b, v_gla_g_norm=v_gla_g_norm, v_gla_w_out=v_gla_w_out, v_mlp_w_in=v_mlp_w_in, v_mlp_w_out=v_mlp_w_out)
    weights = {n: given[n] for n in TWIN_WEIGHTS}
    shared = {n: given[n] for n in SHARED_INPUTS}
    per_example = {n: given[n] for n in ['x']}
    grad_fn = _jax.value_and_grad(_loss, argnums=(0, 1))

    def one_microbatch(ex, loss_target):
        ex = dict(ex)
        diff = ex.pop(TWIN_DIFF_INPUT)
        return grad_fn(weights, diff, {**shared, **ex}, loss_target)

    if N_MICROBATCH == 1:
        loss, (grad_w, grad_x) = one_microbatch(per_example, given["loss_target"])
    else:
        def body(carry, xs):
            loss_sum, grad_sum = carry
            l_k, (gw_k, gx_k) = one_microbatch(xs[0], xs[1])
            with _jax.named_scope("update"):
                return (loss_sum + l_k, _jax.tree.map(_jnp.add, grad_sum, gw_k)), gx_k

        init = (_jnp.zeros((), _jnp.float32), _jax.tree.map(_jnp.zeros_like, weights))
        (loss, grad_w), grad_x = _jax.lax.scan(body, init, (per_example, given["loss_target"]))
    with _jax.named_scope("update"):
        delta_w, new_m, new_v = {}, {}, {}
        for n in TWIN_WEIGHTS:
            delta_w[n], new_m[n], new_v[n] = _adamw(weights[n], grad_w[n], given["m_" + n], given["v_" + n])
    return (loss, grad_x, *[grad_w[n] for n in TWIN_WEIGHTS], *[delta_w[n] for n in TWIN_WEIGHTS],
            *[new_m[n] for n in TWIN_WEIGHTS], *[new_v[n] for n in TWIN_WEIGHTS])
```

```python
import functools

import jax
import jax.numpy as jnp
from jax import lax
from jax.experimental import pallas as pl
from jax.experimental.pallas import tpu as pltpu

F32 = jnp.float32
BF16 = jnp.bfloat16
N_DEV = 8
N_PEER = N_DEV - 1
LANES = 128
VMEM_LIMIT_BYTES = 48 * 1024 * 1024
MESH_ID = pl.DeviceIdType.MESH

POOL_WINDOWS = (2, 4, 8, 16)
N_POOL_GROUPS = len(POOL_WINDOWS)
GLA_HEADS = 4
GATE_RANK = 16
GATE_TAU = 16.0
CHUNK = 64
EPS = 1e-6
ADAM_LR = 0.001
ADAM_B1 = 0.9
ADAM_B2 = 0.999
ADAM_EPS = 1e-08
ADAM_WD = 0.01
ADAM_STEP = 10

NN = ((1,), (0,))
NT = ((1,), (1,))
TN = ((0,), (0,))


def _tile(dim, pref):
    if dim <= pref:
        return dim
    t = (pref // LANES) * LANES
    while t > LANES and dim % t:
        t -= LANES
    assert dim % t == 0, (dim, pref)
    return t


def _row_tile(rows, pref):
    if rows <= pref:
        return rows
    t = max(8, pref // 8 * 8)
    while t > 8 and rows % t:
        t -= 8
    assert rows % t == 0, (rows, pref)
    return t


def _params(*sem):
    return pltpu.CompilerParams(dimension_semantics=sem, vmem_limit_bytes=VMEM_LIMIT_BYTES)


def _dot(a, b, dims):
    return lax.dot_general(a, b, (dims, ((), ())), preferred_element_type=F32)


def _mm(name, a, b, *, grid, a_spec, b_spec, o_spec, acc_shape, outs, dims, extras=(), epi=None, into=None):
    n_ex, n_out, nk = len(extras), len(outs), grid[2]

    def body(*refs):
        a_ref, b_ref = refs[0], refs[1]
        ex = refs[2:2 + n_ex]
        o_refs = refs[2 + n_ex + (into is not None):2 + n_ex + (into is not None) + n_out]
        acc = refs[-1]
        k = pl.program_id(2)

        @pl.when(k == 0)
        def _():
            acc[...] = jnp.zeros_like(acc)

        acc[...] += _dot(a_ref[...].astype(BF16), b_ref[...].astype(BF16), dims)

        @pl.when(k == nk - 1)
        def _():
            res = epi(acc[...], *[e[...] for e in ex]) if epi is not None else (acc[...],)
            for o, r in zip(o_refs, res):
                o[...] = r.astype(o.dtype)

    operands = [a, b] + [e for e, _ in extras]
    in_specs = [a_spec, b_spec] + [s for _, s in extras]
    aliases = {}
    if into is not None:
        operands.append(into)
        in_specs.append(pl.BlockSpec(memory_space=pl.ANY))
        aliases = {len(operands) - 1: 0}
    res = pl.pallas_call(
        body, name=name, grid=grid, in_specs=in_specs, out_specs=[o_spec] * n_out,
        out_shape=[jax.ShapeDtypeStruct(s, d) for s, d in outs],
        scratch_shapes=[pltpu.VMEM(acc_shape, F32)], input_output_aliases=aliases,
        compiler_params=_params("parallel", "parallel", "arbitrary"),
    )(*operands)
    return res


def _spec2(rows, cols, fn):
    return pl.BlockSpec((rows, cols), fn)


def _rowwise(name, fn, *, n_rows, tr, rows, bcast=(), outs=(), accs=()):
    n_r, n_b, n_o, n_a = len(rows), len(bcast), len(outs), len(accs)

    def body(*refs):
        r_refs, b_refs = refs[:n_r], refs[n_r:n_r + n_b]
        o_refs, a_refs = refs[n_r + n_b:n_r + n_b + n_o], refs[n_r + n_b + n_o:]
        res = fn(*[r[...] for r in r_refs], *[b[...] for b in b_refs])
        for o, r in zip(o_refs, res[:n_o]):
            o[...] = r.astype(o.dtype)
        if n_a:
            @pl.when(pl.program_id(0) == 0)
            def _():
                for acc in a_refs:
                    acc[...] = jnp.zeros_like(acc)

            for acc, r in zip(a_refs, res[n_o:]):
                acc[...] += r

    in_specs = [pl.BlockSpec((tr, w), functools.partial(lambda i, cb: (i, cb), cb=cb)) for _, w, cb in rows]
    in_specs += [pl.BlockSpec(b.shape, lambda i: (0, 0)) for b in bcast]
    out_specs = [pl.BlockSpec((tr, w), lambda i: (i, 0)) for w, _ in outs]
    out_specs += [pl.BlockSpec(s, lambda i: (0, 0)) for s in accs]
    out_shape = [jax.ShapeDtypeStruct((n_rows, w), d) for w, d in outs]
    out_shape += [jax.ShapeDtypeStruct(s, F32) for s in accs]
    return pl.pallas_call(
        body, name=name, grid=(n_rows // tr,), in_specs=in_specs, out_specs=out_specs, out_shape=out_shape,
        compiler_params=_params("arbitrary"),
    )(*[r for r, _, _ in rows], *bcast)


def _colsum(t):
    return jnp.sum(t, axis=0, keepdims=True)


def _rms_fwd(name, h, g, dtype, tr):
    def fn(x, gain):
        r = lax.rsqrt(jnp.mean(x * x, axis=-1, keepdims=True) + EPS)
        return (x * r * gain,)

    d = h.shape[1]
    return _rowwise(name, fn, n_rows=h.shape[0], tr=tr, rows=[(h, d, 0)], bcast=[g], outs=[(d, dtype)])[0]


def _rms_bwd(name, h, g, dhn, dres, tr):
    def fn(x, dy, dr, gain):
        r = lax.rsqrt(jnp.mean(x * x, axis=-1, keepdims=True) + EPS)
        xhat = x * r
        gd = gain * dy
        dh = r * (gd - xhat * jnp.mean(gd * xhat, axis=-1, keepdims=True)) + dr
        return dh, dh, _colsum(dy * xhat)

    d = h.shape[1]
    return _rowwise(name, fn, n_rows=h.shape[0], tr=tr, rows=[(h, d, 0), (dhn, d, 0), (dres, d, 0)], bcast=[g],
                    outs=[(d, F32), (d, BF16)], accs=[(1, d)])


def _loss_head(name, h, target, g, tr):
    d = h.shape[1]

    def fn(x, tgt, gain):
        r = lax.rsqrt(jnp.mean(x * x, axis=-1, keepdims=True) + EPS)
        xhat = x * r
        err = xhat * gain - tgt
        loss = 0.5 * jnp.sum(jnp.mean(err * err, axis=-1, keepdims=True), axis=0, keepdims=True)
        dy = err / d
        gd = gain * dy
        dh = r * (gd - xhat * jnp.mean(gd * xhat, axis=-1, keepdims=True))
        return dh, dh, jnp.broadcast_to(loss, (1, LANES)), _colsum(dy * xhat)

    return _rowwise(name, fn, n_rows=h.shape[0], tr=tr, rows=[(h, d, 0), (target, d, 0)], bcast=[g],
                    outs=[(d, F32), (d, BF16)], accs=[(1, LANES), (1, d)])


def _shift_rows(h, k, seq):
    if k == 0:
        return h
    rolled = pltpu.roll(h, (-k) % seq, 0)
    t = lax.broadcasted_iota(jnp.int32, h.shape, 0)
    return jnp.where((t + k >= 0) & (t + k < seq), rolled, 0.0)


def _window_halves(h, half, seq):
    right = h
    left = _shift_rows(h, -1, seq)
    step = 1
    while step < half:
        right = right + _shift_rows(right, step, seq)
        left = left + _shift_rows(left, -step, seq)
        step *= 2
    return left + right


def _window_count(shape, half, seq):
    t = lax.broadcasted_iota(jnp.int32, shape, 0)
    return (jnp.minimum(t + half, seq) - jnp.maximum(t - half, 0)).astype(F32)


def _pool_slabs(name, src, out_dtype, *, n_seq, seq, backward):
    d = src.shape[1]
    group = d // N_POOL_GROUPS
    tc = _tile(group, 256)

    def body(s_ref, o_ref):
        g = (pl.program_id(1) * tc) // group
        for gi, w in enumerate(POOL_WINDOWS):
            half = w // 2

            @pl.when(g == gi)
            def _():
                v = s_ref[...].astype(F32)
                cnt = _window_count(v.shape, half, seq)
                if backward:
                    u = v / cnt
                    res = (_window_halves(u, half, seq) - _shift_rows(u, -half, seq) + _shift_rows(u, half, seq)) - v
                else:
                    res = _window_halves(v, half, seq) / cnt - v
                o_ref[...] = res.astype(o_ref.dtype)

    spec = pl.BlockSpec((seq, tc), lambda b, j: (b, j))
    return pl.pallas_call(
        body, name=name, grid=(n_seq, d // tc), in_specs=[spec], out_specs=spec,
        out_shape=jax.ShapeDtypeStruct(src.shape, out_dtype), compiler_params=_params("parallel", "parallel"),
    )(src)


def _split3(x):
    x1 = x.astype(BF16)
    r1 = x - x1.astype(F32)
    x2 = r1.astype(BF16)
    x3 = (r1 - x2.astype(F32)).astype(BF16)
    return x1, x2, x3


def _tri_dot(tri, x):
    x1, x2, x3 = _split3(x)
    return _dot(tri, x1, NN) + _dot(tri, x2, NN) + _dot(tri, x3, NN)


def _chunk_terms(q_ref, k_ref, g_ref, rev, scale):
    c = q_ref.shape[0]
    ri = lax.broadcasted_iota(jnp.int32, (c, c), 0)
    ci = lax.broadcasted_iota(jnp.int32, (c, c), 1)
    causal = (ci >= ri) if rev else (ci <= ri)
    mid = c // 2 if rev else c // 2 - 1
    last = 0 if rev else c - 1
    b = _tri_dot(causal.astype(BF16), g_ref[...])
    b_mid, b_last = b[mid:mid + 1, :], b[last:last + 1, :]
    e_qm, e_km, e_qi, e_ks = jnp.exp(b - b_mid), jnp.exp(b_mid - b), jnp.exp(b), jnp.exp(b_last - b)
    q = q_ref[...].astype(F32) * scale
    k = k_ref[...].astype(F32)
    return dict(causal=causal, mid=mid, last=last, decay=jnp.exp(b_last), e_qm=e_qm, e_km=e_km, e_qi=e_qi, e_ks=e_ks,
                qm=q * e_qm, km=k * e_km, qi=q * e_qi, ks=k * e_ks)


def _gla_specs(n_chunks, heads, dk, dv, key_dim, rev, backward):
    def chunk(n):
        scan_pos = (n_chunks - 1 - n) if backward else n
        return (n_chunks - 1 - scan_pos) if rev else scan_pos

    def at(width, col0):
        return pl.BlockSpec((CHUNK, width), lambda b, h, n: (b * n_chunks + chunk(n), col0 // width + h))

    state = pl.BlockSpec((None, dv, dk), lambda b, h, n: ((b * heads + h) * n_chunks + chunk(n), 0, 0))
    return at(dk, 0), at(dk, key_dim), at(dv, 2 * key_dim), at(dk, 0), at(dv, 0), state


def _gla_fwd(name, pm, log_a, *, rev, n_seq, seq, key_dim, val_dim):
    heads, n_chunks = GLA_HEADS, seq // CHUNK
    dk, dv = key_dim // heads, val_dim // heads
    scale = dk ** -0.5
    q_spec, k_spec, v_spec, g_spec, o_spec, st_spec = _gla_specs(n_chunks, heads, dk, dv, key_dim, rev, False)

    def body(q_ref, k_ref, v_ref, g_ref, o_ref, st_ref, state):
        @pl.when(pl.program_id(2) == 0)
        def _():
            state[...] = jnp.zeros_like(state)

        t = _chunk_terms(q_ref, k_ref, g_ref, rev, scale)
        v = v_ref[...]
        p = jnp.where(t["causal"], _dot(t["qm"].astype(BF16), t["km"].astype(BF16), NT), 0.0)
        entering = state[...].astype(BF16)
        st_ref[...] = entering
        o_ref[...] = _dot(p.astype(BF16), v, NN) + _dot(t["qi"].astype(BF16), entering, NT)
        state[...] = t["decay"] * state[...] + _dot(v, t["ks"].astype(BF16), TN)

    tokens = n_seq * seq
    return pl.pallas_call(
        body, name=name, grid=(n_seq, heads, n_chunks), in_specs=[q_spec, k_spec, v_spec, g_spec],
        out_specs=[o_spec, st_spec],
        out_shape=[jax.ShapeDtypeStruct((tokens, val_dim), F32),
                   jax.ShapeDtypeStruct((n_seq * heads * n_chunks, dv, dk), BF16)],
        scratch_shapes=[pltpu.VMEM((dv, dk), F32)], compiler_params=_params("parallel", "parallel", "arbitrary"),
    )(pm, pm, pm, log_a)


def _gla_bwd(name, pm, log_a, states, d_o, *, rev, n_seq, seq, key_dim, val_dim):
    heads, n_chunks = GLA_HEADS, seq // CHUNK
    dk, dv = key_dim // heads, val_dim // heads
    scale = dk ** -0.5
    q_spec, k_spec, v_spec, g_spec, o_spec, st_spec = _gla_specs(n_chunks, heads, dk, dv, key_dim, rev, True)

    def body(q_ref, k_ref, v_ref, g_ref, do_ref, st_ref, dq_ref, dk_ref, dv_ref, dg_ref, dstate):
        @pl.when(pl.program_id(2) == 0)
        def _():
            dstate[...] = jnp.zeros_like(dstate)

        t = _chunk_terms(q_ref, k_ref, g_ref, rev, scale)
        causal = t["causal"]
        v, d_out, entering = v_ref[...], do_ref[...], st_ref[...]
        qm, km, qi, ks = (t[n].astype(BF16) for n in ("qm", "km", "qi", "ks"))
        dst = dstate[...]
        dst16 = dst.astype(BF16)
        p = jnp.where(causal, _dot(qm, km, NT), 0.0).astype(BF16)
        dp = jnp.where(causal, _dot(d_out, v, NT), 0.0).astype(BF16)
        dv_ref[...] = _dot(p, d_out, TN) + _dot(ks, dst16, NT)
        dqm, dkm = _dot(dp, km, NN), _dot(dp, qm, TN)
        dqi, dks = _dot(d_out, entering, NN), _dot(v, dst16, NN)
        d_decay = _colsum(dst * entering.astype(F32))
        dq_ref[...] = (dqi * t["e_qi"] + dqm * t["e_qm"]) * scale
        dk_ref[...] = dkm * t["e_km"] + dks * t["e_ks"]
        qm_term, km_term = dqm * t["qm"], dkm * t["km"]
        ks_term = dks * t["ks"]
        db = dqi * t["qi"] + qm_term - km_term - ks_term
        row = lax.broadcasted_iota(jnp.int32, db.shape, 0)
        db = db + jnp.where(row == t["mid"], _colsum(km_term - qm_term), 0.0)
        db = db + jnp.where(row == t["last"], _colsum(ks_term) + d_decay * t["decay"], 0.0)
        ri = lax.broadcasted_iota(jnp.int32, causal.shape, 0)
        ci = lax.broadcasted_iota(jnp.int32, causal.shape, 1)
        anti = (ci <= ri) if rev else (ci >= ri)
        dg_ref[...] = _tri_dot(anti.astype(BF16), db)
        dstate[...] = t["decay"] * dst + _dot(d_out, qi, TN)

    tokens = n_seq * seq
    kd = jax.ShapeDtypeStruct((tokens, key_dim), F32)
    return pl.pallas_call(
        body, name=name, grid=(n_seq, heads, n_chunks),
        in_specs=[q_spec, k_spec, v_spec, g_spec, o_spec, st_spec], out_specs=[g_spec, g_spec, o_spec, g_spec],
        out_shape=[kd, kd, jax.ShapeDtypeStruct((tokens, val_dim), F32), kd],
        scratch_shapes=[pltpu.VMEM((dv, dk), F32)], compiler_params=_params("parallel", "parallel", "arbitrary"),
    )(pm, pm, pm, log_a, d_o, states)


def _sigmoid_neg(z):
    e = jnp.exp(-jnp.abs(z))
    return jnp.where(z >= 0, e / (1.0 + e), 1.0 / (1.0 + e))


def _gate_fwd(name, pr, wf, bf, wb, bb, tr):
    kd = wf.shape[1]

    def fn(r, w_f, b_f, w_b, b_b):
        r16 = r.astype(BF16)
        out = []
        for w, b in ((w_f, b_f), (w_b, b_b)):
            z = _dot(r16, w.astype(BF16), NN) + b
            out.append((jnp.minimum(z, 0.0) - jnp.log1p(jnp.exp(-jnp.abs(z)))) / GATE_TAU)
        return tuple(out)

    return _rowwise(name, fn, n_rows=pr.shape[0], tr=tr, rows=[(pr, LANES, 0)], bcast=[wf, bf, wb, bb],
                    outs=[(kd, F32), (kd, F32)])


def _gate_bwd(name, pr, wf, bf, wb, bb, dgf, dgb, tr):
    kd = wf.shape[1]

    def fn(r, d_f, d_b, w_f, b_f, w_b, b_b):
        r16 = r.astype(BF16)
        dpr = jnp.zeros(r.shape, F32)
        sums = []
        for w, b, dg in ((w_f, b_f, d_f), (w_b, b_b, d_b)):
            w16 = w.astype(BF16)
            z = _dot(r16, w16, NN) + b
            dz = dg / GATE_TAU * _sigmoid_neg(z)
            dz16 = dz.astype(BF16)
            dpr = dpr + _dot(dz16, w16, NT)
            sums += [_dot(r16, dz16, TN), _colsum(dz)]
        return (dpr, *sums)

    return _rowwise(name, fn, n_rows=pr.shape[0], tr=tr, rows=[(pr, LANES, 0), (dgf, kd, 0), (dgb, kd, 0)],
                    bcast=[wf, bf, wb, bb], outs=[(LANES, BF16)], accs=[(LANES, kd), (1, kd), (LANES, kd), (1, kd)])


def _head_norm_fwd(name, of, ob, pm, gn, *, val_dim, gate_block, tr):
    dv = val_dim // GLA_HEADS

    def fn(o_f, o_b, gate, gain):
        out = []
        for h in range(GLA_HEADS):
            sl = slice(h * dv, (h + 1) * dv)
            o = o_f[:, sl] + o_b[:, sl]
            gt = gate[:, sl].astype(F32)
            on = o * lax.rsqrt(jnp.mean(o * o, axis=-1, keepdims=True) + EPS) * gain
            out.append(on * (gt * jax.nn.sigmoid(gt)))
        return (jnp.concatenate(out, axis=1),)

    return _rowwise(name, fn, n_rows=of.shape[0], tr=tr, rows=[(of, val_dim, 0), (ob, val_dim, 0), (pm, val_dim, gate_block)],
                    bcast=[gn], outs=[(val_dim, BF16)])[0]


def _head_norm_bwd(name, of, ob, pm, gn, dz, *, val_dim, gate_block, tr):
    dv = val_dim // GLA_HEADS

    def fn(o_f, o_b, gate, d_z, gain):
        d_o, d_gate = [], []
        d_gain = jnp.zeros((1, dv), F32)
        for h in range(GLA_HEADS):
            sl = slice(h * dv, (h + 1) * dv)
            o = o_f[:, sl] + o_b[:, sl]
            gt = gate[:, sl].astype(F32)
            dzh = d_z[:, sl]
            r = lax.rsqrt(jnp.mean(o * o, axis=-1, keepdims=True) + EPS)
            ohat = o * r
            sg = jax.nn.sigmoid(gt)
            d_gate.append(dzh * (ohat * gain) * (sg * (1.0 + gt * (1.0 - sg))))
            don = dzh * (gt * sg)
            d_gain = d_gain + _colsum(don * ohat)
            gd = gain * don
            d_o.append(r * (gd - ohat * jnp.mean(gd * ohat, axis=-1, keepdims=True)))
        return jnp.concatenate(d_o, axis=1), jnp.concatenate(d_gate, axis=1), d_gain

    return _rowwise(name, fn, n_rows=of.shape[0], tr=tr,
                    rows=[(of, val_dim, 0), (ob, val_dim, 0), (pm, val_dim, gate_block), (dz, val_dim, 0)], bcast=[gn],
                    outs=[(val_dim, BF16), (val_dim, BF16)], accs=[(1, dv)])


def _proj_grad(name, dqf, dqb, dkf, dkb, dvf, dvb, dgate, *, key_dim, val_dim, tr):
    def fn(q1, q2, k1, k2, v1, v2, gt):
        return (jnp.concatenate([(q1 + q2).astype(BF16), (k1 + k2).astype(BF16), (v1 + v2).astype(BF16), gt], axis=1),)

    rows = [(dqf, key_dim, 0), (dqb, key_dim, 0), (dkf, key_dim, 0), (dkb, key_dim, 0), (dvf, val_dim, 0),
            (dvb, val_dim, 0), (dgate, val_dim, 0)]
    return _rowwise(name, fn, n_rows=dqf.shape[0], tr=tr, rows=rows, outs=[(2 * key_dim + 2 * val_dim, BF16)])[0]


def _pool_out_grad(name, dh, y_pre, scale, tr):
    def fn(d, y, s):
        return d * s, _colsum(d * y)

    dm = dh.shape[1]
    return _rowwise(name, fn, n_rows=dh.shape[0], tr=tr, rows=[(dh, dm, 0), (y_pre, dm, 0)], bcast=[scale],
                    outs=[(dm, BF16)], accs=[(1, dm)])


def _mesh_place():
    x, y, c = lax.axis_index("x"), lax.axis_index("y"), lax.axis_index("c")
    return x, y, c, 4 * x + 2 * y + c


def _peer(x, y, c, p):
    px, py, pc = (x if not p & 4 else 1 - x), (y if not p & 2 else 1 - y), (c if not p & 1 else 1 - c)
    return (px, py, pc), 4 * px + 2 * py + pc


def _exchange(name, arrs, scatter):
    n = len(arrs)

    def body(*refs):
        ins, outs = refs[:n], refs[n:2 * n]
        send_sems, recv_sems, local_sems = refs[2 * n:]
        x, y, c, me = _mesh_place()
        local = [pltpu.make_async_copy(ins[a].at[me] if scatter else ins[a], outs[a].at[me], local_sems.at[a])
                 for a in range(n)]
        for cp in local:
            cp.start()

        def copy(a, p, arriving):
            peer, peer_block = _peer(x, y, c, p)
            src = ins[a].at[peer_block] if scatter else ins[a]
            dst = outs[a].at[peer_block if arriving else me]
            k = a * N_PEER + p - 1
            return pltpu.make_async_remote_copy(src_ref=src, dst_ref=dst, send_sem=send_sems.at[k], recv_sem=recv_sems.at[k],
                                                device_id=peer, device_id_type=MESH_ID)

        sends = [copy(a, p, False) for a in range(n) for p in range(1, N_DEV)]
        for cp in sends:
            cp.start()
        for a in range(n):
            for p in range(1, N_DEV):
                copy(a, p, True).wait_recv()
        for cp in sends:
            cp.wait_send()
        for cp in local:
            cp.wait()

    hbm = pl.BlockSpec(memory_space=pl.ANY)
    out_shape = [jax.ShapeDtypeStruct(a.shape if scatter else (N_DEV,) + a.shape, a.dtype) for a in arrs]
    return pl.pallas_call(
        body, name=name, in_specs=[hbm] * n, out_specs=[hbm] * n, out_shape=out_shape,
        scratch_shapes=[pltpu.SemaphoreType.DMA((n * N_PEER,)), pltpu.SemaphoreType.DMA((n * N_PEER,)),
                        pltpu.SemaphoreType.DMA((n,))],
    )(*arrs)


def _all_reduce_small(name, pack):
    def body(in_ref, out_ref, parts, send_sems, recv_sems):
        x, y, c, me = _mesh_place()
        parts[me] = in_ref[...]

        def copy(p, arriving):
            peer, peer_block = _peer(x, y, c, p)
            return pltpu.make_async_remote_copy(src_ref=in_ref, dst_ref=parts.at[peer_block if arriving else me],
                                                send_sem=send_sems.at[p - 1], recv_sem=recv_sems.at[p - 1],
                                                device_id=peer, device_id_type=MESH_ID)

        sends = [copy(p, False) for p in range(1, N_DEV)]
        for cp in sends:
            cp.start()
        for p in range(1, N_DEV):
            copy(p, True).wait_recv()
        for cp in sends:
            cp.wait_send()
        total = parts[0]
        for j in range(1, N_DEV):
            total = total + parts[j]
        out_ref[...] = total

    vmem = pl.BlockSpec(memory_space=pltpu.VMEM)
    return pl.pallas_call(
        body, name=name, in_specs=[vmem], out_specs=vmem, out_shape=jax.ShapeDtypeStruct(pack.shape, F32),
        scratch_shapes=[pltpu.VMEM((N_DEV,) + pack.shape, F32), pltpu.SemaphoreType.DMA((N_PEER,)),
                        pltpu.SemaphoreType.DMA((N_PEER,))],
        compiler_params=pltpu.CompilerParams(vmem_limit_bytes=VMEM_LIMIT_BYTES),
    )(pack)


def _adamw_math(w, g, m, v):
    m = ADAM_B1 * m + (1.0 - ADAM_B1) * g
    v = ADAM_B2 * v + (1.0 - ADAM_B2) * jnp.square(g)
    m_hat = m / (1.0 - ADAM_B1 ** ADAM_STEP)
    v_hat = v / (1.0 - ADAM_B2 ** ADAM_STEP)
    delta = -ADAM_LR * (m_hat / (jnp.sqrt(v_hat) + ADAM_EPS) + ADAM_WD * w)
    return delta, m, v


def _adamw_shard(name, w, m, v, parts):
    shape = w.shape
    cols = shape[-1]
    rows = w.size // cols
    tr = _row_tile(rows, (256 * 1024) // cols)

    def body(w_ref, m_ref, v_ref, p_ref, g_ref, d_ref, nm_ref, nv_ref):
        g = p_ref[0].astype(F32)
        for j in range(1, N_DEV):
            g = g + p_ref[j].astype(F32)
        delta, new_m, new_v = _adamw_math(w_ref[...], g, m_ref[...], v_ref[...])
        g_ref[...], d_ref[...], nm_ref[...], nv_ref[...] = g, delta, new_m, new_v

    spec = pl.BlockSpec((tr, cols), lambda i: (i, 0))
    res = pl.pallas_call(
        body, name=name, grid=(rows // tr,),
        in_specs=[spec, spec, spec, pl.BlockSpec((N_DEV, tr, cols), lambda i: (0, i, 0))], out_specs=[spec] * 4,
        out_shape=[jax.ShapeDtypeStruct((rows, cols), F32)] * 4, compiler_params=_params("parallel"),
    )(w.reshape(rows, cols), m.reshape(rows, cols), v.reshape(rows, cols), parts.reshape(N_DEV, rows, cols))
    return [r.reshape(shape) for r in res]


def _adamw_small(name, w, m, v, g):
    shape = w.shape
    two_d = (1, shape[0]) if len(shape) == 1 else (w.size // shape[-1], shape[-1])

    def body(w_ref, m_ref, v_ref, g_ref, d_ref, nm_ref, nv_ref):
        d_ref[...], nm_ref[...], nv_ref[...] = _adamw_math(w_ref[...], g_ref[...], m_ref[...], v_ref[...])

    res = pl.pallas_call(body, name=name, out_shape=[jax.ShapeDtypeStruct(two_d, F32)] * 3)(
        w.reshape(two_d), m.reshape(two_d), v.reshape(two_d), g.reshape(two_d))
    return [r.reshape(shape) for r in res]


def _mlp_fwd(tag, layer, h_in, gain, w1g, w2g, tm, tr):
    tokens, d = h_in.shape
    fs = w1g.shape[3]
    f = N_DEV * fs
    hn = _rms_fwd(f"rms_mlp{tag}", h_in, gain, BF16, tr)
    tn, tk = _tile(fs, 1024), _tile(d, 512)
    per = fs // tn

    def relu2(acc):
        a = jnp.maximum(acc, 0.0)
        return acc, a * a

    a_pre, act = _mm(
        f"mlp_up{tag}", hn, w1g, grid=(tokens // tm, f // tn, d // tk), a_spec=_spec2(tm, tk, lambda m, n, k: (m, k)),
        b_spec=pl.BlockSpec((None, None, tk, tn), lambda m, n, k: (n // per, layer, k, n % per)),
        o_spec=_spec2(tm, tn, lambda m, n, k: (m, n)), acc_shape=(tm, tn), outs=[((tokens, f), BF16)] * 2, dims=NN, epi=relu2)
    tn2, tk2 = _tile(d, 1024), _tile(fs, 512)
    kper = fs // tk2
    h_out = _mm(
        f"mlp_down{tag}", act, w2g, grid=(tokens // tm, d // tn2, f // tk2), a_spec=_spec2(tm, tk2, lambda m, n, k: (m, k)),
        b_spec=pl.BlockSpec((None, None, tk2, tn2), lambda m, n, k: (k // kper, layer, k % kper, n)),
        o_spec=_spec2(tm, tn2, lambda m, n, k: (m, n)), acc_shape=(tm, tn2), outs=[((tokens, d), F32)], dims=NN,
        extras=[(h_in, _spec2(tm, tn2, lambda m, n, k: (m, n)))], epi=lambda acc, res: (res + acc,))[0]
    return hn, a_pre, act, h_out


def _mlp_bwd(tag, layer, h_in, gain, hn, a_pre, act, dh, dh16, w1g, w2g, dw1_into, dw2_into, tm, tr):
    tokens, d = h_in.shape
    fs = w1g.shape[3]
    f = N_DEV * fs
    n_layers = w1g.shape[1]
    tn, tk = _tile(fs, 1024), _tile(d, 512)
    nper = fs // tn
    da = _mm(
        f"mlp_dact{tag}", dh16, w2g, grid=(tokens // tm, f // tn, d // tk), a_spec=_spec2(tm, tk, lambda m, n, k: (m, k)),
        b_spec=pl.BlockSpec((None, None, tn, tk), lambda m, n, k: (n // nper, layer, n % nper, k)),
        o_spec=_spec2(tm, tn, lambda m, n, k: (m, n)), acc_shape=(tm, tn), outs=[((tokens, f), BF16)], dims=NT,
        extras=[(a_pre, _spec2(tm, tn, lambda m, n, k: (m, n)))],
        epi=lambda acc, pre: (acc * (2.0 * jnp.maximum(pre.astype(F32), 0.0)),))[0]
    tkt = _tile(tokens, 512)
    tm2, tn2 = _tile(fs, 1024), _tile(d, 1024)
    mper = fs // tm2
    dw2 = _mm(
        f"mlp_dw_out{tag}", act, dh16, grid=(f // tm2, d // tn2, tokens // tkt), a_spec=_spec2(tkt, tm2, lambda m, n, k: (k, m)),
        b_spec=_spec2(tkt, tn2, lambda m, n, k: (k, n)),
        o_spec=pl.BlockSpec((None, None, tm2, tn2), lambda m, n, k: (m // mper, layer, m % mper, n)), acc_shape=(tm2, tn2),
        outs=[((N_DEV, n_layers, fs, d), BF16)], dims=TN, into=dw2_into)[0]
    tm3 = _tile(d, 1024)
    dw1 = _mm(
        f"mlp_dw_in{tag}", hn, da, grid=(d // tm3, f // tn, tokens // tkt), a_spec=_spec2(tkt, tm3, lambda m, n, k: (k, m)),
        b_spec=_spec2(tkt, tn, lambda m, n, k: (k, n)),
        o_spec=pl.BlockSpec((None, None, tm3, tn), lambda m, n, k: (n // nper, layer, m, n % nper)), acc_shape=(tm3, tn),
        outs=[((N_DEV, n_layers, d, fs), BF16)], dims=TN, into=dw1_into)[0]
    tn4, tk4 = _tile(d, 1024), _tile(fs, 512)
    kper = fs // tk4
    dhn = _mm(
        f"mlp_dx{tag}", da, w1g, grid=(tokens // tm, d // tn4, f // tk4), a_spec=_spec2(tm, tk4, lambda m, n, k: (m, k)),
        b_spec=pl.BlockSpec((None, None, tn4, tk4), lambda m, n, k: (k // kper, layer, n, k % kper)),
        o_spec=_spec2(tm, tn4, lambda m, n, k: (m, n)), acc_shape=(tm, tn4), outs=[((tokens, d), F32)], dims=NT)[0]
    dh_in, dh_in16, dgain = _rms_bwd(f"rms_mlp_bwd{tag}", h_in, gain, dhn, dh, tr)
    return dh_in, dh_in16, dgain, dw1, dw2


def _plain_mm(name, a, b, dims, out_dtype, *, tm, tn_pref=1024, tk_pref=512, extras=None, epi=None):
    if dims == NN:
        (m, k), n = a.shape, b.shape[1]
    elif dims == NT:
        (m, k), n = a.shape, b.shape[0]
    else:
        (k, m), n = a.shape, b.shape[1]
    tm, tn, tk = _tile(m, tm), _tile(n, tn_pref), _tile(k, tk_pref)
    a_spec = _spec2(tk, tm, lambda i, j, l: (l, i)) if dims == TN else _spec2(tm, tk, lambda i, j, l: (i, l))
    b_spec = _spec2(tn, tk, lambda i, j, l: (j, l)) if dims == NT else _spec2(tk, tn, lambda i, j, l: (l, j))
    o_spec = _spec2(tm, tn, lambda i, j, l: (i, j))
    ex = [(e, o_spec) for e in (extras or [])]
    return _mm(name, a, b, grid=(m // tm, n // tn, k // tk), a_spec=a_spec, b_spec=b_spec, o_spec=o_spec, acc_shape=(tm, tn),
               outs=[((m, n), out_dtype)], dims=dims, extras=ex, epi=epi)[0]


def kernel(x, norm_mix, norm_mlp, norm_final, pool_w, pool_scale, gla_w_in, gla_w_up_f, gla_b_up_f, gla_w_up_b, gla_b_up_b, gla_g_norm, gla_w_out, mlp_w_in, mlp_w_out, loss_target, m_norm_mix, m_norm_mlp, m_norm_final, m_pool_w, m_pool_scale, m_gla_w_in, m_gla_w_up_f, m_gla_b_up_f, m_gla_w_up_b, m_gla_b_up_b, m_gla_g_norm, m_gla_w_out, m_mlp_w_in, m_mlp_w_out, v_norm_mix, v_norm_mlp, v_norm_final, v_pool_w, v_pool_scale, v_gla_w_in, v_gla_w_up_f, v_gla_b_up_f, v_gla_w_up_b, v_gla_b_up_b, v_gla_g_norm, v_gla_w_out, v_mlp_w_in, v_mlp_w_out):
    n_seq, seq, d = x.shape
    tokens = n_seq * seq
    group = d // N_POOL_GROUPS
    key_dim, val_dim = d // 2, d
    dk, dv = key_dim // GLA_HEADS, val_dim // GLA_HEADS
    n_main = 2 * key_dim + 2 * val_dim
    proj_shard = gla_w_in.shape[2]
    proj_width = N_DEV * proj_shard
    kd_shard = key_dim // N_DEV
    n_layers, _, fs = mlp_w_in.shape
    assert proj_width == n_main + 2 * GATE_RANK and seq % CHUNK == 0 and 2 * GATE_RANK <= LANES
    me = 4 * lax.axis_index("x") + 2 * lax.axis_index("y") + lax.axis_index("c")
    tm = _tile(tokens, 1024)
    tr = _tile(tokens, 128)
    x2, target = x.reshape(tokens, d), loss_target.reshape(tokens, d)

    small_rows = 2 * GATE_RANK + 3
    small = jnp.concatenate([
        gla_w_up_f[0], gla_w_up_b[0], gla_b_up_f, gla_b_up_b,
        jnp.pad(gla_g_norm, ((0, 0), (0, kd_shard - gla_g_norm.shape[1])))], axis=0)
    pool_g, win_g, wout_g, w1g, w2g, small_g = _exchange(
        "gather_weights",
        [pool_w[0].astype(BF16), gla_w_in[0].astype(BF16), gla_w_out[0].astype(BF16), mlp_w_in.astype(BF16),
         mlp_w_out.astype(BF16), small], scatter=False)
    w_pool = pool_g.transpose(1, 0, 2, 3).reshape(N_POOL_GROUPS, group, group)
    w_in = win_g.transpose(1, 0, 2).reshape(d, proj_width)
    w_main = w_in[:, :n_main]
    w_r = jnp.pad(w_in[:, n_main:], ((0, 0), (0, LANES - 2 * GATE_RANK)))
    w_out = wout_g.reshape(val_dim, d)
    small_full = small_g.transpose(1, 0, 2).reshape(small_rows, key_dim)
    pad_rows = lambda w, r0: jnp.pad(w, ((r0, LANES - r0 - GATE_RANK), (0, 0)))
    w_up_f, w_up_b = pad_rows(small_full[:GATE_RANK], 0), pad_rows(small_full[GATE_RANK:2 * GATE_RANK], GATE_RANK)
    b_up_f, b_up_b = small_full[2 * GATE_RANK:2 * GATE_RANK + 1], small_full[2 * GATE_RANK + 1:2 * GATE_RANK + 2]
    g_norm = small_g[:, 2 * GATE_RANK + 2, :gla_g_norm.shape[1]].reshape(1, dv)

    hn0 = _rms_fwd("rms_mix0", x2, norm_mix[0:1], F32, tr)
    pooled = _pool_slabs("pool_fwd", hn0, BF16, n_seq=n_seq, seq=seq, backward=False)
    tn, tk = _tile(group, 1024), _tile(group, 512)
    nper, kper = group // tn, group // tk
    y_pre, h1 = _mm(
        "pool_mix", pooled, w_pool, grid=(tokens // tm, d // tn, group // tk),
        a_spec=_spec2(tm, tk, lambda m, n, k: (m, (n // nper) * kper + k)),
        b_spec=pl.BlockSpec((None, tk, tn), lambda m, n, k: (n // nper, k, n % nper)),
        o_spec=_spec2(tm, tn, lambda m, n, k: (m, n)), acc_shape=(tm, tn), outs=[((tokens, d), F32)] * 2, dims=NN,
        extras=[(x2, _spec2(tm, tn, lambda m, n, k: (m, n))), (pool_scale, _spec2(1, tn, lambda m, n, k: (0, n)))],
        epi=lambda acc, res, s: (acc, res + acc * s))
    hn1, a_pre0, act0, h2 = _mlp_fwd("0", 0, h1, norm_mlp[0:1], w1g, w2g, tm, tr)

    hn2 = _rms_fwd("rms_mix1", h2, norm_mix[1:2], BF16, tr)
    pm = _plain_mm("gla_proj", hn2, w_main, NN, BF16, tm=tm)
    pr = _plain_mm("gla_proj_gate", hn2, w_r, NN, F32, tm=tm)
    log_a_f, log_a_b = _gate_fwd("gla_gate", pr, w_up_f, b_up_f, w_up_b, b_up_b, tr)
    dims = dict(n_seq=n_seq, seq=seq, key_dim=key_dim, val_dim=val_dim)
    o_f, st_f = _gla_fwd("gla_scan_f", pm, log_a_f, rev=False, **dims)
    o_b, st_b = _gla_fwd("gla_scan_b", pm, log_a_b, rev=True, **dims)
    gate_block = (2 * key_dim + val_dim) // val_dim
    z = _head_norm_fwd("gla_head_norm", o_f, o_b, pm, g_norm, val_dim=val_dim, gate_block=gate_block, tr=tr)
    h3 = _plain_mm("gla_out", z, w_out, NN, F32, tm=tm, extras=[h2], epi=lambda acc, res: (res + acc,))
    hn3, a_pre1, act1, h4 = _mlp_fwd("1", 1, h3, norm_mlp[1:2], w1g, w2g, tm, tr)

    dh4, dh4_16, loss_row, d_norm_final = _loss_head("loss_head", h4, target, norm_final.reshape(1, d), tr)

    dh3, dh3_16, d_norm_mlp1, dw1, dw2 = _mlp_bwd("1", 1, h3, norm_mlp[1:2], hn3, a_pre1, act1, dh4, dh4_16, w1g, w2g,
                                                  None, None, tm, tr)
    dz = _plain_mm("gla_out_dx", dh3_16, w_out, NT, F32, tm=tm)
    dw_out = _plain_mm("gla_out_dw", z, dh3_16, TN, BF16, tm=1024)
    d_o, d_gate, d_g_norm = _head_norm_bwd("gla_head_norm_bwd", o_f, o_b, pm, g_norm, dz, val_dim=val_dim,
                                           gate_block=gate_block, tr=tr)
    dq_f, dk_f, dv_f, dla_f = _gla_bwd("gla_scan_bwd_f", pm, log_a_f, st_f, d_o, rev=False, **dims)
    dq_b, dk_b, dv_b, dla_b = _gla_bwd("gla_scan_bwd_b", pm, log_a_b, st_b, d_o, rev=True, **dims)
    dpr, dw_up_f, db_up_f, dw_up_b, db_up_b = _gate_bwd("gla_gate_bwd", pr, w_up_f, b_up_f, w_up_b, b_up_b, dla_f, dla_b, tr)
    dpm = _proj_grad("gla_proj_grad", dq_f, dq_b, dk_f, dk_b, dv_f, dv_b, d_gate, key_dim=key_dim, val_dim=val_dim, tr=tr)
    dw_main = _plain_mm("gla_proj_dw", hn2, dpm, TN, BF16, tm=1024)
    dw_r = _plain_mm("gla_proj_gate_dw", hn2, dpr, TN, BF16, tm=1024)
    dhn2 = _plain_mm("gla_proj_dx", dpm, w_main, NT, F32, tm=tm)
    dhn2 = _plain_mm("gla_proj_gate_dx", dpr, w_r, NT, F32, tm=tm, extras=[dhn2], epi=lambda acc, res: (res + acc,))
    dh2, dh2_16, d_norm_mix1 = _rms_bwd("rms_mix1_bwd", h2, norm_mix[1:2], dhn2, dh3, tr)

    dh1, _, d_norm_mlp0, dw1, dw2 = _mlp_bwd("0", 0, h1, norm_mlp[0:1], hn1, a_pre0, act0, dh2, dh2_16, w1g, w2g,
                                             dw1, dw2, tm, tr)
    dyp, d_pool_scale = _pool_out_grad("pool_out_grad", dh1, y_pre, pool_scale, tr)
    dpooled = _mm(
        "pool_mix_dx", dyp, w_pool, grid=(tokens // tm, d // tn, group // tk),
        a_spec=_spec2(tm, tk, lambda m, n, k: (m, (n // nper) * kper + k)),
        b_spec=pl.BlockSpec((None, tn, tk), lambda m, n, k: (n // nper, n % nper, k)),
        o_spec=_spec2(tm, tn, lambda m, n, k: (m, n)), acc_shape=(tm, tn), outs=[((tokens, d), F32)], dims=NT)[0]
    tmw, tkt = _tile(group, 1024), _tile(tokens, 512)
    mper = group // tmw
    dw_pool = _mm(
        "pool_mix_dw", pooled, dyp, grid=(d // tmw, group // tn, tokens // tkt),
        a_spec=_spec2(tkt, tmw, lambda m, n, k: (k, m)), b_spec=_spec2(tkt, tn, lambda m, n, k: (k, (m // mper) * nper + n)),
        o_spec=pl.BlockSpec((None, tmw, tn), lambda m, n, k: (m // mper, m % mper, n)), acc_shape=(tmw, tn),
        outs=[((N_POOL_GROUPS, group, group), BF16)], dims=TN)[0]
    dhn0 = _pool_slabs("pool_bwd", dpooled, F32, n_seq=n_seq, seq=seq, backward=True)
    dx, _, d_norm_mix0 = _rms_bwd("rms_mix0_bwd", x2, norm_mix[0:1], dhn0, dh1, tr)

    dw_in = jnp.concatenate([dw_main, dw_r[:, :2 * GATE_RANK]], axis=1)
    parts = _exchange(
        "scatter_grads",
        [dw_pool.reshape(N_POOL_GROUPS, N_DEV, group // N_DEV, group).transpose(1, 0, 2, 3),
         dw_in.reshape(d, N_DEV, proj_shard).transpose(1, 0, 2), dw_out.reshape(N_DEV, val_dim // N_DEV, d), dw1, dw2],
        scatter=True)
    pack_cols = key_dim
    rows_of = lambda a: a.reshape(-1, pack_cols)
    pieces = [rows_of(t) for t in (d_norm_mix0, d_norm_mix1, d_norm_mlp0, d_norm_mlp1, d_norm_final, d_pool_scale,
                                   dw_up_f[:GATE_RANK], dw_up_b[GATE_RANK:2 * GATE_RANK], db_up_f, db_up_b)]
    pieces += [jnp.pad(d_g_norm, ((0, 0), (0, pack_cols - dv))), jnp.pad(loss_row, ((0, 0), (0, pack_cols - LANES)))]
    n_rows = sum(p.shape[0] for p in pieces)
    pack = jnp.concatenate(pieces + [jnp.zeros((-n_rows % 8, pack_cols), F32)], axis=0)
    red = _all_reduce_small("reduce_small", pack)
    cuts, r0 = [], 0
    for p in pieces:
        cuts.append(red[r0:r0 + p.shape[0]])
        r0 += p.shape[0]
    per_d = d // pack_cols
    g_norm_mix = jnp.concatenate([cuts[0].reshape(1, d), cuts[1].reshape(1, d)], axis=0)
    g_norm_mlp = jnp.concatenate([cuts[2].reshape(1, d), cuts[3].reshape(1, d)], axis=0)
    g_norm_final = cuts[4].reshape(d)
    g_pool_scale = cuts[5].reshape(1, d)
    shard_cols = lambda t, width: lax.dynamic_slice_in_dim(t, me * width, width, axis=1)
    g_w_up_f, g_w_up_b = shard_cols(cuts[6], kd_shard)[None], shard_cols(cuts[7], kd_shard)[None]
    g_b_up_f, g_b_up_b = shard_cols(cuts[8], kd_shard), shard_cols(cuts[9], kd_shard)
    g_g_norm = shard_cols(cuts[10][:, :dv], dv // N_DEV)
    loss = cuts[11][0, 0]
    del per_d

    big = {}
    for nm, w, m, v, part in (("pool_w", pool_w, m_pool_w, v_pool_w, parts[0]), ("gla_w_in", gla_w_in, m_gla_w_in, v_gla_w_in, parts[1]),
                              ("gla_w_out", gla_w_out, m_gla_w_out, v_gla_w_out, parts[2]),
                              ("mlp_w_in", mlp_w_in, m_mlp_w_in, v_mlp_w_in, parts[3]),
                              ("mlp_w_out", mlp_w_out, m_mlp_w_out, v_mlp_w_out, parts[4])):
        big[nm] = _adamw_shard(f"adamw_{nm}", w, m, v, part)
    grads = {"norm_mix": g_norm_mix, "norm_mlp": g_norm_mlp, "norm_final": g_norm_final, "pool_scale": g_pool_scale,
             "gla_w_up_f": g_w_up_f, "gla_b_up_f": g_b_up_f, "gla_w_up_b": g_w_up_b, "gla_b_up_b": g_b_up_b, "gla_g_norm": g_g_norm}
    state = {"norm_mix": (norm_mix, m_norm_mix, v_norm_mix), "norm_mlp": (norm_mlp, m_norm_mlp, v_norm_mlp),
             "norm_final": (norm_final, m_norm_final, v_norm_final), "pool_scale": (pool_scale, m_pool_scale, v_pool_scale),
             "gla_w_up_f": (gla_w_up_f, m_gla_w_up_f, v_gla_w_up_f), "gla_b_up_f": (gla_b_up_f, m_gla_b_up_f, v_gla_b_up_f),
             "gla_w_up_b": (gla_w_up_b, m_gla_w_up_b, v_gla_w_up_b), "gla_b_up_b": (gla_b_up_b, m_gla_b_up_b, v_gla_b_up_b),
             "gla_g_norm": (gla_g_norm, m_gla_g_norm, v_gla_g_norm)}
    order = ["norm_mix", "norm_mlp", "norm_final", "pool_w", "pool_scale", "gla_w_in", "gla_w_up_f", "gla_b_up_f", "gla_w_up_b",
             "gla_b_up_b", "gla_g_norm", "gla_w_out", "mlp_w_in", "mlp_w_out"]
    out_g, out_d, out_m, out_v = [], [], [], []
    for nm in order:
        if nm in big:
            g, dl, new_m, new_v = big[nm]
        else:
            g = grads[nm]
            dl, new_m, new_v = _adamw_small(f"adamw_{nm}", *state[nm], g)
        out_g.append(g), out_d.append(dl), out_m.append(new_m), out_v.append(new_v)
    return (loss, dx.reshape(x.shape), *out_g, *out_d, *out_m, *out_v)
```

```python
import functools

import jax
import jax.numpy as jnp
from jax import lax
from jax.experimental import pallas as pl
from jax.experimental.pallas import tpu as pltpu

F32 = jnp.float32
BF16 = jnp.bfloat16
N_DEV = 8
N_PEER = N_DEV - 1
LANES = 128
VMEM_LIMIT_BYTES = 48 * 1024 * 1024
MESH_ID = pl.DeviceIdType.MESH

POOL_WINDOWS = (2, 4, 8, 16)
N_POOL_GROUPS = len(POOL_WINDOWS)
GLA_HEADS = 4
GATE_RANK = 16
GATE_TAU = 16.0
CHUNK = 64
EPS = 1e-6
ADAM_LR = 0.001
ADAM_B1 = 0.9
ADAM_B2 = 0.999
ADAM_EPS = 1e-08
ADAM_WD = 0.01
ADAM_STEP = 10

NN = ((1,), (0,))
NT = ((1,), (1,))
TN = ((0,), (0,))


def _tile(dim, pref):
    if dim <= pref:
        return dim
    t = (pref // LANES) * LANES
    while t > LANES and dim % t:
        t -= LANES
    assert dim % t == 0, (dim, pref)
    return t


def _row_tile(rows, pref):
    if rows <= pref:
        return rows
    t = max(8, pref // 8 * 8)
    while t > 8 and rows % t:
        t -= 8
    assert rows % t == 0, (rows, pref)
    return t


def _params(*sem):
    return pltpu.CompilerParams(dimension_semantics=sem, vmem_limit_bytes=VMEM_LIMIT_BYTES)


def _dot(a, b, dims):
    return lax.dot_general(a, b, (dims, ((), ())), preferred_element_type=F32)


def _mm(name, a, b, *, grid, a_spec, b_spec, o_spec, acc_shape, outs, dims, extras=(), epi=None, gathers=(), scatters=()):
    n_ex, n_out, nk = len(extras), len(outs), grid[2]
    n_g, n_s = len(gathers), len(scatters)
    carries = bool(n_g or n_s)

    def body(*refs):
        a_ref, b_ref = refs[0], refs[1]
        ex = refs[2:2 + n_ex]
        r0 = 2 + n_ex
        g_in, s_in = refs[r0:r0 + n_g], refs[r0 + n_g:r0 + n_g + n_s]
        r0 += n_g + n_s
        o_refs = refs[r0:r0 + n_out]
        g_out, s_out = refs[r0 + n_out:r0 + n_out + n_g], refs[r0 + n_out + n_g:r0 + n_out + n_g + n_s]
        acc = refs[r0 + n_out + n_g + n_s]
        i, j, k = pl.program_id(0), pl.program_id(1), pl.program_id(2)
        if carries:
            start, finish = _transfers(g_in, g_out, s_in, s_out, (), (), (), *refs[-3:])

            @pl.when((i == 0) & (j == 0) & (k == 0))
            def _():
                start()

        @pl.when(k == 0)
        def _():
            acc[...] = jnp.zeros_like(acc)

        acc[...] += _dot(a_ref[...].astype(BF16), b_ref[...].astype(BF16), dims)

        @pl.when(k == nk - 1)
        def _():
            res = epi(acc[...], *[e[...] for e in ex]) if epi is not None else (acc[...],)
            for o, r in zip(o_refs, res):
                o[...] = r.astype(o.dtype)

        if carries:
            @pl.when((i == grid[0] - 1) & (j == grid[1] - 1) & (k == nk - 1))
            def _():
                finish()

    hbm = pl.BlockSpec(memory_space=pl.ANY)
    operands = [a, b] + [e for e, _ in extras] + list(gathers) + list(scatters)
    in_specs = [a_spec, b_spec] + [s for _, s in extras] + [hbm] * (n_g + n_s)
    out_shape = [jax.ShapeDtypeStruct(s, d) for s, d in outs]
    out_shape += [jax.ShapeDtypeStruct((N_DEV,) + g.shape, g.dtype) for g in gathers]
    out_shape += [jax.ShapeDtypeStruct(t.shape, t.dtype) for t in scatters]
    scratch = [pltpu.VMEM(acc_shape, F32)] + (_transfer_sems(n_g, n_s, 0) if carries else [])
    sem = ("arbitrary",) * 3 if carries else ("parallel", "parallel", "arbitrary")
    return pl.pallas_call(
        body, name=name, grid=grid, in_specs=in_specs, out_specs=[o_spec] * n_out + [hbm] * (n_g + n_s),
        out_shape=out_shape, scratch_shapes=scratch, compiler_params=_params(*sem),
    )(*operands)


def _spec2(rows, cols, fn):
    return pl.BlockSpec((rows, cols), fn)


def _rowwise(name, fn, *, n_rows, tr, rows, bcast=(), outs=(), accs=()):
    n_r, n_b, n_o, n_a = len(rows), len(bcast), len(outs), len(accs)

    def body(*refs):
        r_refs, b_refs = refs[:n_r], refs[n_r:n_r + n_b]
        o_refs, a_refs = refs[n_r + n_b:n_r + n_b + n_o], refs[n_r + n_b + n_o:]
        res = fn(*[r[...] for r in r_refs], *[b[...] for b in b_refs])
        for o, r in zip(o_refs, res[:n_o]):
            o[...] = r.astype(o.dtype)
        if n_a:
            @pl.when(pl.program_id(0) == 0)
            def _():
                for acc in a_refs:
                    acc[...] = jnp.zeros_like(acc)

            for acc, r in zip(a_refs, res[n_o:]):
                acc[...] += r

    in_specs = [pl.BlockSpec((tr, w), functools.partial(lambda i, cb: (i, cb), cb=cb)) for _, w, cb in rows]
    in_specs += [pl.BlockSpec(b.shape, lambda i: (0, 0)) for b in bcast]
    out_specs = [pl.BlockSpec((tr, w), lambda i: (i, 0)) for w, _ in outs]
    out_specs += [pl.BlockSpec(s, lambda i: (0, 0)) for s in accs]
    out_shape = [jax.ShapeDtypeStruct((n_rows, w), d) for w, d in outs]
    out_shape += [jax.ShapeDtypeStruct(s, F32) for s in accs]
    return pl.pallas_call(
        body, name=name, grid=(n_rows // tr,), in_specs=in_specs, out_specs=out_specs, out_shape=out_shape,
        compiler_params=_params("arbitrary"),
    )(*[r for r, _, _ in rows], *bcast)


def _colsum(t):
    return jnp.sum(t, axis=0, keepdims=True)


def _rms_fwd(name, h, g, dtype, tr):
    def fn(x, gain):
        r = lax.rsqrt(jnp.mean(x * x, axis=-1, keepdims=True) + EPS)
        return (x * r * gain,)

    d = h.shape[1]
    return _rowwise(name, fn, n_rows=h.shape[0], tr=tr, rows=[(h, d, 0)], bcast=[g], outs=[(d, dtype)])[0]


def _rms_bwd(name, h, g, dhn, dres, tr):
    def fn(x, dy, dr, gain):
        r = lax.rsqrt(jnp.mean(x * x, axis=-1, keepdims=True) + EPS)
        xhat = x * r
        gd = gain * dy
        dh = r * (gd - xhat * jnp.mean(gd * xhat, axis=-1, keepdims=True)) + dr
        return dh, dh, _colsum(dy * xhat)

    d = h.shape[1]
    return _rowwise(name, fn, n_rows=h.shape[0], tr=tr, rows=[(h, d, 0), (dhn, d, 0), (dres, d, 0)], bcast=[g],
                    outs=[(d, F32), (d, BF16)], accs=[(1, d)])


def _loss_head(name, h, target, g, tr):
    d = h.shape[1]

    def fn(x, tgt, gain):
        r = lax.rsqrt(jnp.mean(x * x, axis=-1, keepdims=True) + EPS)
        xhat = x * r
        err = xhat * gain - tgt
        loss = 0.5 * jnp.sum(jnp.mean(err * err, axis=-1, keepdims=True), axis=0, keepdims=True)
        dy = err / d
        gd = gain * dy
        dh = r * (gd - xhat * jnp.mean(gd * xhat, axis=-1, keepdims=True))
        return dh, dh, jnp.broadcast_to(loss, (1, LANES)), _colsum(dy * xhat)

    return _rowwise(name, fn, n_rows=h.shape[0], tr=tr, rows=[(h, d, 0), (target, d, 0)], bcast=[g],
                    outs=[(d, F32), (d, BF16)], accs=[(1, LANES), (1, d)])


def _shift_rows(h, k, seq):
    if k == 0:
        return h
    rolled = pltpu.roll(h, (-k) % seq, 0)
    t = lax.broadcasted_iota(jnp.int32, h.shape, 0)
    return jnp.where((t + k >= 0) & (t + k < seq), rolled, 0.0)


def _window_halves(h, half, seq):
    right = h
    left = _shift_rows(h, -1, seq)
    step = 1
    while step < half:
        right = right + _shift_rows(right, step, seq)
        left = left + _shift_rows(left, -step, seq)
        step *= 2
    return left + right


def _window_count(shape, half, seq):
    t = lax.broadcasted_iota(jnp.int32, shape, 0)
    return (jnp.minimum(t + half, seq) - jnp.maximum(t - half, 0)).astype(F32)


def _pool_slabs(name, src, out_dtype, *, n_seq, seq, backward):
    d = src.shape[1]
    group = d // N_POOL_GROUPS
    tc = _tile(group, 256)

    def body(s_ref, o_ref):
        g = (pl.program_id(1) * tc) // group
        for gi, w in enumerate(POOL_WINDOWS):
            half = w // 2

            @pl.when(g == gi)
            def _():
                v = s_ref[...].astype(F32)
                cnt = _window_count(v.shape, half, seq)
                if backward:
                    u = v / cnt
                    res = (_window_halves(u, half, seq) - _shift_rows(u, -half, seq) + _shift_rows(u, half, seq)) - v
                else:
                    res = _window_halves(v, half, seq) / cnt - v
                o_ref[...] = res.astype(o_ref.dtype)

    spec = pl.BlockSpec((seq, tc), lambda b, j: (b, j))
    return pl.pallas_call(
        body, name=name, grid=(n_seq, d // tc), in_specs=[spec], out_specs=spec,
        out_shape=jax.ShapeDtypeStruct(src.shape, out_dtype), compiler_params=_params("parallel", "parallel"),
    )(src)


def _split3(x):
    x1 = x.astype(BF16)
    r1 = x - x1.astype(F32)
    x2 = r1.astype(BF16)
    x3 = (r1 - x2.astype(F32)).astype(BF16)
    return x1, x2, x3


def _tri_dot(tri, x):
    x1, x2, x3 = _split3(x)
    return _dot(tri, x1, NN) + _dot(tri, x2, NN) + _dot(tri, x3, NN)


def _chunk_terms(q_ref, k_ref, g_ref, rev, scale):
    c = q_ref.shape[0]
    ri = lax.broadcasted_iota(jnp.int32, (c, c), 0)
    ci = lax.broadcasted_iota(jnp.int32, (c, c), 1)
    causal = (ci >= ri) if rev else (ci <= ri)
    mid = c // 2 if rev else c // 2 - 1
    last = 0 if rev else c - 1
    b = _tri_dot(causal.astype(BF16), g_ref[...])
    b_mid, b_last = b[mid:mid + 1, :], b[last:last + 1, :]
    e_qm, e_km, e_qi, e_ks = jnp.exp(b - b_mid), jnp.exp(b_mid - b), jnp.exp(b), jnp.exp(b_last - b)
    q = q_ref[...].astype(F32) * scale
    k = k_ref[...].astype(F32)
    return dict(causal=causal, mid=mid, last=last, decay=jnp.exp(b_last), e_qm=e_qm, e_km=e_km, e_qi=e_qi, e_ks=e_ks,
                qm=q * e_qm, km=k * e_km, qi=q * e_qi, ks=k * e_ks)


def _gla_specs(n_chunks, heads, dk, dv, key_dim, rev, backward):
    def chunk(n):
        scan_pos = (n_chunks - 1 - n) if backward else n
        return (n_chunks - 1 - scan_pos) if rev else scan_pos

    def at(width, col0):
        return pl.BlockSpec((CHUNK, width), lambda b, h, n: (b * n_chunks + chunk(n), col0 // width + h))

    state = pl.BlockSpec((None, dv, dk), lambda b, h, n: ((b * heads + h) * n_chunks + chunk(n), 0, 0))
    return at(dk, 0), at(dk, key_dim), at(dv, 2 * key_dim), at(dk, 0), at(dv, 0), state


def _gla_fwd(name, pm, log_a, *, rev, n_seq, seq, key_dim, val_dim):
    heads, n_chunks = GLA_HEADS, seq // CHUNK
    dk, dv = key_dim // heads, val_dim // heads
    scale = dk ** -0.5
    q_spec, k_spec, v_spec, g_spec, o_spec, st_spec = _gla_specs(n_chunks, heads, dk, dv, key_dim, rev, False)

    def body(q_ref, k_ref, v_ref, g_ref, o_ref, st_ref, state):
        @pl.when(pl.program_id(2) == 0)
        def _():
            state[...] = jnp.zeros_like(state)

        t = _chunk_terms(q_ref, k_ref, g_ref, rev, scale)
        v = v_ref[...]
        p = jnp.where(t["causal"], _dot(t["qm"].astype(BF16), t["km"].astype(BF16), NT), 0.0)
        entering = state[...].astype(BF16)
        st_ref[...] = entering
        o_ref[...] = _dot(p.astype(BF16), v, NN) + _dot(t["qi"].astype(BF16), entering, NT)
        state[...] = t["decay"] * state[...] + _dot(v, t["ks"].astype(BF16), TN)

    tokens = n_seq * seq
    return pl.pallas_call(
        body, name=name, grid=(n_seq, heads, n_chunks), in_specs=[q_spec, k_spec, v_spec, g_spec],
        out_specs=[o_spec, st_spec],
        out_shape=[jax.ShapeDtypeStruct((tokens, val_dim), F32),
                   jax.ShapeDtypeStruct((n_seq * heads * n_chunks, dv, dk), BF16)],
        scratch_shapes=[pltpu.VMEM((dv, dk), F32)], compiler_params=_params("parallel", "parallel", "arbitrary"),
    )(pm, pm, pm, log_a)


def _gla_bwd(name, pm, log_a, states, d_o, *, rev, n_seq, seq, key_dim, val_dim):
    heads, n_chunks = GLA_HEADS, seq // CHUNK
    dk, dv = key_dim // heads, val_dim // heads
    scale = dk ** -0.5
    q_spec, k_spec, v_spec, g_spec, o_spec, st_spec = _gla_specs(n_chunks, heads, dk, dv, key_dim, rev, True)

    def body(q_ref, k_ref, v_ref, g_ref, do_ref, st_ref, dq_ref, dk_ref, dv_ref, dg_ref, dstate):
        @pl.when(pl.program_id(2) == 0)
        def _():
            dstate[...] = jnp.zeros_like(dstate)

        t = _chunk_terms(q_ref, k_ref, g_ref, rev, scale)
        causal = t["causal"]
        v, d_out, entering = v_ref[...], do_ref[...], st_ref[...]
        qm, km, qi, ks = (t[n].astype(BF16) for n in ("qm", "km", "qi", "ks"))
        dst = dstate[...]
        dst16 = dst.astype(BF16)
        p = jnp.where(causal, _dot(qm, km, NT), 0.0).astype(BF16)
        dp = jnp.where(causal, _dot(d_out, v, NT), 0.0).astype(BF16)
        dv_ref[...] = _dot(p, d_out, TN) + _dot(ks, dst16, NT)
        dqm, dkm = _dot(dp, km, NN), _dot(dp, qm, TN)
        dqi, dks = _dot(d_out, entering, NN), _dot(v, dst16, NN)
        d_decay = _colsum(dst * entering.astype(F32))
        dq_ref[...] = (dqi * t["e_qi"] + dqm * t["e_qm"]) * scale
        dk_ref[...] = dkm * t["e_km"] + dks * t["e_ks"]
        qm_term, km_term = dqm * t["qm"], dkm * t["km"]
        ks_term = dks * t["ks"]
        db = dqi * t["qi"] + qm_term - km_term - ks_term
        row = lax.broadcasted_iota(jnp.int32, db.shape, 0)
        db = db + jnp.where(row == t["mid"], _colsum(km_term - qm_term), 0.0)
        db = db + jnp.where(row == t["last"], _colsum(ks_term) + d_decay * t["decay"], 0.0)
        ri = lax.broadcasted_iota(jnp.int32, causal.shape, 0)
        ci = lax.broadcasted_iota(jnp.int32, causal.shape, 1)
        anti = (ci <= ri) if rev else (ci >= ri)
        dg_ref[...] = _tri_dot(anti.astype(BF16), db)
        dstate[...] = t["decay"] * dst + _dot(d_out, qi, TN)

    tokens = n_seq * seq
    kd = jax.ShapeDtypeStruct((tokens, key_dim), F32)
    return pl.pallas_call(
        body, name=name, grid=(n_seq, heads, n_chunks),
        in_specs=[q_spec, k_spec, v_spec, g_spec, o_spec, st_spec], out_specs=[g_spec, g_spec, o_spec, g_spec],
        out_shape=[kd, kd, jax.ShapeDtypeStruct((tokens, val_dim), F32), kd],
        scratch_shapes=[pltpu.VMEM((dv, dk), F32)], compiler_params=_params("parallel", "parallel", "arbitrary"),
    )(pm, pm, pm, log_a, d_o, states)


def _sigmoid_neg(z):
    e = jnp.exp(-jnp.abs(z))
    return jnp.where(z >= 0, e / (1.0 + e), 1.0 / (1.0 + e))


def _gate_fwd(name, pr, wf, bf, wb, bb, tr):
    kd = wf.shape[1]

    def fn(r, w_f, b_f, w_b, b_b):
        r16 = r.astype(BF16)
        out = []
        for w, b in ((w_f, b_f), (w_b, b_b)):
            z = _dot(r16, w.astype(BF16), NN) + b
            out.append((jnp.minimum(z, 0.0) - jnp.log1p(jnp.exp(-jnp.abs(z)))) / GATE_TAU)
        return tuple(out)

    return _rowwise(name, fn, n_rows=pr.shape[0], tr=tr, rows=[(pr, LANES, 0)], bcast=[wf, bf, wb, bb],
                    outs=[(kd, F32), (kd, F32)])


def _gate_bwd(name, pr, wf, bf, wb, bb, dgf, dgb, tr):
    kd = wf.shape[1]

    def fn(r, d_f, d_b, w_f, b_f, w_b, b_b):
        r16 = r.astype(BF16)
        dpr = jnp.zeros(r.shape, F32)
        sums = []
        for w, b, dg in ((w_f, b_f, d_f), (w_b, b_b, d_b)):
            w16 = w.astype(BF16)
            z = _dot(r16, w16, NN) + b
            dz = dg / GATE_TAU * _sigmoid_neg(z)
            dz16 = dz.astype(BF16)
            dpr = dpr + _dot(dz16, w16, NT)
            sums += [_dot(r16, dz16, TN), _colsum(dz)]
        return (dpr, *sums)

    return _rowwise(name, fn, n_rows=pr.shape[0], tr=tr, rows=[(pr, LANES, 0), (dgf, kd, 0), (dgb, kd, 0)],
                    bcast=[wf, bf, wb, bb], outs=[(LANES, BF16)], accs=[(LANES, kd), (1, kd), (LANES, kd), (1, kd)])


def _head_norm_fwd(name, of, ob, pm, gn, *, val_dim, gate_block, tr):
    dv = val_dim // GLA_HEADS

    def fn(o_f, o_b, gate, gain):
        out = []
        for h in range(GLA_HEADS):
            sl = slice(h * dv, (h + 1) * dv)
            o = o_f[:, sl] + o_b[:, sl]
            gt = gate[:, sl].astype(F32)
            on = o * lax.rsqrt(jnp.mean(o * o, axis=-1, keepdims=True) + EPS) * gain
            out.append(on * (gt * jax.nn.sigmoid(gt)))
        return (jnp.concatenate(out, axis=1),)

    return _rowwise(name, fn, n_rows=of.shape[0], tr=tr, rows=[(of, val_dim, 0), (ob, val_dim, 0), (pm, val_dim, gate_block)],
                    bcast=[gn], outs=[(val_dim, BF16)])[0]


def _head_norm_bwd(name, of, ob, pm, gn, dz, *, val_dim, gate_block, tr):
    dv = val_dim // GLA_HEADS

    def fn(o_f, o_b, gate, d_z, gain):
        d_o, d_gate = [], []
        d_gain = jnp.zeros((1, dv), F32)
        for h in range(GLA_HEADS):
            sl = slice(h * dv, (h + 1) * dv)
            o = o_f[:, sl] + o_b[:, sl]
            gt = gate[:, sl].astype(F32)
            dzh = d_z[:, sl]
            r = lax.rsqrt(jnp.mean(o * o, axis=-1, keepdims=True) + EPS)
            ohat = o * r
            sg = jax.nn.sigmoid(gt)
            d_gate.append(dzh * (ohat * gain) * (sg * (1.0 + gt * (1.0 - sg))))
            don = dzh * (gt * sg)
            d_gain = d_gain + _colsum(don * ohat)
            gd = gain * don
            d_o.append(r * (gd - ohat * jnp.mean(gd * ohat, axis=-1, keepdims=True)))
        return jnp.concatenate(d_o, axis=1), jnp.concatenate(d_gate, axis=1), d_gain

    return _rowwise(name, fn, n_rows=of.shape[0], tr=tr,
                    rows=[(of, val_dim, 0), (ob, val_dim, 0), (pm, val_dim, gate_block), (dz, val_dim, 0)], bcast=[gn],
                    outs=[(val_dim, BF16), (val_dim, BF16)], accs=[(1, dv)])


def _proj_grad(name, dqf, dqb, dkf, dkb, dvf, dvb, dgate, *, key_dim, val_dim, tr):
    def fn(q1, q2, k1, k2, v1, v2, gt):
        return (jnp.concatenate([(q1 + q2).astype(BF16), (k1 + k2).astype(BF16), (v1 + v2).astype(BF16), gt], axis=1),)

    rows = [(dqf, key_dim, 0), (dqb, key_dim, 0), (dkf, key_dim, 0), (dkb, key_dim, 0), (dvf, val_dim, 0),
            (dvb, val_dim, 0), (dgate, val_dim, 0)]
    return _rowwise(name, fn, n_rows=dqf.shape[0], tr=tr, rows=rows, outs=[(2 * key_dim + 2 * val_dim, BF16)])[0]


def _pool_out_grad(name, dh, y_pre, scale, tr):
    def fn(d, y, s):
        return d * s, _colsum(d * y)

    dm = dh.shape[1]
    return _rowwise(name, fn, n_rows=dh.shape[0], tr=tr, rows=[(dh, dm, 0), (y_pre, dm, 0)], bcast=[scale],
                    outs=[(dm, BF16)], accs=[(1, dm)])


def _mesh_place():
    x, y, c = lax.axis_index("x"), lax.axis_index("y"), lax.axis_index("c")
    return x, y, c, 4 * x + 2 * y + c


def _peer(x, y, c, p):
    px, py, pc = (x if not p & 4 else 1 - x), (y if not p & 2 else 1 - y), (c if not p & 1 else 1 - c)
    return (px, py, pc), 4 * px + 2 * py + pc


CHIP_BITS = (2, 4, 6)
N_CHIP = N_DEV // 2
GATHER_COPIES = 1 + 2 * len(CHIP_BITS)


def _transfer_sems(n_gather, n_scatter, n_pair):
    n_remote = n_gather * GATHER_COPIES + n_scatter * len(CHIP_BITS) + n_pair * N_CHIP
    n_local = n_gather + n_scatter + n_pair * N_CHIP
    return [pltpu.SemaphoreType.DMA((n_remote,)), pltpu.SemaphoreType.DMA((n_remote,)), pltpu.SemaphoreType.DMA((n_local,))]


def _transfers(g_in, g_out, s_in, s_out, p_in, p_out, p_own, send, recv, loc):
    x, y, c, me = _mesh_place()
    chip = 2 * x + y
    sibling, _ = _peer(x, y, c, 1)

    def remote(src, dst, k, peer):
        return pltpu.make_async_remote_copy(src_ref=src, dst_ref=dst, send_sem=send.at[k], recv_sem=recv.at[k],
                                            device_id=peer, device_id_type=MESH_ID)

    local, first, arrivals, forwards, forward_arrivals = [], [], [], [], []
    k = 0
    for src, out in zip(g_in, g_out):
        local.append(pltpu.make_async_copy(src, out.at[me], loc.at[len(local)]))
        for p in (1,) + CHIP_BITS:
            peer, block = _peer(x, y, c, p)
            first.append(remote(src, out.at[me], k, peer))
            arrivals.append(remote(src, out.at[block], k, peer))
            k += 1
        for p in CHIP_BITS:
            _, block = _peer(x, y, c, p)
            _, sibling_block = _peer(x, y, c, p ^ 1)
            forwards.append(remote(out.at[block], out.at[block], k, sibling))
            forward_arrivals.append(remote(out.at[sibling_block], out.at[sibling_block], k, sibling))
            k += 1
    for src, out in zip(s_in, s_out):
        local.append(pltpu.make_async_copy(src.at[chip], out.at[chip], loc.at[len(local)]))
        for p in CHIP_BITS:
            peer, _ = _peer(x, y, c, p)
            peer_chip = 2 * peer[0] + peer[1]
            first.append(remote(src.at[peer_chip], out.at[chip], k, peer))
            arrivals.append(remote(src.at[peer_chip], out.at[peer_chip], k, peer))
            k += 1
    for src, out, own in zip(p_in, p_out, p_own):
        for q in range(N_CHIP):
            local.append(pltpu.make_async_copy(src.at[2 * q + c], own.at[q], loc.at[len(local)]))
            first.append(remote(src.at[2 * q + 1 - c], out.at[q], k, sibling))
            arrivals.append(remote(src.at[2 * q + 1 - c], out.at[q], k, sibling))
            k += 1

    def start():
        for cp in local + first:
            cp.start()

    def finish():
        for cp in arrivals:
            cp.wait_recv()
        for cp in forwards:
            cp.start()
        for cp in forward_arrivals:
            cp.wait_recv()
        for cp in first + forwards:
            cp.wait_send()
        for cp in local:
            cp.wait()

    return start, finish


def _exchange(name, gathers=(), scatters=(), pairs=()):
    n_g, n_s, n_p = len(gathers), len(scatters), len(pairs)
    n_in = n_g + n_s + n_p

    def body(*refs):
        ins, outs = refs[:n_in], refs[n_in:-3]
        p_outs = outs[n_g + n_s:]
        start, finish = _transfers(ins[:n_g], outs[:n_g], ins[n_g:n_g + n_s], outs[n_g:n_g + n_s], ins[n_g + n_s:],
                                   p_outs[0::2], p_outs[1::2], *refs[-3:])
        start()
        finish()

    hbm = pl.BlockSpec(memory_space=pl.ANY)
    out_shape = [jax.ShapeDtypeStruct((N_DEV,) + g.shape, g.dtype) for g in gathers]
    out_shape += [jax.ShapeDtypeStruct(t.shape, t.dtype) for t in scatters]
    for t in pairs:
        out_shape += [jax.ShapeDtypeStruct((N_CHIP,) + t.shape[1:], t.dtype)] * 2
    return pl.pallas_call(
        body, name=name, in_specs=[hbm] * n_in, out_specs=[hbm] * len(out_shape), out_shape=out_shape,
        scratch_shapes=_transfer_sems(n_g, n_s, n_p),
    )(*gathers, *scatters, *pairs)


def _chip_sums(name, grad):
    received, own = _exchange(name + "_pair", pairs=[grad])
    cols = grad.shape[-1]
    rows = own.size // cols
    tr = _row_tile(rows, (512 * 1024) // cols)
    out = _rowwise(name + "_add", lambda a, b: (a.astype(F32) + b.astype(F32),), n_rows=rows, tr=tr,
                   rows=[(own.reshape(rows, cols), cols, 0), (received.reshape(rows, cols), cols, 0)], outs=[(cols, BF16)])[0]
    return out.reshape(own.shape)


def _all_reduce_small(name, pack):
    def body(in_ref, out_ref, parts, send_sems, recv_sems):
        x, y, c, me = _mesh_place()
        parts[me] = in_ref[...]

        def copy(p, arriving):
            peer, peer_block = _peer(x, y, c, p)
            return pltpu.make_async_remote_copy(src_ref=in_ref, dst_ref=parts.at[peer_block if arriving else me],
                                                send_sem=send_sems.at[p - 1], recv_sem=recv_sems.at[p - 1],
                                                device_id=peer, device_id_type=MESH_ID)

        sends = [copy(p, False) for p in range(1, N_DEV)]
        for cp in sends:
            cp.start()
        for p in range(1, N_DEV):
            copy(p, True).wait_recv()
        for cp in sends:
            cp.wait_send()
        total = parts[0]
        for j in range(1, N_DEV):
            total = total + parts[j]
        out_ref[...] = total

    vmem = pl.BlockSpec(memory_space=pltpu.VMEM)
    return pl.pallas_call(
        body, name=name, in_specs=[vmem], out_specs=vmem, out_shape=jax.ShapeDtypeStruct(pack.shape, F32),
        scratch_shapes=[pltpu.VMEM((N_DEV,) + pack.shape, F32), pltpu.SemaphoreType.DMA((N_PEER,)),
                        pltpu.SemaphoreType.DMA((N_PEER,))],
        compiler_params=pltpu.CompilerParams(vmem_limit_bytes=VMEM_LIMIT_BYTES),
    )(pack)


def _adamw_math(w, g, m, v):
    m = ADAM_B1 * m + (1.0 - ADAM_B1) * g
    v = ADAM_B2 * v + (1.0 - ADAM_B2) * jnp.square(g)
    m_hat = m / (1.0 - ADAM_B1 ** ADAM_STEP)
    v_hat = v / (1.0 - ADAM_B2 ** ADAM_STEP)
    delta = -ADAM_LR * (m_hat / (jnp.sqrt(v_hat) + ADAM_EPS) + ADAM_WD * w)
    return delta, m, v


def _adamw_shard(name, w, m, v, parts, layer=0, into=None):
    shape = w.shape
    n_layers, cols = shape[0], shape[-1]
    rows = w.size // cols // n_layers
    tr = _row_tile(rows, (256 * 1024) // cols)
    n_into = 0 if into is None else 4

    def body(*refs):
        w_ref, m_ref, v_ref, p_ref = refs[:4]
        g_ref, d_ref, nm_ref, nv_ref = refs[4 + n_into:]
        g = p_ref[0].astype(F32)
        for j in range(1, N_CHIP):
            g = g + p_ref[j].astype(F32)
        delta, new_m, new_v = _adamw_math(w_ref[...], g, m_ref[...], v_ref[...])
        g_ref[...], d_ref[...], nm_ref[...], nv_ref[...] = g, delta, new_m, new_v

    view = (n_layers, rows, cols)
    spec = pl.BlockSpec((None, tr, cols), lambda i: (layer, i, 0))
    operands = [w.reshape(view), m.reshape(view), v.reshape(view), parts.reshape(N_CHIP, rows, cols)]
    in_specs = [spec, spec, spec, pl.BlockSpec((N_CHIP, tr, cols), lambda i: (0, i, 0))]
    if into is not None:
        operands += [t.reshape(view) for t in into]
        in_specs += [pl.BlockSpec(memory_space=pl.ANY)] * 4
    res = pl.pallas_call(
        body, name=name, grid=(rows // tr,), in_specs=in_specs, out_specs=[spec] * 4,
        out_shape=[jax.ShapeDtypeStruct(view, F32)] * 4, input_output_aliases={4 + t: t for t in range(n_into)},
        compiler_params=_params("parallel"),
    )(*operands)
    return [r.reshape(shape) for r in res]


def _adamw_small(name, w, m, v, g):
    shape = w.shape
    two_d = (1, shape[0]) if len(shape) == 1 else (w.size // shape[-1], shape[-1])

    def body(w_ref, m_ref, v_ref, g_ref, d_ref, nm_ref, nv_ref):
        d_ref[...], nm_ref[...], nv_ref[...] = _adamw_math(w_ref[...], g_ref[...], m_ref[...], v_ref[...])

    res = pl.pallas_call(body, name=name, out_shape=[jax.ShapeDtypeStruct(two_d, F32)] * 3)(
        w.reshape(two_d), m.reshape(two_d), v.reshape(two_d), g.reshape(two_d))
    return [r.reshape(shape) for r in res]


def _mlp_fwd(tag, h_in, gain, w1, w2_mine, down_gathers, tm, tr):
    tokens, d = h_in.shape
    fs = w1.shape[2]
    f = N_DEV * fs
    hn = _rms_fwd(f"rms_mlp{tag}", h_in, gain, BF16, tr)
    tn, tk = _tile(fs, 1024), _tile(d, 512)
    per = fs // tn

    def relu2(acc):
        a = jnp.maximum(acc, 0.0)
        return acc, a * a

    a_pre, act, w2 = _mm(
        f"mlp_up{tag}", hn, w1, grid=(tokens // tm, f // tn, d // tk), a_spec=_spec2(tm, tk, lambda m, n, k: (m, k)),
        b_spec=pl.BlockSpec((None, tk, tn), lambda m, n, k: (n // per, k, n % per)),
        o_spec=_spec2(tm, tn, lambda m, n, k: (m, n)), acc_shape=(tm, tn), outs=[((tokens, f), BF16)] * 2, dims=NN, epi=relu2,
        gathers=[w2_mine])
    tn2, tk2 = _tile(d, 1024), _tile(fs, 512)
    kper = fs // tk2
    h_out, *got = _mm(
        f"mlp_down{tag}", act, w2, grid=(tokens // tm, d // tn2, f // tk2), a_spec=_spec2(tm, tk2, lambda m, n, k: (m, k)),
        b_spec=pl.BlockSpec((None, tk2, tn2), lambda m, n, k: (k // kper, k % kper, n)),
        o_spec=_spec2(tm, tn2, lambda m, n, k: (m, n)), acc_shape=(tm, tn2), outs=[((tokens, d), F32)], dims=NN,
        extras=[(h_in, _spec2(tm, tn2, lambda m, n, k: (m, n)))], epi=lambda acc, res: (res + acc,), gathers=down_gathers)
    return hn, a_pre, act, h_out, w2, got


def _mlp_bwd(tag, h_in, gain, hn, a_pre, act, dh, dh16, w1, w2, tm, tr):
    tokens, d = h_in.shape
    fs = w1.shape[2]
    f = N_DEV * fs
    tn, tk = _tile(fs, 1024), _tile(d, 512)
    nper = fs // tn
    da = _mm(
        f"mlp_dact{tag}", dh16, w2, grid=(tokens // tm, f // tn, d // tk), a_spec=_spec2(tm, tk, lambda m, n, k: (m, k)),
        b_spec=pl.BlockSpec((None, tn, tk), lambda m, n, k: (n // nper, n % nper, k)),
        o_spec=_spec2(tm, tn, lambda m, n, k: (m, n)), acc_shape=(tm, tn), outs=[((tokens, f), BF16)], dims=NT,
        extras=[(a_pre, _spec2(tm, tn, lambda m, n, k: (m, n)))],
        epi=lambda acc, pre: (acc * (2.0 * jnp.maximum(pre.astype(F32), 0.0)),))[0]
    tkt = _tile(tokens, 512)
    tm2, tn2 = _tile(fs, 1024), _tile(d, 1024)
    mper = fs // tm2
    dw2 = _mm(
        f"mlp_dw_out{tag}", act, dh16, grid=(f // tm2, d // tn2, tokens // tkt), a_spec=_spec2(tkt, tm2, lambda m, n, k: (k, m)),
        b_spec=_spec2(tkt, tn2, lambda m, n, k: (k, n)),
        o_spec=pl.BlockSpec((None, tm2, tn2), lambda m, n, k: (m // mper, m % mper, n)), acc_shape=(tm2, tn2),
        outs=[((N_DEV, fs, d), BF16)], dims=TN)[0]
    dw2_sums = _chip_sums(f"mlp_dw_out{tag}", dw2)
    tm3 = _tile(d, 1024)
    dw1, dw2_landed = _mm(
        f"mlp_dw_in{tag}", hn, da, grid=(d // tm3, f // tn, tokens // tkt), a_spec=_spec2(tkt, tm3, lambda m, n, k: (k, m)),
        b_spec=_spec2(tkt, tn, lambda m, n, k: (k, n)),
        o_spec=pl.BlockSpec((None, tm3, tn), lambda m, n, k: (n // nper, m, n % nper)), acc_shape=(tm3, tn),
        outs=[((N_DEV, d, fs), BF16)], dims=TN, scatters=[dw2_sums])
    dw1_sums = _chip_sums(f"mlp_dw_in{tag}", dw1)
    tn4, tk4 = _tile(d, 1024), _tile(fs, 512)
    kper = fs // tk4
    dhn, dw1_landed = _mm(
        f"mlp_dx{tag}", da, w1, grid=(tokens // tm, d // tn4, f // tk4), a_spec=_spec2(tm, tk4, lambda m, n, k: (m, k)),
        b_spec=pl.BlockSpec((None, tn4, tk4), lambda m, n, k: (k // kper, n, k % kper)),
        o_spec=_spec2(tm, tn4, lambda m, n, k: (m, n)), acc_shape=(tm, tn4), outs=[((tokens, d), F32)], dims=NT,
        scatters=[dw1_sums])
    dh_in, dh_in16, dgain = _rms_bwd(f"rms_mlp_bwd{tag}", h_in, gain, dhn, dh, tr)
    return dh_in, dh_in16, dgain, dw1_landed, dw2_landed


def _plain_mm(name, a, b, dims, out_dtype, *, tm, tn_pref=1024, tk_pref=512, extras=None, epi=None, gathers=(), scatters=()):
    if dims == NN:
        (m, k), n = a.shape, b.shape[1]
    elif dims == NT:
        (m, k), n = a.shape, b.shape[0]
    else:
        (k, m), n = a.shape, b.shape[1]
    tm, tn, tk = _tile(m, tm), _tile(n, tn_pref), _tile(k, tk_pref)
    a_spec = _spec2(tk, tm, lambda i, j, l: (l, i)) if dims == TN else _spec2(tm, tk, lambda i, j, l: (i, l))
    b_spec = _spec2(tn, tk, lambda i, j, l: (j, l)) if dims == NT else _spec2(tk, tn, lambda i, j, l: (l, j))
    o_spec = _spec2(tm, tn, lambda i, j, l: (i, j))
    ex = [(e, o_spec) for e in (extras or [])]
    res = _mm(name, a, b, grid=(m // tm, n // tn, k // tk), a_spec=a_spec, b_spec=b_spec, o_spec=o_spec, acc_shape=(tm, tn),
              outs=[((m, n), out_dtype)], dims=dims, extras=ex, epi=epi, gathers=gathers, scatters=scatters)
    return res if (gathers or scatters) else res[0]


def kernel(x, norm_mix, norm_mlp, norm_final, pool_w, pool_scale, gla_w_in, gla_w_up_f, gla_b_up_f, gla_w_up_b, gla_b_up_b, gla_g_norm, gla_w_out, mlp_w_in, mlp_w_out, loss_target, m_norm_mix, m_norm_mlp, m_norm_final, m_pool_w, m_pool_scale, m_gla_w_in, m_gla_w_up_f, m_gla_b_up_f, m_gla_w_up_b, m_gla_b_up_b, m_gla_g_norm, m_gla_w_out, m_mlp_w_in, m_mlp_w_out, v_norm_mix, v_norm_mlp, v_norm_final, v_pool_w, v_pool_scale, v_gla_w_in, v_gla_w_up_f, v_gla_b_up_f, v_gla_w_up_b, v_gla_b_up_b, v_gla_g_norm, v_gla_w_out, v_mlp_w_in, v_mlp_w_out):
    n_seq, seq, d = x.shape
    tokens = n_seq * seq
    group = d // N_POOL_GROUPS
    key_dim, val_dim = d // 2, d
    dv = val_dim // GLA_HEADS
    n_main = 2 * key_dim + 2 * val_dim
    proj_shard = gla_w_in.shape[2]
    proj_width = N_DEV * proj_shard
    kd_shard = key_dim // N_DEV
    assert proj_width == n_main + 2 * GATE_RANK and seq % CHUNK == 0 and 2 * GATE_RANK <= LANES
    assert mlp_w_in.shape[0] == 2 and pool_w.shape[0] == 1 and gla_w_in.shape[0] == 1
    me = 4 * lax.axis_index("x") + 2 * lax.axis_index("y") + lax.axis_index("c")
    tm = _tile(tokens, 1024)
    tr = _tile(tokens, 128)
    x2, target = x.reshape(tokens, d), loss_target.reshape(tokens, d)

    small_rows = 2 * GATE_RANK + 3
    small = jnp.concatenate([
        gla_w_up_f[0], gla_w_up_b[0], gla_b_up_f, gla_b_up_b,
        jnp.pad(gla_g_norm, ((0, 0), (0, kd_shard - gla_g_norm.shape[1])))], axis=0)
    w1_mine = [mlp_w_in[l].astype(BF16) for l in range(2)]
    w2_mine = [mlp_w_out[l].astype(BF16) for l in range(2)]
    pool_g, small_g, w1_0 = _exchange("gather_first", gathers=[pool_w[0].astype(BF16), small, w1_mine[0]])
    w_pool = pool_g.transpose(1, 0, 2, 3).reshape(N_POOL_GROUPS, group, group)
    small_full = small_g.transpose(1, 0, 2).reshape(small_rows, key_dim)
    pad_rows = lambda w, r0: jnp.pad(w, ((r0, LANES - r0 - GATE_RANK), (0, 0)))
    w_up_f, w_up_b = pad_rows(small_full[:GATE_RANK], 0), pad_rows(small_full[GATE_RANK:2 * GATE_RANK], GATE_RANK)
    b_up_f, b_up_b = small_full[2 * GATE_RANK:2 * GATE_RANK + 1], small_full[2 * GATE_RANK + 1:2 * GATE_RANK + 2]
    g_norm = small_g[:, 2 * GATE_RANK + 2, :gla_g_norm.shape[1]].reshape(1, dv)

    hn0 = _rms_fwd("rms_mix0", x2, norm_mix[0:1], F32, tr)
    pooled = _pool_slabs("pool_fwd", hn0, BF16, n_seq=n_seq, seq=seq, backward=False)
    tn, tk = _tile(group, 1024), _tile(group, 512)
    nper, kper = group // tn, group // tk
    y_pre, h1 = _mm(
        "pool_mix", pooled, w_pool, grid=(tokens // tm, d // tn, group // tk),
        a_spec=_spec2(tm, tk, lambda m, n, k: (m, (n // nper) * kper + k)),
        b_spec=pl.BlockSpec((None, tk, tn), lambda m, n, k: (n // nper, k, n % nper)),
        o_spec=_spec2(tm, tn, lambda m, n, k: (m, n)), acc_shape=(tm, tn), outs=[((tokens, d), F32)] * 2, dims=NN,
        extras=[(x2, _spec2(tm, tn, lambda m, n, k: (m, n))), (pool_scale, _spec2(1, tn, lambda m, n, k: (0, n)))],
        epi=lambda acc, res, s: (acc, res + acc * s))
    hn1, a_pre0, act0, h2, w2_0, (win_g, wout_g) = _mlp_fwd(
        "0", h1, norm_mlp[0:1], w1_0, w2_mine[0], [gla_w_in[0].astype(BF16), gla_w_out[0].astype(BF16)], tm, tr)
    w_in = win_g.transpose(1, 0, 2).reshape(d, proj_width)
    w_main = w_in[:, :n_main]
    w_r = jnp.pad(w_in[:, n_main:], ((0, 0), (0, LANES - 2 * GATE_RANK)))
    w_out = wout_g.reshape(val_dim, d)

    hn2 = _rms_fwd("rms_mix1", h2, norm_mix[1:2], BF16, tr)
    pm, w1_1 = _plain_mm("gla_proj", hn2, w_main, NN, BF16, tm=tm, gathers=[w1_mine[1]])
    pr = _plain_mm("gla_proj_gate", hn2, w_r, NN, F32, tm=tm)
    log_a_f, log_a_b = _gate_fwd("gla_gate", pr, w_up_f, b_up_f, w_up_b, b_up_b, tr)
    dims = dict(n_seq=n_seq, seq=seq, key_dim=key_dim, val_dim=val_dim)
    o_f, st_f = _gla_fwd("gla_scan_f", pm, log_a_f, rev=False, **dims)
    o_b, st_b = _gla_fwd("gla_scan_b", pm, log_a_b, rev=True, **dims)
    gate_block = (2 * key_dim + val_dim) // val_dim
    z = _head_norm_fwd("gla_head_norm", o_f, o_b, pm, g_norm, val_dim=val_dim, gate_block=gate_block, tr=tr)
    h3 = _plain_mm("gla_out", z, w_out, NN, F32, tm=tm, extras=[h2], epi=lambda acc, res: (res + acc,))
    hn3, a_pre1, act1, h4, w2_1, _ = _mlp_fwd("1", h3, norm_mlp[1:2], w1_1, w2_mine[1], [], tm, tr)

    dh4, dh4_16, loss_row, d_norm_final = _loss_head("loss_head", h4, target, norm_final.reshape(1, d), tr)

    dh3, dh3_16, d_norm_mlp1, dw1_1, dw2_1 = _mlp_bwd("1", h3, norm_mlp[1:2], hn3, a_pre1, act1, dh4, dh4_16, w1_1, w2_1, tm, tr)
    dz = _plain_mm("gla_out_dx", dh3_16, w_out, NT, F32, tm=tm)
    dw_out = _plain_mm("gla_out_dw", z, dh3_16, TN, BF16, tm=1024)
    dw_out_sums = _chip_sums("gla_out_dw", dw_out.reshape(N_DEV, val_dim // N_DEV, d))
    d_o, d_gate, d_g_norm = _head_norm_bwd("gla_head_norm_bwd", o_f, o_b, pm, g_norm, dz, val_dim=val_dim,
                                           gate_block=gate_block, tr=tr)
    dq_f, dk_f, dv_f, dla_f = _gla_bwd("gla_scan_bwd_f", pm, log_a_f, st_f, d_o, rev=False, **dims)
    dq_b, dk_b, dv_b, dla_b = _gla_bwd("gla_scan_bwd_b", pm, log_a_b, st_b, d_o, rev=True, **dims)
    dpr, dw_up_f, db_up_f, dw_up_b, db_up_b = _gate_bwd("gla_gate_bwd", pr, w_up_f, b_up_f, w_up_b, b_up_b, dla_f, dla_b, tr)
    dpm = _proj_grad("gla_proj_grad", dq_f, dq_b, dk_f, dk_b, dv_f, dv_b, d_gate, key_dim=key_dim, val_dim=val_dim, tr=tr)
    dw_main, dw_out_landed = _plain_mm("gla_proj_dw", hn2, dpm, TN, BF16, tm=1024, scatters=[dw_out_sums])
    dw_r = _plain_mm("gla_proj_gate_dw", hn2, dpr, TN, BF16, tm=1024)
    dw_in = jnp.concatenate([dw_main, dw_r[:, :2 * GATE_RANK]], axis=1)
    dw_in_sums = _chip_sums("gla_proj_dw", dw_in.reshape(d, N_DEV, proj_shard).transpose(1, 0, 2))
    dhn2, dw_in_landed = _plain_mm("gla_proj_dx", dpm, w_main, NT, F32, tm=tm, scatters=[dw_in_sums])
    dhn2 = _plain_mm("gla_proj_gate_dx", dpr, w_r, NT, F32, tm=tm, extras=[dhn2], epi=lambda acc, res: (res + acc,))
    dh2, dh2_16, d_norm_mix1 = _rms_bwd("rms_mix1_bwd", h2, norm_mix[1:2], dhn2, dh3, tr)

    dh1, _, d_norm_mlp0, dw1_0, dw2_0 = _mlp_bwd("0", h1, norm_mlp[0:1], hn1, a_pre0, act0, dh2, dh2_16, w1_0, w2_0, tm, tr)
    dyp, d_pool_scale = _pool_out_grad("pool_out_grad", dh1, y_pre, pool_scale, tr)
    dpooled = _mm(
        "pool_mix_dx", dyp, w_pool, grid=(tokens // tm, d // tn, group // tk),
        a_spec=_spec2(tm, tk, lambda m, n, k: (m, (n // nper) * kper + k)),
        b_spec=pl.BlockSpec((None, tn, tk), lambda m, n, k: (n // nper, n % nper, k)),
        o_spec=_spec2(tm, tn, lambda m, n, k: (m, n)), acc_shape=(tm, tn), outs=[((tokens, d), F32)], dims=NT)[0]
    tmw, tkt = _tile(group, 1024), _tile(tokens, 512)
    mper = group // tmw
    dw_pool = _mm(
        "pool_mix_dw", pooled, dyp, grid=(d // tmw, group // tn, tokens // tkt),
        a_spec=_spec2(tkt, tmw, lambda m, n, k: (k, m)), b_spec=_spec2(tkt, tn, lambda m, n, k: (k, (m // mper) * nper + n)),
        o_spec=pl.BlockSpec((None, tmw, tn), lambda m, n, k: (m // mper, m % mper, n)), acc_shape=(tmw, tn),
        outs=[((N_POOL_GROUPS, group, group), BF16)], dims=TN)[0]
    dw_pool_sums = _chip_sums("pool_mix_dw", dw_pool.reshape(N_POOL_GROUPS, N_DEV, group // N_DEV, group).transpose(1, 0, 2, 3))
    dw_pool_landed = _exchange("scatter_pool", scatters=[dw_pool_sums])[0]
    dhn0 = _pool_slabs("pool_bwd", dpooled, F32, n_seq=n_seq, seq=seq, backward=True)
    dx, _, d_norm_mix0 = _rms_bwd("rms_mix0_bwd", x2, norm_mix[0:1], dhn0, dh1, tr)

    pack_cols = key_dim
    rows_of = lambda a: a.reshape(-1, pack_cols)
    pieces = [rows_of(t) for t in (d_norm_mix0, d_norm_mix1, d_norm_mlp0, d_norm_mlp1, d_norm_final, d_pool_scale,
                                   dw_up_f[:GATE_RANK], dw_up_b[GATE_RANK:2 * GATE_RANK], db_up_f, db_up_b)]
    pieces += [jnp.pad(d_g_norm, ((0, 0), (0, pack_cols - dv))), jnp.pad(loss_row, ((0, 0), (0, pack_cols - LANES)))]
    n_rows = sum(p.shape[0] for p in pieces)
    pack = jnp.concatenate(pieces + [jnp.zeros((-n_rows % 8, pack_cols), F32)], axis=0)
    red = _all_reduce_small("reduce_small", pack)
    cuts, r0 = [], 0
    for p in pieces:
        cuts.append(red[r0:r0 + p.shape[0]])
        r0 += p.shape[0]
    g_norm_mix = jnp.concatenate([cuts[0].reshape(1, d), cuts[1].reshape(1, d)], axis=0)
    g_norm_mlp = jnp.concatenate([cuts[2].reshape(1, d), cuts[3].reshape(1, d)], axis=0)
    g_norm_final = cuts[4].reshape(d)
    g_pool_scale = cuts[5].reshape(1, d)
    shard_cols = lambda t, width: lax.dynamic_slice_in_dim(t, me * width, width, axis=1)
    g_w_up_f, g_w_up_b = shard_cols(cuts[6], kd_shard)[None], shard_cols(cuts[7], kd_shard)[None]
    g_b_up_f, g_b_up_b = shard_cols(cuts[8], kd_shard), shard_cols(cuts[9], kd_shard)
    g_g_norm = shard_cols(cuts[10][:, :dv], dv // N_DEV)
    loss = cuts[11][0, 0]

    big = {
        "pool_w": _adamw_shard("adamw_pool_w", pool_w, m_pool_w, v_pool_w, dw_pool_landed),
        "gla_w_in": _adamw_shard("adamw_gla_w_in", gla_w_in, m_gla_w_in, v_gla_w_in, dw_in_landed),
        "gla_w_out": _adamw_shard("adamw_gla_w_out", gla_w_out, m_gla_w_out, v_gla_w_out, dw_out_landed),
    }
    for nm, w, m, v, landed in (("mlp_w_in", mlp_w_in, m_mlp_w_in, v_mlp_w_in, (dw1_0, dw1_1)),
                                ("mlp_w_out", mlp_w_out, m_mlp_w_out, v_mlp_w_out, (dw2_0, dw2_1))):
        layer1 = _adamw_shard(f"adamw_{nm}1", w, m, v, landed[1], layer=1)
        big[nm] = _adamw_shard(f"adamw_{nm}0", w, m, v, landed[0], layer=0, into=layer1)
    grads = {"norm_mix": g_norm_mix, "norm_mlp": g_norm_mlp, "norm_final": g_norm_final, "pool_scale": g_pool_scale,
             "gla_w_up_f": g_w_up_f, "gla_b_up_f": g_b_up_f, "gla_w_up_b": g_w_up_b, "gla_b_up_b": g_b_up_b, "gla_g_norm": g_g_norm}
    state = {"norm_mix": (norm_mix, m_norm_mix, v_norm_mix), "norm_mlp": (norm_mlp, m_norm_mlp, v_norm_mlp),
             "norm_final": (norm_final, m_norm_final, v_norm_final), "pool_scale": (pool_scale, m_pool_scale, v_pool_scale),
             "gla_w_up_f": (gla_w_up_f, m_gla_w_up_f, v_gla_w_up_f), "gla_b_up_f": (gla_b_up_f, m_gla_b_up_f, v_gla_b_up_f),
             "gla_w_up_b": (gla_w_up_b, m_gla_w_up_b, v_gla_w_up_b), "gla_b_up_b": (gla_b_up_b, m_gla_b_up_b, v_gla_b_up_b),
             "gla_g_norm": (gla_g_norm, m_gla_g_norm, v_gla_g_norm)}
    order = ["norm_mix", "norm_mlp", "norm_final", "pool_w", "pool_scale", "gla_w_in", "gla_w_up_f", "gla_b_up_f", "gla_w_up_b",
             "gla_b_up_b", "gla_g_norm", "gla_w_out", "mlp_w_in", "mlp_w_out"]
    out_g, out_d, out_m, out_v = [], [], [], []
    for nm in order:
        if nm in big:
            g, dl, new_m, new_v = big[nm]
        else:
            g = grads[nm]
            dl, new_m, new_v = _adamw_small(f"adamw_{nm}", *state[nm], g)
        out_g.append(g), out_d.append(dl), out_m.append(new_m), out_v.append(new_v)
    return (loss, dx.reshape(x.shape), *out_g, *out_d, *out_m, *out_v)
```

```python
import functools

import jax
import jax.numpy as jnp
from jax import lax
from jax.experimental import pallas as pl
from jax.experimental.pallas import tpu as pltpu

F32 = jnp.float32
BF16 = jnp.bfloat16
N_DEV = 8
N_PEER = N_DEV - 1
LANES = 128
VMEM_LIMIT_BYTES = 48 * 1024 * 1024
MESH_ID = pl.DeviceIdType.MESH

POOL_WINDOWS = (2, 4, 8, 16)
N_POOL_GROUPS = len(POOL_WINDOWS)
GLA_HEADS = 4
GATE_RANK = 16
GATE_TAU = 16.0
CHUNK = 64
EPS = 1e-6
ADAM_LR = 0.001
ADAM_B1 = 0.9
ADAM_B2 = 0.999
ADAM_EPS = 1e-08
ADAM_WD = 0.01
ADAM_STEP = 10

NN = ((1,), (0,))
NT = ((1,), (1,))
TN = ((0,), (0,))


def _tile(dim, pref):
    if dim <= pref:
        return dim
    t = (pref // LANES) * LANES
    while t > LANES and dim % t:
        t -= LANES
    assert dim % t == 0, (dim, pref)
    return t


def _row_tile(rows, pref):
    if rows <= pref:
        return rows
    t = max(8, pref // 8 * 8)
    while t > 8 and rows % t:
        t -= 8
    assert rows % t == 0, (rows, pref)
    return t


def _params(*sem):
    return pltpu.CompilerParams(dimension_semantics=sem, vmem_limit_bytes=VMEM_LIMIT_BYTES)


def _dot(a, b, dims):
    return lax.dot_general(a, b, (dims, ((), ())), preferred_element_type=F32)


def _mm(name, a, b, *, grid, a_spec, b_spec, o_spec, acc_shape, outs, dims, extras=(), epi=None, gathers=(), scatters=()):
    n_ex, n_out, nk = len(extras), len(outs), grid[2]
    n_g, n_s = len(gathers), len(scatters)
    carries = bool(n_g or n_s)

    def body(*refs):
        a_ref, b_ref = refs[0], refs[1]
        ex = refs[2:2 + n_ex]
        r0 = 2 + n_ex
        g_in, s_in = refs[r0:r0 + n_g], refs[r0 + n_g:r0 + n_g + n_s]
        r0 += n_g + n_s
        o_refs = refs[r0:r0 + n_out]
        g_out, s_out = refs[r0 + n_out:r0 + n_out + n_g], refs[r0 + n_out + n_g:r0 + n_out + n_g + n_s]
        acc = refs[r0 + n_out + n_g + n_s]
        i, j, k = pl.program_id(0), pl.program_id(1), pl.program_id(2)
        if carries:
            start, finish = _transfers(g_in, g_out, s_in, s_out, (), (), *refs[-3:])

            @pl.when((i == 0) & (j == 0) & (k == 0))
            def _():
                start()

        @pl.when(k == 0)
        def _():
            acc[...] = jnp.zeros_like(acc)

        acc[...] += _dot(a_ref[...].astype(BF16), b_ref[...].astype(BF16), dims)

        @pl.when(k == nk - 1)
        def _():
            res = epi(acc[...], *[e[...] for e in ex]) if epi is not None else (acc[...],)
            for o, r in zip(o_refs, res):
                o[...] = r.astype(o.dtype)

        if carries:
            @pl.when((i == grid[0] - 1) & (j == grid[1] - 1) & (k == nk - 1))
            def _():
                finish()

    hbm = pl.BlockSpec(memory_space=pl.ANY)
    operands = [a, b] + [e for e, _ in extras] + list(gathers) + list(scatters)
    in_specs = [a_spec, b_spec] + [s for _, s in extras] + [hbm] * (n_g + n_s)
    out_shape = [jax.ShapeDtypeStruct(s, d) for s, d in outs]
    out_shape += [jax.ShapeDtypeStruct((N_DEV,) + g.shape, g.dtype) for g in gathers]
    out_shape += [jax.ShapeDtypeStruct((len(CHIP_BITS),) + t.shape[1:], t.dtype) for t in scatters]
    scratch = [pltpu.VMEM(acc_shape, F32)] + (_transfer_sems(n_g, n_s, 0) if carries else [])
    sem = ("arbitrary",) * 3 if carries else ("parallel", "parallel", "arbitrary")
    return pl.pallas_call(
        body, name=name, grid=grid, in_specs=in_specs, out_specs=[o_spec] * n_out + [hbm] * (n_g + n_s),
        out_shape=out_shape, scratch_shapes=scratch, compiler_params=_params(*sem),
    )(*operands)


def _spec2(rows, cols, fn):
    return pl.BlockSpec((rows, cols), fn)


def _rowwise(name, fn, *, n_rows, tr, rows, bcast=(), outs=(), accs=()):
    n_r, n_b, n_o, n_a = len(rows), len(bcast), len(outs), len(accs)

    def body(*refs):
        r_refs, b_refs = refs[:n_r], refs[n_r:n_r + n_b]
        o_refs, a_refs = refs[n_r + n_b:n_r + n_b + n_o], refs[n_r + n_b + n_o:]
        res = fn(*[r[...] for r in r_refs], *[b[...] for b in b_refs])
        for o, r in zip(o_refs, res[:n_o]):
            o[...] = r.astype(o.dtype)
        if n_a:
            @pl.when(pl.program_id(0) == 0)
            def _():
                for acc in a_refs:
                    acc[...] = jnp.zeros_like(acc)

            for acc, r in zip(a_refs, res[n_o:]):
                acc[...] += r

    in_specs = [pl.BlockSpec((tr, w), functools.partial(lambda i, cb: (i, cb), cb=cb)) for _, w, cb in rows]
    in_specs += [pl.BlockSpec(b.shape, lambda i: (0, 0)) for b in bcast]
    out_specs = [pl.BlockSpec((tr, w), lambda i: (i, 0)) for w, _ in outs]
    out_specs += [pl.BlockSpec(s, lambda i: (0, 0)) for s in accs]
    out_shape = [jax.ShapeDtypeStruct((n_rows, w), d) for w, d in outs]
    out_shape += [jax.ShapeDtypeStruct(s, F32) for s in accs]
    return pl.pallas_call(
        body, name=name, grid=(n_rows // tr,), in_specs=in_specs, out_specs=out_specs, out_shape=out_shape,
        compiler_params=_params("arbitrary"),
    )(*[r for r, _, _ in rows], *bcast)


def _colsum(t):
    return jnp.sum(t, axis=0, keepdims=True)


def _rms_fwd(name, h, g, dtype, tr):
    def fn(x, gain):
        r = lax.rsqrt(jnp.mean(x * x, axis=-1, keepdims=True) + EPS)
        return (x * r * gain,)

    d = h.shape[1]
    return _rowwise(name, fn, n_rows=h.shape[0], tr=tr, rows=[(h, d, 0)], bcast=[g], outs=[(d, dtype)])[0]


def _rms_bwd(name, h, g, dhn, dres, tr):
    def fn(x, dy, dr, gain):
        r = lax.rsqrt(jnp.mean(x * x, axis=-1, keepdims=True) + EPS)
        xhat = x * r
        gd = gain * dy
        dh = r * (gd - xhat * jnp.mean(gd * xhat, axis=-1, keepdims=True)) + dr
        return dh, dh, _colsum(dy * xhat)

    d = h.shape[1]
    return _rowwise(name, fn, n_rows=h.shape[0], tr=tr, rows=[(h, d, 0), (dhn, d, 0), (dres, d, 0)], bcast=[g],
                    outs=[(d, F32), (d, BF16)], accs=[(1, d)])


def _loss_head(name, h, target, g, tr):
    d = h.shape[1]

    def fn(x, tgt, gain):
        r = lax.rsqrt(jnp.mean(x * x, axis=-1, keepdims=True) + EPS)
        xhat = x * r
        err = xhat * gain - tgt
        loss = 0.5 * jnp.sum(jnp.mean(err * err, axis=-1, keepdims=True), axis=0, keepdims=True)
        dy = err / d
        gd = gain * dy
        dh = r * (gd - xhat * jnp.mean(gd * xhat, axis=-1, keepdims=True))
        return dh, dh, jnp.broadcast_to(loss, (1, LANES)), _colsum(dy * xhat)

    return _rowwise(name, fn, n_rows=h.shape[0], tr=tr, rows=[(h, d, 0), (target, d, 0)], bcast=[g],
                    outs=[(d, F32), (d, BF16)], accs=[(1, LANES), (1, d)])


def _shift_rows(h, k, seq):
    if k == 0:
        return h
    rolled = pltpu.roll(h, (-k) % seq, 0)
    t = lax.broadcasted_iota(jnp.int32, h.shape, 0)
    return jnp.where((t + k >= 0) & (t + k < seq), rolled, 0.0)


def _window_halves(h, half, seq):
    right = h
    left = _shift_rows(h, -1, seq)
    step = 1
    while step < half:
        right = right + _shift_rows(right, step, seq)
        left = left + _shift_rows(left, -step, seq)
        step *= 2
    return left + right


def _window_count(shape, half, seq):
    t = lax.broadcasted_iota(jnp.int32, shape, 0)
    return (jnp.minimum(t + half, seq) - jnp.maximum(t - half, 0)).astype(F32)


def _pool_slabs(name, src, out_dtype, *, n_seq, seq, backward):
    d = src.shape[1]
    group = d // N_POOL_GROUPS
    tc = _tile(group, 256)

    def body(s_ref, o_ref):
        g = (pl.program_id(1) * tc) // group
        for gi, w in enumerate(POOL_WINDOWS):
            half = w // 2

            @pl.when(g == gi)
            def _():
                v = s_ref[...].astype(F32)
                cnt = _window_count(v.shape, half, seq)
                if backward:
                    u = v / cnt
                    res = (_window_halves(u, half, seq) - _shift_rows(u, -half, seq) + _shift_rows(u, half, seq)) - v
                else:
                    res = _window_halves(v, half, seq) / cnt - v
                o_ref[...] = res.astype(o_ref.dtype)

    spec = pl.BlockSpec((seq, tc), lambda b, j: (b, j))
    return pl.pallas_call(
        body, name=name, grid=(n_seq, d // tc), in_specs=[spec], out_specs=spec,
        out_shape=jax.ShapeDtypeStruct(src.shape, out_dtype), compiler_params=_params("parallel", "parallel"),
    )(src)


def _split3(x):
    x1 = x.astype(BF16)
    r1 = x - x1.astype(F32)
    x2 = r1.astype(BF16)
    x3 = (r1 - x2.astype(F32)).astype(BF16)
    return x1, x2, x3


def _tri_dot(tri, x):
    x1, x2, x3 = _split3(x)
    return _dot(tri, x1, NN) + _dot(tri, x2, NN) + _dot(tri, x3, NN)


def _chunk_terms(q_ref, k_ref, g_ref, rev, scale):
    c = q_ref.shape[0]
    ri = lax.broadcasted_iota(jnp.int32, (c, c), 0)
    ci = lax.broadcasted_iota(jnp.int32, (c, c), 1)
    causal = (ci >= ri) if rev else (ci <= ri)
    mid = c // 2 if rev else c // 2 - 1
    last = 0 if rev else c - 1
    b = _tri_dot(causal.astype(BF16), g_ref[...])
    b_mid, b_last = b[mid:mid + 1, :], b[last:last + 1, :]
    e_qm, e_km, e_qi, e_ks = jnp.exp(b - b_mid), jnp.exp(b_mid - b), jnp.exp(b), jnp.exp(b_last - b)
    q = q_ref[...].astype(F32) * scale
    k = k_ref[...].astype(F32)
    return dict(causal=causal, mid=mid, last=last, decay=jnp.exp(b_last), e_qm=e_qm, e_km=e_km, e_qi=e_qi, e_ks=e_ks,
                qm=q * e_qm, km=k * e_km, qi=q * e_qi, ks=k * e_ks)


def _gla_specs(n_chunks, heads, dk, dv, key_dim, rev, backward):
    def chunk(n):
        scan_pos = (n_chunks - 1 - n) if backward else n
        return (n_chunks - 1 - scan_pos) if rev else scan_pos

    def at(width, col0):
        return pl.BlockSpec((CHUNK, width), lambda b, h, n: (b * n_chunks + chunk(n), col0 // width + h))

    state = pl.BlockSpec((None, dv, dk), lambda b, h, n: ((b * heads + h) * n_chunks + chunk(n), 0, 0))
    return at(dk, 0), at(dk, key_dim), at(dv, 2 * key_dim), at(dk, 0), at(dv, 0), state


def _gla_fwd(name, pm, log_a, *, rev, n_seq, seq, key_dim, val_dim):
    heads, n_chunks = GLA_HEADS, seq // CHUNK
    dk, dv = key_dim // heads, val_dim // heads
    scale = dk ** -0.5
    q_spec, k_spec, v_spec, g_spec, o_spec, st_spec = _gla_specs(n_chunks, heads, dk, dv, key_dim, rev, False)

    def body(q_ref, k_ref, v_ref, g_ref, o_ref, st_ref, state):
        @pl.when(pl.program_id(2) == 0)
        def _():
            state[...] = jnp.zeros_like(state)

        t = _chunk_terms(q_ref, k_ref, g_ref, rev, scale)
        v = v_ref[...]
        p = jnp.where(t["causal"], _dot(t["qm"].astype(BF16), t["km"].astype(BF16), NT), 0.0)
        entering = state[...].astype(BF16)
        st_ref[...] = entering
        o_ref[...] = _dot(p.astype(BF16), v, NN) + _dot(t["qi"].astype(BF16), entering, NT)
        state[...] = t["decay"] * state[...] + _dot(v, t["ks"].astype(BF16), TN)

    tokens = n_seq * seq
    return pl.pallas_call(
        body, name=name, grid=(n_seq, heads, n_chunks), in_specs=[q_spec, k_spec, v_spec, g_spec],
        out_specs=[o_spec, st_spec],
        out_shape=[jax.ShapeDtypeStruct((tokens, val_dim), F32),
                   jax.ShapeDtypeStruct((n_seq * heads * n_chunks, dv, dk), BF16)],
        scratch_shapes=[pltpu.VMEM((dv, dk), F32)], compiler_params=_params("parallel", "parallel", "arbitrary"),
    )(pm, pm, pm, log_a)


def _gla_bwd(name, pm, log_a, states, d_o, *, rev, n_seq, seq, key_dim, val_dim):
    heads, n_chunks = GLA_HEADS, seq // CHUNK
    dk, dv = key_dim // heads, val_dim // heads
    scale = dk ** -0.5
    q_spec, k_spec, v_spec, g_spec, o_spec, st_spec = _gla_specs(n_chunks, heads, dk, dv, key_dim, rev, True)

    def body(q_ref, k_ref, v_ref, g_ref, do_ref, st_ref, dq_ref, dk_ref, dv_ref, dg_ref, dstate):
        @pl.when(pl.program_id(2) == 0)
        def _():
            dstate[...] = jnp.zeros_like(dstate)

        t = _chunk_terms(q_ref, k_ref, g_ref, rev, scale)
        causal = t["causal"]
        v, d_out, entering = v_ref[...], do_ref[...], st_ref[...]
        qm, km, qi, ks = (t[n].astype(BF16) for n in ("qm", "km", "qi", "ks"))
        dst = dstate[...]
        dst16 = dst.astype(BF16)
        p = jnp.where(causal, _dot(qm, km, NT), 0.0).astype(BF16)
        dp = jnp.where(causal, _dot(d_out, v, NT), 0.0).astype(BF16)
        dv_ref[...] = _dot(p, d_out, TN) + _dot(ks, dst16, NT)
        dqm, dkm = _dot(dp, km, NN), _dot(dp, qm, TN)
        dqi, dks = _dot(d_out, entering, NN), _dot(v, dst16, NN)
        d_decay = _colsum(dst * entering.astype(F32))
        dq_ref[...] = (dqi * t["e_qi"] + dqm * t["e_qm"]) * scale
        dk_ref[...] = dkm * t["e_km"] + dks * t["e_ks"]
        qm_term, km_term = dqm * t["qm"], dkm * t["km"]
        ks_term = dks * t["ks"]
        db = dqi * t["qi"] + qm_term - km_term - ks_term
        row = lax.broadcasted_iota(jnp.int32, db.shape, 0)
        db = db + jnp.where(row == t["mid"], _colsum(km_term - qm_term), 0.0)
        db = db + jnp.where(row == t["last"], _colsum(ks_term) + d_decay * t["decay"], 0.0)
        ri = lax.broadcasted_iota(jnp.int32, causal.shape, 0)
        ci = lax.broadcasted_iota(jnp.int32, causal.shape, 1)
        anti = (ci <= ri) if rev else (ci >= ri)
        dg_ref[...] = _tri_dot(anti.astype(BF16), db)
        dstate[...] = t["decay"] * dst + _dot(d_out, qi, TN)

    tokens = n_seq * seq
    kd = jax.ShapeDtypeStruct((tokens, key_dim), F32)
    return pl.pallas_call(
        body, name=name, grid=(n_seq, heads, n_chunks),
        in_specs=[q_spec, k_spec, v_spec, g_spec, o_spec, st_spec], out_specs=[g_spec, g_spec, o_spec, g_spec],
        out_shape=[kd, kd, jax.ShapeDtypeStruct((tokens, val_dim), F32), kd],
        scratch_shapes=[pltpu.VMEM((dv, dk), F32)], compiler_params=_params("parallel", "parallel", "arbitrary"),
    )(pm, pm, pm, log_a, d_o, states)


def _sigmoid_neg(z):
    e = jnp.exp(-jnp.abs(z))
    return jnp.where(z >= 0, e / (1.0 + e), 1.0 / (1.0 + e))


def _gate_fwd(name, pr, wf, bf, wb, bb, tr):
    kd = wf.shape[1]

    def fn(r, w_f, b_f, w_b, b_b):
        r16 = r.astype(BF16)
        out = []
        for w, b in ((w_f, b_f), (w_b, b_b)):
            z = _dot(r16, w.astype(BF16), NN) + b
            out.append((jnp.minimum(z, 0.0) - jnp.log1p(jnp.exp(-jnp.abs(z)))) / GATE_TAU)
        return tuple(out)

    return _rowwise(name, fn, n_rows=pr.shape[0], tr=tr, rows=[(pr, LANES, 0)], bcast=[wf, bf, wb, bb],
                    outs=[(kd, F32), (kd, F32)])


def _gate_bwd(name, pr, wf, bf, wb, bb, dgf, dgb, tr):
    kd = wf.shape[1]

    def fn(r, d_f, d_b, w_f, b_f, w_b, b_b):
        r16 = r.astype(BF16)
        dpr = jnp.zeros(r.shape, F32)
        sums = []
        for w, b, dg in ((w_f, b_f, d_f), (w_b, b_b, d_b)):
            w16 = w.astype(BF16)
            z = _dot(r16, w16, NN) + b
            dz = dg / GATE_TAU * _sigmoid_neg(z)
            dz16 = dz.astype(BF16)
            dpr = dpr + _dot(dz16, w16, NT)
            sums += [_dot(r16, dz16, TN), _colsum(dz)]
        return (dpr, *sums)

    return _rowwise(name, fn, n_rows=pr.shape[0], tr=tr, rows=[(pr, LANES, 0), (dgf, kd, 0), (dgb, kd, 0)],
                    bcast=[wf, bf, wb, bb], outs=[(LANES, BF16)], accs=[(LANES, kd), (1, kd), (LANES, kd), (1, kd)])


def _head_norm_fwd(name, of, ob, pm, gn, *, val_dim, gate_block, tr):
    dv = val_dim // GLA_HEADS

    def fn(o_f, o_b, gate, gain):
        out = []
        for h in range(GLA_HEADS):
            sl = slice(h * dv, (h + 1) * dv)
            o = o_f[:, sl] + o_b[:, sl]
            gt = gate[:, sl].astype(F32)
            on = o * lax.rsqrt(jnp.mean(o * o, axis=-1, keepdims=True) + EPS) * gain
            out.append(on * (gt * jax.nn.sigmoid(gt)))
        return (jnp.concatenate(out, axis=1),)

    return _rowwise(name, fn, n_rows=of.shape[0], tr=tr, rows=[(of, val_dim, 0), (ob, val_dim, 0), (pm, val_dim, gate_block)],
                    bcast=[gn], outs=[(val_dim, BF16)])[0]


def _head_norm_bwd(name, of, ob, pm, gn, dz, *, val_dim, gate_block, tr):
    dv = val_dim // GLA_HEADS

    def fn(o_f, o_b, gate, d_z, gain):
        d_o, d_gate = [], []
        d_gain = jnp.zeros((1, dv), F32)
        for h in range(GLA_HEADS):
            sl = slice(h * dv, (h + 1) * dv)
            o = o_f[:, sl] + o_b[:, sl]
            gt = gate[:, sl].astype(F32)
            dzh = d_z[:, sl]
            r = lax.rsqrt(jnp.mean(o * o, axis=-1, keepdims=True) + EPS)
            ohat = o * r
            sg = jax.nn.sigmoid(gt)
            d_gate.append(dzh * (ohat * gain) * (sg * (1.0 + gt * (1.0 - sg))))
            don = dzh * (gt * sg)
            d_gain = d_gain + _colsum(don * ohat)
            gd = gain * don
            d_o.append(r * (gd - ohat * jnp.mean(gd * ohat, axis=-1, keepdims=True)))
        return jnp.concatenate(d_o, axis=1), jnp.concatenate(d_gate, axis=1), d_gain

    return _rowwise(name, fn, n_rows=of.shape[0], tr=tr,
                    rows=[(of, val_dim, 0), (ob, val_dim, 0), (pm, val_dim, gate_block), (dz, val_dim, 0)], bcast=[gn],
                    outs=[(val_dim, BF16), (val_dim, BF16)], accs=[(1, dv)])


def _proj_grad(name, dqf, dqb, dkf, dkb, dvf, dvb, dgate, *, key_dim, val_dim, tr):
    def fn(q1, q2, k1, k2, v1, v2, gt):
        return (jnp.concatenate([(q1 + q2).astype(BF16), (k1 + k2).astype(BF16), (v1 + v2).astype(BF16), gt], axis=1),)

    rows = [(dqf, key_dim, 0), (dqb, key_dim, 0), (dkf, key_dim, 0), (dkb, key_dim, 0), (dvf, val_dim, 0),
            (dvb, val_dim, 0), (dgate, val_dim, 0)]
    return _rowwise(name, fn, n_rows=dqf.shape[0], tr=tr, rows=rows, outs=[(2 * key_dim + 2 * val_dim, BF16)])[0]


def _pool_out_grad(name, dh, y_pre, scale, tr):
    def fn(d, y, s):
        return d * s, _colsum(d * y)

    dm = dh.shape[1]
    return _rowwise(name, fn, n_rows=dh.shape[0], tr=tr, rows=[(dh, dm, 0), (y_pre, dm, 0)], bcast=[scale],
                    outs=[(dm, BF16)], accs=[(1, dm)])


def _mesh_place():
    x, y, c = lax.axis_index("x"), lax.axis_index("y"), lax.axis_index("c")
    return x, y, c, 4 * x + 2 * y + c


def _peer(x, y, c, p):
    px, py, pc = (x if not p & 4 else 1 - x), (y if not p & 2 else 1 - y), (c if not p & 1 else 1 - c)
    return (px, py, pc), 4 * px + 2 * py + pc


CHIP_BITS = (2, 4, 6)
N_CHIP = N_DEV // 2
GATHER_COPIES = 1 + 2 * len(CHIP_BITS)


def _transfer_sems(n_gather, n_scatter, n_pair):
    n_remote = n_gather * GATHER_COPIES + n_scatter * len(CHIP_BITS) + n_pair * N_CHIP
    return [pltpu.SemaphoreType.DMA((n_remote,)), pltpu.SemaphoreType.DMA((n_remote,)),
            pltpu.SemaphoreType.DMA((max(n_gather, 1),))]


def _transfers(g_in, g_out, s_in, s_out, p_in, p_out, send, recv, loc):
    x, y, c, me = _mesh_place()
    sibling, _ = _peer(x, y, c, 1)

    def remote(src, dst, k, peer):
        return pltpu.make_async_remote_copy(src_ref=src, dst_ref=dst, send_sem=send.at[k], recv_sem=recv.at[k],
                                            device_id=peer, device_id_type=MESH_ID)

    local, first, arrivals, forwards, forward_arrivals = [], [], [], [], []
    k = 0
    for src, out in zip(g_in, g_out):
        local.append(pltpu.make_async_copy(src, out.at[me], loc.at[len(local)]))
        for p in (1,) + CHIP_BITS:
            peer, block = _peer(x, y, c, p)
            first.append(remote(src, out.at[me], k, peer))
            arrivals.append(remote(src, out.at[block], k, peer))
            k += 1
        for p in CHIP_BITS:
            _, block = _peer(x, y, c, p)
            _, sibling_block = _peer(x, y, c, p ^ 1)
            forwards.append(remote(out.at[block], out.at[block], k, sibling))
            forward_arrivals.append(remote(out.at[sibling_block], out.at[sibling_block], k, sibling))
            k += 1
    for src, out in zip(s_in, s_out):
        for j, p in enumerate(CHIP_BITS):
            peer, _ = _peer(x, y, c, p)
            peer_chip = 2 * peer[0] + peer[1]
            first.append(remote(src.at[peer_chip], out.at[j], k, peer))
            arrivals.append(remote(src.at[peer_chip], out.at[j], k, peer))
            k += 1
    for src, out in zip(p_in, p_out):
        for q in range(N_CHIP):
            first.append(remote(src.at[2 * q + 1 - c], out.at[q], k, sibling))
            arrivals.append(remote(src.at[2 * q + 1 - c], out.at[q], k, sibling))
            k += 1

    def start():
        for cp in local + first:
            cp.start()

    def finish():
        for cp in arrivals:
            cp.wait_recv()
        for cp in forwards:
            cp.start()
        for cp in forward_arrivals:
            cp.wait_recv()
        for cp in first + forwards:
            cp.wait_send()
        for cp in local:
            cp.wait()

    return start, finish


def _exchange(name, gathers=(), scatters=(), pairs=()):
    n_g, n_s, n_p = len(gathers), len(scatters), len(pairs)
    n_in = n_g + n_s + n_p

    def body(*refs):
        ins, outs = refs[:n_in], refs[n_in:-3]
        start, finish = _transfers(ins[:n_g], outs[:n_g], ins[n_g:n_g + n_s], outs[n_g:n_g + n_s], ins[n_g + n_s:],
                                   outs[n_g + n_s:], *refs[-3:])
        start()
        finish()

    hbm = pl.BlockSpec(memory_space=pl.ANY)
    out_shape = [jax.ShapeDtypeStruct((N_DEV,) + g.shape, g.dtype) for g in gathers]
    out_shape += [jax.ShapeDtypeStruct((len(CHIP_BITS),) + t.shape[1:], t.dtype) for t in scatters]
    out_shape += [jax.ShapeDtypeStruct((N_CHIP,) + t.shape[1:], t.dtype) for t in pairs]
    return pl.pallas_call(
        body, name=name, in_specs=[hbm] * n_in, out_specs=[hbm] * len(out_shape), out_shape=out_shape,
        scratch_shapes=_transfer_sems(n_g, n_s, n_p),
    )(*gathers, *scatters, *pairs)


def _place_scalar(axis_value):
    return jnp.asarray(axis_value, jnp.int32).reshape(1)


def _chip_sums(name, grad):
    received = _exchange(name + "_pair", pairs=[grad])[0]
    cols = grad.shape[-1]
    rows = received.size // cols // N_CHIP
    tr = _row_tile(rows, (512 * 1024) // cols)

    def body(core_ref, mine_ref, got_ref, out_ref):
        out_ref[...] = (mine_ref[...].astype(F32) + got_ref[...].astype(F32)).astype(out_ref.dtype)

    block = (None, tr, cols)
    out = pl.pallas_call(
        body, name=name + "_add", out_shape=jax.ShapeDtypeStruct((N_CHIP, rows, cols), grad.dtype),
        grid_spec=pltpu.PrefetchScalarGridSpec(
            num_scalar_prefetch=1, grid=(N_CHIP, rows // tr),
            in_specs=[pl.BlockSpec(block, lambda q, i, core: (2 * q + core[0], i, 0)),
                      pl.BlockSpec(block, lambda q, i, core: (q, i, 0))],
            out_specs=pl.BlockSpec(block, lambda q, i, core: (q, i, 0))),
        compiler_params=_params("parallel", "parallel"),
    )(_place_scalar(lax.axis_index("c")), grad.reshape(N_DEV, rows, cols), received.reshape(N_CHIP, rows, cols))
    return out.reshape(received.shape)


def _all_reduce_small(name, pack):
    def body(in_ref, out_ref, parts, send_sems, recv_sems):
        x, y, c, me = _mesh_place()
        parts[me] = in_ref[...]

        def copy(p, arriving):
            peer, peer_block = _peer(x, y, c, p)
            return pltpu.make_async_remote_copy(src_ref=in_ref, dst_ref=parts.at[peer_block if arriving else me],
                                                send_sem=send_sems.at[p - 1], recv_sem=recv_sems.at[p - 1],
                                                device_id=peer, device_id_type=MESH_ID)

        sends = [copy(p, False) for p in range(1, N_DEV)]
        for cp in sends:
            cp.start()
        for p in range(1, N_DEV):
            copy(p, True).wait_recv()
        for cp in sends:
            cp.wait_send()
        total = parts[0]
        for j in range(1, N_DEV):
            total = total + parts[j]
        out_ref[...] = total

    vmem = pl.BlockSpec(memory_space=pltpu.VMEM)
    return pl.pallas_call(
        body, name=name, in_specs=[vmem], out_specs=vmem, out_shape=jax.ShapeDtypeStruct(pack.shape, F32),
        scratch_shapes=[pltpu.VMEM((N_DEV,) + pack.shape, F32), pltpu.SemaphoreType.DMA((N_PEER,)),
                        pltpu.SemaphoreType.DMA((N_PEER,))],
        compiler_params=pltpu.CompilerParams(vmem_limit_bytes=VMEM_LIMIT_BYTES),
    )(pack)


def _adamw_math(w, g, m, v):
    m = ADAM_B1 * m + (1.0 - ADAM_B1) * g
    v = ADAM_B2 * v + (1.0 - ADAM_B2) * jnp.square(g)
    m_hat = m / (1.0 - ADAM_B1 ** ADAM_STEP)
    v_hat = v / (1.0 - ADAM_B2 ** ADAM_STEP)
    delta = -ADAM_LR * (m_hat / (jnp.sqrt(v_hat) + ADAM_EPS) + ADAM_WD * w)
    return delta, m, v


def _adamw_shard(name, w, m, v, grad_parts, layer=0, into=None):
    landed, sums = grad_parts
    shape = w.shape
    n_layers, cols = shape[0], shape[-1]
    rows = w.size // cols // n_layers
    tr = _row_tile(rows, (256 * 1024) // cols)
    n_into = 0 if into is None else 4
    n_landed = landed.shape[0]

    def body(chip_ref, w_ref, m_ref, v_ref, mine_ref, got_ref, *refs):
        g_ref, d_ref, nm_ref, nv_ref = refs[n_into:]
        g = mine_ref[...].astype(F32)
        for j in range(n_landed):
            g = g + got_ref[j].astype(F32)
        delta, new_m, new_v = _adamw_math(w_ref[...], g, m_ref[...], v_ref[...])
        g_ref[...], d_ref[...], nm_ref[...], nv_ref[...] = g, delta, new_m, new_v

    view = (n_layers, rows, cols)
    spec = pl.BlockSpec((None, tr, cols), lambda i, chip: (layer, i, 0))
    operands = [w.reshape(view), m.reshape(view), v.reshape(view), sums.reshape(N_CHIP, rows, cols),
                landed.reshape(n_landed, rows, cols)]
    in_specs = [spec, spec, spec, pl.BlockSpec((None, tr, cols), lambda i, chip: (chip[0], i, 0)),
                pl.BlockSpec((n_landed, tr, cols), lambda i, chip: (0, i, 0))]
    if into is not None:
        operands += [t.reshape(view) for t in into]
        in_specs += [pl.BlockSpec(memory_space=pl.ANY)] * 4
    first_into = 1 + 5
    res = pl.pallas_call(
        body, name=name, out_shape=[jax.ShapeDtypeStruct(view, F32)] * 4,
        grid_spec=pltpu.PrefetchScalarGridSpec(num_scalar_prefetch=1, grid=(rows // tr,), in_specs=in_specs, out_specs=[spec] * 4),
        input_output_aliases={first_into + t: t for t in range(n_into)}, compiler_params=_params("parallel"),
    )(_place_scalar(2 * lax.axis_index("x") + lax.axis_index("y")), *operands)
    return [r.reshape(shape) for r in res]


def _adamw_small(name, w, m, v, g):
    shape = w.shape
    two_d = (1, shape[0]) if len(shape) == 1 else (w.size // shape[-1], shape[-1])

    def body(w_ref, m_ref, v_ref, g_ref, d_ref, nm_ref, nv_ref):
        d_ref[...], nm_ref[...], nv_ref[...] = _adamw_math(w_ref[...], g_ref[...], m_ref[...], v_ref[...])

    res = pl.pallas_call(body, name=name, out_shape=[jax.ShapeDtypeStruct(two_d, F32)] * 3)(
        w.reshape(two_d), m.reshape(two_d), v.reshape(two_d), g.reshape(two_d))
    return [r.reshape(shape) for r in res]


def _mlp_fwd(tag, h_in, gain, w1, w2_mine, down_gathers, tm, tr):
    tokens, d = h_in.shape
    fs = w1.shape[2]
    f = N_DEV * fs
    hn = _rms_fwd(f"rms_mlp{tag}", h_in, gain, BF16, tr)
    tn, tk = _tile(fs, 1024), _tile(d, 512)
    per = fs // tn

    def relu2(acc):
        a = jnp.maximum(acc, 0.0)
        return acc, a * a

    a_pre, act, w2 = _mm(
        f"mlp_up{tag}", hn, w1, grid=(tokens // tm, f // tn, d // tk), a_spec=_spec2(tm, tk, lambda m, n, k: (m, k)),
        b_spec=pl.BlockSpec((None, tk, tn), lambda m, n, k: (n // per, k, n % per)),
        o_spec=_spec2(tm, tn, lambda m, n, k: (m, n)), acc_shape=(tm, tn), outs=[((tokens, f), BF16)] * 2, dims=NN, epi=relu2,
        gathers=[w2_mine])
    tn2, tk2 = _tile(d, 1024), _tile(fs, 512)
    kper = fs // tk2
    h_out, *got = _mm(
        f"mlp_down{tag}", act, w2, grid=(tokens // tm, d // tn2, f // tk2), a_spec=_spec2(tm, tk2, lambda m, n, k: (m, k)),
        b_spec=pl.BlockSpec((None, tk2, tn2), lambda m, n, k: (k // kper, k % kper, n)),
        o_spec=_spec2(tm, tn2, lambda m, n, k: (m, n)), acc_shape=(tm, tn2), outs=[((tokens, d), F32)], dims=NN,
        extras=[(h_in, _spec2(tm, tn2, lambda m, n, k: (m, n)))], epi=lambda acc, res: (res + acc,), gathers=down_gathers)
    return hn, a_pre, act, h_out, w2, got


def _mlp_bwd(tag, h_in, gain, hn, a_pre, act, dh, dh16, w1, w2, tm, tr):
    tokens, d = h_in.shape
    fs = w1.shape[2]
    f = N_DEV * fs
    tn, tk = _tile(fs, 1024), _tile(d, 512)
    nper = fs // tn
    da = _mm(
        f"mlp_dact{tag}", dh16, w2, grid=(tokens // tm, f // tn, d // tk), a_spec=_spec2(tm, tk, lambda m, n, k: (m, k)),
        b_spec=pl.BlockSpec((None, tn, tk), lambda m, n, k: (n // nper, n % nper, k)),
        o_spec=_spec2(tm, tn, lambda m, n, k: (m, n)), acc_shape=(tm, tn), outs=[((tokens, f), BF16)], dims=NT,
        extras=[(a_pre, _spec2(tm, tn, lambda m, n, k: (m, n)))],
        epi=lambda acc, pre: (acc * (2.0 * jnp.maximum(pre.astype(F32), 0.0)),))[0]
    tkt = _tile(tokens, 512)
    tm2, tn2 = _tile(fs, 1024), _tile(d, 1024)
    mper = fs // tm2
    dw2 = _mm(
        f"mlp_dw_out{tag}", act, dh16, grid=(f // tm2, d // tn2, tokens // tkt), a_spec=_spec2(tkt, tm2, lambda m, n, k: (k, m)),
        b_spec=_spec2(tkt, tn2, lambda m, n, k: (k, n)),
        o_spec=pl.BlockSpec((None, tm2, tn2), lambda m, n, k: (m // mper, m % mper, n)), acc_shape=(tm2, tn2),
        outs=[((N_DEV, fs, d), BF16)], dims=TN)[0]
    dw2_sums = _chip_sums(f"mlp_dw_out{tag}", dw2)
    tm3 = _tile(d, 1024)
    dw1, dw2_landed = _mm(
        f"mlp_dw_in{tag}", hn, da, grid=(d // tm3, f // tn, tokens // tkt), a_spec=_spec2(tkt, tm3, lambda m, n, k: (k, m)),
        b_spec=_spec2(tkt, tn, lambda m, n, k: (k, n)),
        o_spec=pl.BlockSpec((None, tm3, tn), lambda m, n, k: (n // nper, m, n % nper)), acc_shape=(tm3, tn),
        outs=[((N_DEV, d, fs), BF16)], dims=TN, scatters=[dw2_sums])
    dw1_sums = _chip_sums(f"mlp_dw_in{tag}", dw1)
    tn4, tk4 = _tile(d, 1024), _tile(fs, 512)
    kper = fs // tk4
    dhn, dw1_landed = _mm(
        f"mlp_dx{tag}", da, w1, grid=(tokens // tm, d // tn4, f // tk4), a_spec=_spec2(tm, tk4, lambda m, n, k: (m, k)),
        b_spec=pl.BlockSpec((None, tn4, tk4), lambda m, n, k: (k // kper, n, k % kper)),
        o_spec=_spec2(tm, tn4, lambda m, n, k: (m, n)), acc_shape=(tm, tn4), outs=[((tokens, d), F32)], dims=NT,
        scatters=[dw1_sums])
    dh_in, dh_in16, dgain = _rms_bwd(f"rms_mlp_bwd{tag}", h_in, gain, dhn, dh, tr)
    return dh_in, dh_in16, dgain, (dw1_landed, dw1_sums), (dw2_landed, dw2_sums)


def _plain_mm(name, a, b, dims, out_dtype, *, tm, tn_pref=1024, tk_pref=512, extras=None, epi=None, gathers=(), scatters=()):
    if dims == NN:
        (m, k), n = a.shape, b.shape[1]
    elif dims == NT:
        (m, k), n = a.shape, b.shape[0]
    else:
        (k, m), n = a.shape, b.shape[1]
    tm, tn, tk = _tile(m, tm), _tile(n, tn_pref), _tile(k, tk_pref)
    a_spec = _spec2(tk, tm, lambda i, j, l: (l, i)) if dims == TN else _spec2(tm, tk, lambda i, j, l: (i, l))
    b_spec = _spec2(tn, tk, lambda i, j, l: (j, l)) if dims == NT else _spec2(tk, tn, lambda i, j, l: (l, j))
    o_spec = _spec2(tm, tn, lambda i, j, l: (i, j))
    ex = [(e, o_spec) for e in (extras or [])]
    res = _mm(name, a, b, grid=(m // tm, n // tn, k // tk), a_spec=a_spec, b_spec=b_spec, o_spec=o_spec, acc_shape=(tm, tn),
              outs=[((m, n), out_dtype)], dims=dims, extras=ex, epi=epi, gathers=gathers, scatters=scatters)
    return res if (gathers or scatters) else res[0]


def kernel(x, norm_mix, norm_mlp, norm_final, pool_w, pool_scale, gla_w_in, gla_w_up_f, gla_b_up_f, gla_w_up_b, gla_b_up_b, gla_g_norm, gla_w_out, mlp_w_in, mlp_w_out, loss_target, m_norm_mix, m_norm_mlp, m_norm_final, m_pool_w, m_pool_scale, m_gla_w_in, m_gla_w_up_f, m_gla_b_up_f, m_gla_w_up_b, m_gla_b_up_b, m_gla_g_norm, m_gla_w_out, m_mlp_w_in, m_mlp_w_out, v_norm_mix, v_norm_mlp, v_norm_final, v_pool_w, v_pool_scale, v_gla_w_in, v_gla_w_up_f, v_gla_b_up_f, v_gla_w_up_b, v_gla_b_up_b, v_gla_g_norm, v_gla_w_out, v_mlp_w_in, v_mlp_w_out):
    n_seq, seq, d = x.shape
    tokens = n_seq * seq
    group = d // N_POOL_GROUPS
    key_dim, val_dim = d // 2, d
    dv = val_dim // GLA_HEADS
    n_main = 2 * key_dim + 2 * val_dim
    proj_shard = gla_w_in.shape[2]
    proj_width = N_DEV * proj_shard
    kd_shard = key_dim // N_DEV
    assert proj_width == n_main + 2 * GATE_RANK and seq % CHUNK == 0 and 2 * GATE_RANK <= LANES
    assert mlp_w_in.shape[0] == 2 and pool_w.shape[0] == 1 and gla_w_in.shape[0] == 1
    me = 4 * lax.axis_index("x") + 2 * lax.axis_index("y") + lax.axis_index("c")
    tm = _tile(tokens, 1024)
    tr = _tile(tokens, 128)
    x2, target = x.reshape(tokens, d), loss_target.reshape(tokens, d)

    small_rows = 2 * GATE_RANK + 3
    small = jnp.concatenate([
        gla_w_up_f[0], gla_w_up_b[0], gla_b_up_f, gla_b_up_b,
        jnp.pad(gla_g_norm, ((0, 0), (0, kd_shard - gla_g_norm.shape[1])))], axis=0)
    w1_mine = [mlp_w_in[l].astype(BF16) for l in range(2)]
    w2_mine = [mlp_w_out[l].astype(BF16) for l in range(2)]
    pool_g, small_g, w1_0 = _exchange("gather_first", gathers=[pool_w[0].astype(BF16), small, w1_mine[0]])
    w_pool = pool_g.transpose(1, 0, 2, 3).reshape(N_POOL_GROUPS, group, group)
    small_full = small_g.transpose(1, 0, 2).reshape(small_rows, key_dim)
    pad_rows = lambda w, r0: jnp.pad(w, ((r0, LANES - r0 - GATE_RANK), (0, 0)))
    w_up_f, w_up_b = pad_rows(small_full[:GATE_RANK], 0), pad_rows(small_full[GATE_RANK:2 * GATE_RANK], GATE_RANK)
    b_up_f, b_up_b = small_full[2 * GATE_RANK:2 * GATE_RANK + 1], small_full[2 * GATE_RANK + 1:2 * GATE_RANK + 2]
    g_norm = small_g[:, 2 * GATE_RANK + 2, :gla_g_norm.shape[1]].reshape(1, dv)

    hn0 = _rms_fwd("rms_mix0", x2, norm_mix[0:1], F32, tr)
    pooled = _pool_slabs("pool_fwd", hn0, BF16, n_seq=n_seq, seq=seq, backward=False)
    tn, tk = _tile(group, 1024), _tile(group, 512)
    nper, kper = group // tn, group // tk
    y_pre, h1 = _mm(
        "pool_mix", pooled, w_pool, grid=(tokens // tm, d // tn, group // tk),
        a_spec=_spec2(tm, tk, lambda m, n, k: (m, (n // nper) * kper + k)),
        b_spec=pl.BlockSpec((None, tk, tn), lambda m, n, k: (n // nper, k, n % nper)),
        o_spec=_spec2(tm, tn, lambda m, n, k: (m, n)), acc_shape=(tm, tn), outs=[((tokens, d), F32)] * 2, dims=NN,
        extras=[(x2, _spec2(tm, tn, lambda m, n, k: (m, n))), (pool_scale, _spec2(1, tn, lambda m, n, k: (0, n)))],
        epi=lambda acc, res, s: (acc, res + acc * s))
    hn1, a_pre0, act0, h2, w2_0, (win_g, wout_g) = _mlp_fwd(
        "0", h1, norm_mlp[0:1], w1_0, w2_mine[0], [gla_w_in[0].astype(BF16), gla_w_out[0].astype(BF16)], tm, tr)
    w_in = win_g.transpose(1, 0, 2).reshape(d, proj_width)
    w_main = w_in[:, :n_main]
    w_r = jnp.pad(w_in[:, n_main:], ((0, 0), (0, LANES - 2 * GATE_RANK)))
    w_out = wout_g.reshape(val_dim, d)

    hn2 = _rms_fwd("rms_mix1", h2, norm_mix[1:2], BF16, tr)
    pm, w1_1 = _plain_mm("gla_proj", hn2, w_main, NN, BF16, tm=tm, gathers=[w1_mine[1]])
    pr = _plain_mm("gla_proj_gate", hn2, w_r, NN, F32, tm=tm)
    log_a_f, log_a_b = _gate_fwd("gla_gate", pr, w_up_f, b_up_f, w_up_b, b_up_b, tr)
    dims = dict(n_seq=n_seq, seq=seq, key_dim=key_dim, val_dim=val_dim)
    o_f, st_f = _gla_fwd("gla_scan_f", pm, log_a_f, rev=False, **dims)
    o_b, st_b = _gla_fwd("gla_scan_b", pm, log_a_b, rev=True, **dims)
    gate_block = (2 * key_dim + val_dim) // val_dim
    z = _head_norm_fwd("gla_head_norm", o_f, o_b, pm, g_norm, val_dim=val_dim, gate_block=gate_block, tr=tr)
    h3 = _plain_mm("gla_out", z, w_out, NN, F32, tm=tm, extras=[h2], epi=lambda acc, res: (res + acc,))
    hn3, a_pre1, act1, h4, w2_1, _ = _mlp_fwd("1", h3, norm_mlp[1:2], w1_1, w2_mine[1], [], tm, tr)

    dh4, dh4_16, loss_row, d_norm_final = _loss_head("loss_head", h4, target, norm_final.reshape(1, d), tr)

    dh3, dh3_16, d_norm_mlp1, dw1_1, dw2_1 = _mlp_bwd("1", h3, norm_mlp[1:2], hn3, a_pre1, act1, dh4, dh4_16, w1_1, w2_1, tm, tr)
    dz = _plain_mm("gla_out_dx", dh3_16, w_out, NT, F32, tm=tm)
    dw_out = _plain_mm("gla_out_dw", z, dh3_16, TN, BF16, tm=1024)
    dw_out_sums = _chip_sums("gla_out_dw", dw_out.reshape(N_DEV, val_dim // N_DEV, d))
    d_o, d_gate, d_g_norm = _head_norm_bwd("gla_head_norm_bwd", o_f, o_b, pm, g_norm, dz, val_dim=val_dim,
                                           gate_block=gate_block, tr=tr)
    dq_f, dk_f, dv_f, dla_f = _gla_bwd("gla_scan_bwd_f", pm, log_a_f, st_f, d_o, rev=False, **dims)
    dq_b, dk_b, dv_b, dla_b = _gla_bwd("gla_scan_bwd_b", pm, log_a_b, st_b, d_o, rev=True, **dims)
    dpr, dw_up_f, db_up_f, dw_up_b, db_up_b = _gate_bwd("gla_gate_bwd", pr, w_up_f, b_up_f, w_up_b, b_up_b, dla_f, dla_b, tr)
    dpm = _proj_grad("gla_proj_grad", dq_f, dq_b, dk_f, dk_b, dv_f, dv_b, d_gate, key_dim=key_dim, val_dim=val_dim, tr=tr)
    dw_main, dw_out_landed = _plain_mm("gla_proj_dw", hn2, dpm, TN, BF16, tm=1024, scatters=[dw_out_sums])
    dw_r = _plain_mm("gla_proj_gate_dw", hn2, dpr, TN, BF16, tm=1024)
    dw_in = jnp.concatenate([dw_main, dw_r[:, :2 * GATE_RANK]], axis=1)
    dw_in_sums = _chip_sums("gla_proj_dw", dw_in.reshape(d, N_DEV, proj_shard).transpose(1, 0, 2))
    dhn2, dw_in_landed = _plain_mm("gla_proj_dx", dpm, w_main, NT, F32, tm=tm, scatters=[dw_in_sums])
    dhn2 = _plain_mm("gla_proj_gate_dx", dpr, w_r, NT, F32, tm=tm, extras=[dhn2], epi=lambda acc, res: (res + acc,))
    dh2, dh2_16, d_norm_mix1 = _rms_bwd("rms_mix1_bwd", h2, norm_mix[1:2], dhn2, dh3, tr)

    dh1, _, d_norm_mlp0, dw1_0, dw2_0 = _mlp_bwd("0", h1, norm_mlp[0:1], hn1, a_pre0, act0, dh2, dh2_16, w1_0, w2_0, tm, tr)
    dyp, d_pool_scale = _pool_out_grad("pool_out_grad", dh1, y_pre, pool_scale, tr)
    dpooled = _mm(
        "pool_mix_dx", dyp, w_pool, grid=(tokens // tm, d // tn, group // tk),
        a_spec=_spec2(tm, tk, lambda m, n, k: (m, (n // nper) * kper + k)),
        b_spec=pl.BlockSpec((None, tn, tk), lambda m, n, k: (n // nper, n % nper, k)),
        o_spec=_spec2(tm, tn, lambda m, n, k: (m, n)), acc_shape=(tm, tn), outs=[((tokens, d), F32)], dims=NT)[0]
    tmw, tkt = _tile(group, 1024), _tile(tokens, 512)
    mper = group // tmw
    dw_pool = _mm(
        "pool_mix_dw", pooled, dyp, grid=(d // tmw, group // tn, tokens // tkt),
        a_spec=_spec2(tkt, tmw, lambda m, n, k: (k, m)), b_spec=_spec2(tkt, tn, lambda m, n, k: (k, (m // mper) * nper + n)),
        o_spec=pl.BlockSpec((None, tmw, tn), lambda m, n, k: (m // mper, m % mper, n)), acc_shape=(tmw, tn),
        outs=[((N_POOL_GROUPS, group, group), BF16)], dims=TN)[0]
    dw_pool_sums = _chip_sums("pool_mix_dw", dw_pool.reshape(N_POOL_GROUPS, N_DEV, group // N_DEV, group).transpose(1, 0, 2, 3))
    dw_pool_landed = _exchange("scatter_pool", scatters=[dw_pool_sums])[0]
    dhn0 = _pool_slabs("pool_bwd", dpooled, F32, n_seq=n_seq, seq=seq, backward=True)
    dx, _, d_norm_mix0 = _rms_bwd("rms_mix0_bwd", x2, norm_mix[0:1], dhn0, dh1, tr)

    pack_cols = key_dim
    rows_of = lambda a: a.reshape(-1, pack_cols)
    pieces = [rows_of(t) for t in (d_norm_mix0, d_norm_mix1, d_norm_mlp0, d_norm_mlp1, d_norm_final, d_pool_scale,
                                   dw_up_f[:GATE_RANK], dw_up_b[GATE_RANK:2 * GATE_RANK], db_up_f, db_up_b)]
    pieces += [jnp.pad(d_g_norm, ((0, 0), (0, pack_cols - dv))), jnp.pad(loss_row, ((0, 0), (0, pack_cols - LANES)))]
    n_rows = sum(p.shape[0] for p in pieces)
    pack = jnp.concatenate(pieces + [jnp.zeros((-n_rows % 8, pack_cols), F32)], axis=0)
    red = _all_reduce_small("reduce_small", pack)
    cuts, r0 = [], 0
    for p in pieces:
        cuts.append(red[r0:r0 + p.shape[0]])
        r0 += p.shape[0]
    g_norm_mix = jnp.concatenate([cuts[0].reshape(1, d), cuts[1].reshape(1, d)], axis=0)
    g_norm_mlp = jnp.concatenate([cuts[2].reshape(1, d), cuts[3].reshape(1, d)], axis=0)
    g_norm_final = cuts[4].reshape(d)
    g_pool_scale = cuts[5].reshape(1, d)
    shard_cols = lambda t, width: lax.dynamic_slice_in_dim(t, me * width, width, axis=1)
    g_w_up_f, g_w_up_b = shard_cols(cuts[6], kd_shard)[None], shard_cols(cuts[7], kd_shard)[None]
    g_b_up_f, g_b_up_b = shard_cols(cuts[8], kd_shard), shard_cols(cuts[9], kd_shard)
    g_g_norm = shard_cols(cuts[10][:, :dv], dv // N_DEV)
    loss = cuts[11][0, 0]

    big = {
        "pool_w": _adamw_shard("adamw_pool_w", pool_w, m_pool_w, v_pool_w, (dw_pool_landed, dw_pool_sums)),
        "gla_w_in": _adamw_shard("adamw_gla_w_in", gla_w_in, m_gla_w_in, v_gla_w_in, (dw_in_landed, dw_in_sums)),
        "gla_w_out": _adamw_shard("adamw_gla_w_out", gla_w_out, m_gla_w_out, v_gla_w_out, (dw_out_landed, dw_out_sums)),
    }
    for nm, w, m, v, landed in (("mlp_w_in", mlp_w_in, m_mlp_w_in, v_mlp_w_in, (dw1_0, dw1_1)),
                                ("mlp_w_out", mlp_w_out, m_mlp_w_out, v_mlp_w_out, (dw2_0, dw2_1))):
        layer1 = _adamw_shard(f"adamw_{nm}1", w, m, v, landed[1], layer=1)
        big[nm] = _adamw_shard(f"adamw_{nm}0", w, m, v, landed[0], layer=0, into=layer1)
    grads = {"norm_mix": g_norm_mix, "norm_mlp": g_norm_mlp, "norm_final": g_norm_final, "pool_scale": g_pool_scale,
             "gla_w_up_f": g_w_up_f, "gla_b_up_f": g_b_up_f, "gla_w_up_b": g_w_up_b, "gla_b_up_b": g_b_up_b, "gla_g_norm": g_g_norm}
    state = {"norm_mix": (norm_mix, m_norm_mix, v_norm_mix), "norm_mlp": (norm_mlp, m_norm_mlp, v_norm_mlp),
             "norm_final": (norm_final, m_norm_final, v_norm_final), "pool_scale": (pool_scale, m_pool_scale, v_pool_scale),
             "gla_w_up_f": (gla_w_up_f, m_gla_w_up_f, v_gla_w_up_f), "gla_b_up_f": (gla_b_up_f, m_gla_b_up_f, v_gla_b_up_f),
             "gla_w_up_b": (gla_w_up_b, m_gla_w_up_b, v_gla_w_up_b), "gla_b_up_b": (gla_b_up_b, m_gla_b_up_b, v_gla_b_up_b),
             "gla_g_norm": (gla_g_norm, m_gla_g_norm, v_gla_g_norm)}
    order = ["norm_mix", "norm_mlp", "norm_final", "pool_w", "pool_scale", "gla_w_in", "gla_w_up_f", "gla_b_up_f", "gla_w_up_b",
             "gla_b_up_b", "gla_g_norm", "gla_w_out", "mlp_w_in", "mlp_w_out"]
    out_g, out_d, out_m, out_v = [], [], [], []
    for nm in order:
        if nm in big:
            g, dl, new_m, new_v = big[nm]
        else:
            g = grads[nm]
            dl, new_m, new_v = _adamw_small(f"adamw_{nm}", *state[nm], g)
        out_g.append(g), out_d.append(dl), out_m.append(new_m), out_v.append(new_v)
    return (loss, dx.reshape(x.shape), *out_g, *out_d, *out_m, *out_v)
```

```python
import functools

import jax
import jax.numpy as jnp
from jax import lax
from jax.experimental import pallas as pl
from jax.experimental.pallas import tpu as pltpu

F32 = jnp.float32
BF16 = jnp.bfloat16
N_DEV = 8
N_PEER = N_DEV - 1
LANES = 128
VMEM_LIMIT_BYTES = 48 * 1024 * 1024
MESH_ID = pl.DeviceIdType.MESH
TILE_K = 1024

POOL_WINDOWS = (2, 4, 8, 16)
N_POOL_GROUPS = len(POOL_WINDOWS)
GLA_HEADS = 4
GATE_RANK = 16
GATE_TAU = 16.0
CHUNK = 64
EPS = 1e-6
ADAM_LR = 0.001
ADAM_B1 = 0.9
ADAM_B2 = 0.999
ADAM_EPS = 1e-08
ADAM_WD = 0.01
ADAM_STEP = 10

NN = ((1,), (0,))
NT = ((1,), (1,))
TN = ((0,), (0,))


def _tile(dim, pref):
    if dim <= pref:
        return dim
    t = (pref // LANES) * LANES
    while t > LANES and dim % t:
        t -= LANES
    assert dim % t == 0, (dim, pref)
    return t


def _row_tile(rows, pref):
    if rows <= pref:
        return rows
    t = max(8, pref // 8 * 8)
    while t > 8 and rows % t:
        t -= 8
    assert rows % t == 0, (rows, pref)
    return t


def _params(*sem):
    return pltpu.CompilerParams(dimension_semantics=sem, vmem_limit_bytes=VMEM_LIMIT_BYTES)


def _dot(a, b, dims):
    return lax.dot_general(a, b, (dims, ((), ())), preferred_element_type=F32)


def _mm(name, a, b, *, grid, a_spec, b_spec, o_spec, acc_shape, outs, dims, extras=(), epi=None, gathers=(), scatters=()):
    n_ex, n_out, nk = len(extras), len(outs), grid[2]
    n_g, n_s = len(gathers), len(scatters)
    carries = bool(n_g or n_s)

    def body(*refs):
        a_ref, b_ref = refs[0], refs[1]
        ex = refs[2:2 + n_ex]
        r0 = 2 + n_ex
        g_in, s_in = refs[r0:r0 + n_g], refs[r0 + n_g:r0 + n_g + n_s]
        r0 += n_g + n_s
        o_refs = refs[r0:r0 + n_out]
        g_out, s_out = refs[r0 + n_out:r0 + n_out + n_g], refs[r0 + n_out + n_g:r0 + n_out + n_g + n_s]
        acc = refs[r0 + n_out + n_g + n_s]
        i, j, k = pl.program_id(0), pl.program_id(1), pl.program_id(2)
        if carries:
            start, finish = _transfers(g_in, g_out, s_in, s_out, (), (), *refs[-3:])

            @pl.when((i == 0) & (j == 0) & (k == 0))
            def _():
                start()

        def part():
            return _dot(a_ref[...].astype(BF16), b_ref[...].astype(BF16), dims)

        def finish_tile(total):
            res = epi(total, *[e[...] for e in ex]) if epi is not None else (total,)
            for o, r in zip(o_refs, res):
                o[...] = r.astype(o.dtype)

        if nk == 1:
            finish_tile(part())
        else:
            @pl.when(k == 0)
            def _():
                acc[...] = part()

            @pl.when((k > 0) & (k < nk - 1))
            def _():
                acc[...] += part()

            @pl.when(k == nk - 1)
            def _():
                finish_tile(acc[...] + part())

        if carries:
            @pl.when((i == grid[0] - 1) & (j == grid[1] - 1) & (k == nk - 1))
            def _():
                finish()

    hbm = pl.BlockSpec(memory_space=pl.ANY)
    operands = [a, b] + [e for e, _ in extras] + list(gathers) + list(scatters)
    in_specs = [a_spec, b_spec] + [s for _, s in extras] + [hbm] * (n_g + n_s)
    out_shape = [jax.ShapeDtypeStruct(s, d) for s, d in outs]
    out_shape += [jax.ShapeDtypeStruct((N_DEV,) + g.shape, g.dtype) for g in gathers]
    out_shape += [jax.ShapeDtypeStruct((len(CHIP_BITS),) + t.shape[1:], t.dtype) for t in scatters]
    scratch = [pltpu.VMEM(acc_shape, F32)] + (_transfer_sems(n_g, n_s, 0) if carries else [])
    sem = ("arbitrary",) * 3 if carries else ("parallel", "parallel", "arbitrary")
    return pl.pallas_call(
        body, name=name, grid=grid, in_specs=in_specs, out_specs=[o_spec] * n_out + [hbm] * (n_g + n_s),
        out_shape=out_shape, scratch_shapes=scratch, compiler_params=_params(*sem),
    )(*operands)


def _spec2(rows, cols, fn):
    return pl.BlockSpec((rows, cols), fn)


def _rowwise(name, fn, *, n_rows, tr, rows, bcast=(), outs=(), accs=()):
    n_r, n_b, n_o, n_a = len(rows), len(bcast), len(outs), len(accs)

    def body(*refs):
        r_refs, b_refs = refs[:n_r], refs[n_r:n_r + n_b]
        o_refs, a_refs = refs[n_r + n_b:n_r + n_b + n_o], refs[n_r + n_b + n_o:]
        res = fn(*[r[...] for r in r_refs], *[b[...] for b in b_refs])
        for o, r in zip(o_refs, res[:n_o]):
            o[...] = r.astype(o.dtype)
        if n_a:
            @pl.when(pl.program_id(0) == 0)
            def _():
                for acc in a_refs:
                    acc[...] = jnp.zeros_like(acc)

            for acc, r in zip(a_refs, res[n_o:]):
                acc[...] += r

    in_specs = [pl.BlockSpec((tr, w), functools.partial(lambda i, cb: (i, cb), cb=cb)) for _, w, cb in rows]
    in_specs += [pl.BlockSpec(b.shape, lambda i: (0, 0)) for b in bcast]
    out_specs = [pl.BlockSpec((tr, w), lambda i: (i, 0)) for w, _ in outs]
    out_specs += [pl.BlockSpec(s, lambda i: (0, 0)) for s in accs]
    out_shape = [jax.ShapeDtypeStruct((n_rows, w), d) for w, d in outs]
    out_shape += [jax.ShapeDtypeStruct(s, F32) for s in accs]
    return pl.pallas_call(
        body, name=name, grid=(n_rows // tr,), in_specs=in_specs, out_specs=out_specs, out_shape=out_shape,
        compiler_params=_params("arbitrary"),
    )(*[r for r, _, _ in rows], *bcast)


def _colsum(t):
    return jnp.sum(t, axis=0, keepdims=True)


def _rms_fwd(name, h, g, dtype, tr):
    def fn(x, gain):
        r = lax.rsqrt(jnp.mean(x * x, axis=-1, keepdims=True) + EPS)
        return (x * r * gain,)

    d = h.shape[1]
    return _rowwise(name, fn, n_rows=h.shape[0], tr=tr, rows=[(h, d, 0)], bcast=[g], outs=[(d, dtype)])[0]


def _rms_bwd(name, h, g, dhn, dres, tr):
    def fn(x, dy, dr, gain):
        r = lax.rsqrt(jnp.mean(x * x, axis=-1, keepdims=True) + EPS)
        xhat = x * r
        gd = gain * dy
        dh = r * (gd - xhat * jnp.mean(gd * xhat, axis=-1, keepdims=True)) + dr
        return dh, dh, _colsum(dy * xhat)

    d = h.shape[1]
    return _rowwise(name, fn, n_rows=h.shape[0], tr=tr, rows=[(h, d, 0), (dhn, d, 0), (dres, d, 0)], bcast=[g],
                    outs=[(d, F32), (d, BF16)], accs=[(1, d)])


def _loss_head(name, h, target, g, tr):
    d = h.shape[1]

    def fn(x, tgt, gain):
        r = lax.rsqrt(jnp.mean(x * x, axis=-1, keepdims=True) + EPS)
        xhat = x * r
        err = xhat * gain - tgt
        loss = 0.5 * jnp.sum(jnp.mean(err * err, axis=-1, keepdims=True), axis=0, keepdims=True)
        dy = err / d
        gd = gain * dy
        dh = r * (gd - xhat * jnp.mean(gd * xhat, axis=-1, keepdims=True))
        return dh, dh, jnp.broadcast_to(loss, (1, LANES)), _colsum(dy * xhat)

    return _rowwise(name, fn, n_rows=h.shape[0], tr=tr, rows=[(h, d, 0), (target, d, 0)], bcast=[g],
                    outs=[(d, F32), (d, BF16)], accs=[(1, LANES), (1, d)])


def _shift_rows(h, k, seq):
    if k == 0:
        return h
    rolled = pltpu.roll(h, (-k) % seq, 0)
    t = lax.broadcasted_iota(jnp.int32, h.shape, 0)
    return jnp.where((t + k >= 0) & (t + k < seq), rolled, 0.0)


def _window_halves(h, half, seq):
    right = h
    left = _shift_rows(h, -1, seq)
    step = 1
    while step < half:
        right = right + _shift_rows(right, step, seq)
        left = left + _shift_rows(left, -step, seq)
        step *= 2
    return left + right


def _window_count(shape, half, seq):
    t = lax.broadcasted_iota(jnp.int32, shape, 0)
    return (jnp.minimum(t + half, seq) - jnp.maximum(t - half, 0)).astype(F32)


def _pool_slabs(name, src, out_dtype, *, n_seq, seq, backward):
    d = src.shape[1]
    group = d // N_POOL_GROUPS
    tc = _tile(group, 256)

    def body(s_ref, o_ref):
        g = (pl.program_id(1) * tc) // group
        for gi, w in enumerate(POOL_WINDOWS):
            half = w // 2

            @pl.when(g == gi)
            def _():
                v = s_ref[...].astype(F32)
                cnt = _window_count(v.shape, half, seq)
                if backward:
                    u = v / cnt
                    res = (_window_halves(u, half, seq) - _shift_rows(u, -half, seq) + _shift_rows(u, half, seq)) - v
                else:
                    res = _window_halves(v, half, seq) / cnt - v
                o_ref[...] = res.astype(o_ref.dtype)

    spec = pl.BlockSpec((seq, tc), lambda b, j: (b, j))
    return pl.pallas_call(
        body, name=name, grid=(n_seq, d // tc), in_specs=[spec], out_specs=spec,
        out_shape=jax.ShapeDtypeStruct(src.shape, out_dtype), compiler_params=_params("parallel", "parallel"),
    )(src)


def _split3(x):
    x1 = x.astype(BF16)
    r1 = x - x1.astype(F32)
    x2 = r1.astype(BF16)
    x3 = (r1 - x2.astype(F32)).astype(BF16)
    return x1, x2, x3


def _tri_dot(tri, x):
    x1, x2, x3 = _split3(x)
    return _dot(tri, x1, NN) + _dot(tri, x2, NN) + _dot(tri, x3, NN)


def _chunk_terms(q_ref, k_ref, g_ref, rev, scale):
    c = q_ref.shape[0]
    ri = lax.broadcasted_iota(jnp.int32, (c, c), 0)
    ci = lax.broadcasted_iota(jnp.int32, (c, c), 1)
    causal = (ci >= ri) if rev else (ci <= ri)
    mid = c // 2 if rev else c // 2 - 1
    last = 0 if rev else c - 1
    b = _tri_dot(causal.astype(BF16), g_ref[...])
    b_mid, b_last = b[mid:mid + 1, :], b[last:last + 1, :]
    e_qm, e_km, e_qi, e_ks = jnp.exp(b - b_mid), jnp.exp(b_mid - b), jnp.exp(b), jnp.exp(b_last - b)
    q = q_ref[...].astype(F32) * scale
    k = k_ref[...].astype(F32)
    return dict(causal=causal, mid=mid, last=last, decay=jnp.exp(b_last), e_qm=e_qm, e_km=e_km, e_qi=e_qi, e_ks=e_ks,
                qm=q * e_qm, km=k * e_km, qi=q * e_qi, ks=k * e_ks)


def _gla_specs(n_chunks, heads, dk, dv, key_dim, rev, backward):
    def chunk(n):
        scan_pos = (n_chunks - 1 - n) if backward else n
        return (n_chunks - 1 - scan_pos) if rev else scan_pos

    def at(width, col0):
        return pl.BlockSpec((CHUNK, width), lambda b, h, n: (b * n_chunks + chunk(n), col0 // width + h))

    state = pl.BlockSpec((None, dv, dk), lambda b, h, n: ((b * heads + h) * n_chunks + chunk(n), 0, 0))
    return at(dk, 0), at(dk, key_dim), at(dv, 2 * key_dim), at(dk, 0), at(dv, 0), state


def _gla_fwd(name, pm, log_a, *, rev, n_seq, seq, key_dim, val_dim):
    heads, n_chunks = GLA_HEADS, seq // CHUNK
    dk, dv = key_dim // heads, val_dim // heads
    scale = dk ** -0.5
    q_spec, k_spec, v_spec, g_spec, o_spec, st_spec = _gla_specs(n_chunks, heads, dk, dv, key_dim, rev, False)

    def body(q_ref, k_ref, v_ref, g_ref, o_ref, st_ref, state):
        @pl.when(pl.program_id(2) == 0)
        def _():
            state[...] = jnp.zeros_like(state)

        t = _chunk_terms(q_ref, k_ref, g_ref, rev, scale)
        v = v_ref[...]
        p = jnp.where(t["causal"], _dot(t["qm"].astype(BF16), t["km"].astype(BF16), NT), 0.0)
        entering = state[...].astype(BF16)
        st_ref[...] = entering
        o_ref[...] = _dot(p.astype(BF16), v, NN) + _dot(t["qi"].astype(BF16), entering, NT)
        state[...] = t["decay"] * state[...] + _dot(v, t["ks"].astype(BF16), TN)

    tokens = n_seq * seq
    return pl.pallas_call(
        body, name=name, grid=(n_seq, heads, n_chunks), in_specs=[q_spec, k_spec, v_spec, g_spec],
        out_specs=[o_spec, st_spec],
        out_shape=[jax.ShapeDtypeStruct((tokens, val_dim), F32),
                   jax.ShapeDtypeStruct((n_seq * heads * n_chunks, dv, dk), BF16)],
        scratch_shapes=[pltpu.VMEM((dv, dk), F32)], compiler_params=_params("parallel", "parallel", "arbitrary"),
    )(pm, pm, pm, log_a)


def _gla_bwd(name, pm, log_a, states, d_o, *, rev, n_seq, seq, key_dim, val_dim):
    heads, n_chunks = GLA_HEADS, seq // CHUNK
    dk, dv = key_dim // heads, val_dim // heads
    scale = dk ** -0.5
    q_spec, k_spec, v_spec, g_spec, o_spec, st_spec = _gla_specs(n_chunks, heads, dk, dv, key_dim, rev, True)

    def body(q_ref, k_ref, v_ref, g_ref, do_ref, st_ref, dq_ref, dk_ref, dv_ref, dg_ref, dstate):
        @pl.when(pl.program_id(2) == 0)
        def _():
            dstate[...] = jnp.zeros_like(dstate)

        t = _chunk_terms(q_ref, k_ref, g_ref, rev, scale)
        causal = t["causal"]
        v, d_out, entering = v_ref[...], do_ref[...], st_ref[...]
        qm, km, qi, ks = (t[n].astype(BF16) for n in ("qm", "km", "qi", "ks"))
        dst = dstate[...]
        dst16 = dst.astype(BF16)
        p = jnp.where(causal, _dot(qm, km, NT), 0.0).astype(BF16)
        dp = jnp.where(causal, _dot(d_out, v, NT), 0.0).astype(BF16)
        dv_ref[...] = _dot(p, d_out, TN) + _dot(ks, dst16, NT)
        dqm, dkm = _dot(dp, km, NN), _dot(dp, qm, TN)
        dqi, dks = _dot(d_out, entering, NN), _dot(v, dst16, NN)
        d_decay = _colsum(dst * entering.astype(F32))
        dq_ref[...] = (dqi * t["e_qi"] + dqm * t["e_qm"]) * scale
        dk_ref[...] = dkm * t["e_km"] + dks * t["e_ks"]
        qm_term, km_term = dqm * t["qm"], dkm * t["km"]
        ks_term = dks * t["ks"]
        db = dqi * t["qi"] + qm_term - km_term - ks_term
        row = lax.broadcasted_iota(jnp.int32, db.shape, 0)
        db = db + jnp.where(row == t["mid"], _colsum(km_term - qm_term), 0.0)
        db = db + jnp.where(row == t["last"], _colsum(ks_term) + d_decay * t["decay"], 0.0)
        ri = lax.broadcasted_iota(jnp.int32, causal.shape, 0)
        ci = lax.broadcasted_iota(jnp.int32, causal.shape, 1)
        anti = (ci <= ri) if rev else (ci >= ri)
        dg_ref[...] = _tri_dot(anti.astype(BF16), db)
        dstate[...] = t["decay"] * dst + _dot(d_out, qi, TN)

    tokens = n_seq * seq
    kd = jax.ShapeDtypeStruct((tokens, key_dim), F32)
    return pl.pallas_call(
        body, name=name, grid=(n_seq, heads, n_chunks),
        in_specs=[q_spec, k_spec, v_spec, g_spec, o_spec, st_spec], out_specs=[g_spec, g_spec, o_spec, g_spec],
        out_shape=[kd, kd, jax.ShapeDtypeStruct((tokens, val_dim), F32), kd],
        scratch_shapes=[pltpu.VMEM((dv, dk), F32)], compiler_params=_params("parallel", "parallel", "arbitrary"),
    )(pm, pm, pm, log_a, d_o, states)


def _sigmoid_neg(z):
    e = jnp.exp(-jnp.abs(z))
    return jnp.where(z >= 0, e / (1.0 + e), 1.0 / (1.0 + e))


def _gate_fwd(name, pr, wf, bf, wb, bb, tr):
    kd = wf.shape[1]

    def fn(r, w_f, b_f, w_b, b_b):
        r16 = r.astype(BF16)
        out = []
        for w, b in ((w_f, b_f), (w_b, b_b)):
            z = _dot(r16, w.astype(BF16), NN) + b
            out.append((jnp.minimum(z, 0.0) - jnp.log1p(jnp.exp(-jnp.abs(z)))) / GATE_TAU)
        return tuple(out)

    return _rowwise(name, fn, n_rows=pr.shape[0], tr=tr, rows=[(pr, LANES, 0)], bcast=[wf, bf, wb, bb],
                    outs=[(kd, F32), (kd, F32)])


def _gate_bwd(name, pr, wf, bf, wb, bb, dgf, dgb, tr):
    kd = wf.shape[1]

    def fn(r, d_f, d_b, w_f, b_f, w_b, b_b):
        r16 = r.astype(BF16)
        dpr = jnp.zeros(r.shape, F32)
        sums = []
        for w, b, dg in ((w_f, b_f, d_f), (w_b, b_b, d_b)):
            w16 = w.astype(BF16)
            z = _dot(r16, w16, NN) + b
            dz = dg / GATE_TAU * _sigmoid_neg(z)
            dz16 = dz.astype(BF16)
            dpr = dpr + _dot(dz16, w16, NT)
            sums += [_dot(r16, dz16, TN), _colsum(dz)]
        return (dpr, *sums)

    return _rowwise(name, fn, n_rows=pr.shape[0], tr=tr, rows=[(pr, LANES, 0), (dgf, kd, 0), (dgb, kd, 0)],
                    bcast=[wf, bf, wb, bb], outs=[(LANES, BF16)], accs=[(LANES, kd), (1, kd), (LANES, kd), (1, kd)])


def _head_norm_fwd(name, of, ob, pm, gn, *, val_dim, gate_block, tr):
    dv = val_dim // GLA_HEADS

    def fn(o_f, o_b, gate, gain):
        out = []
        for h in range(GLA_HEADS):
            sl = slice(h * dv, (h + 1) * dv)
            o = o_f[:, sl] + o_b[:, sl]
            gt = gate[:, sl].astype(F32)
            on = o * lax.rsqrt(jnp.mean(o * o, axis=-1, keepdims=True) + EPS) * gain
            out.append(on * (gt * jax.nn.sigmoid(gt)))
        return (jnp.concatenate(out, axis=1),)

    return _rowwise(name, fn, n_rows=of.shape[0], tr=tr, rows=[(of, val_dim, 0), (ob, val_dim, 0), (pm, val_dim, gate_block)],
                    bcast=[gn], outs=[(val_dim, BF16)])[0]


def _head_norm_bwd(name, of, ob, pm, gn, dz, *, val_dim, gate_block, tr):
    dv = val_dim // GLA_HEADS

    def fn(o_f, o_b, gate, d_z, gain):
        d_o, d_gate = [], []
        d_gain = jnp.zeros((1, dv), F32)
        for h in range(GLA_HEADS):
            sl = slice(h * dv, (h + 1) * dv)
            o = o_f[:, sl] + o_b[:, sl]
            gt = gate[:, sl].astype(F32)
            dzh = d_z[:, sl]
            r = lax.rsqrt(jnp.mean(o * o, axis=-1, keepdims=True) + EPS)
            ohat = o * r
            sg = jax.nn.sigmoid(gt)
            d_gate.append(dzh * (ohat * gain) * (sg * (1.0 + gt * (1.0 - sg))))
            don = dzh * (gt * sg)
            d_gain = d_gain + _colsum(don * ohat)
            gd = gain * don
            d_o.append(r * (gd - ohat * jnp.mean(gd * ohat, axis=-1, keepdims=True)))
        return jnp.concatenate(d_o, axis=1), jnp.concatenate(d_gate, axis=1), d_gain

    return _rowwise(name, fn, n_rows=of.shape[0], tr=tr,
                    rows=[(of, val_dim, 0), (ob, val_dim, 0), (pm, val_dim, gate_block), (dz, val_dim, 0)], bcast=[gn],
                    outs=[(val_dim, BF16), (val_dim, BF16)], accs=[(1, dv)])


def _proj_grad(name, dqf, dqb, dkf, dkb, dvf, dvb, dgate, *, key_dim, val_dim, tr):
    def fn(q1, q2, k1, k2, v1, v2, gt):
        return (jnp.concatenate([(q1 + q2).astype(BF16), (k1 + k2).astype(BF16), (v1 + v2).astype(BF16), gt], axis=1),)

    rows = [(dqf, key_dim, 0), (dqb, key_dim, 0), (dkf, key_dim, 0), (dkb, key_dim, 0), (dvf, val_dim, 0),
            (dvb, val_dim, 0), (dgate, val_dim, 0)]
    return _rowwise(name, fn, n_rows=dqf.shape[0], tr=tr, rows=rows, outs=[(2 * key_dim + 2 * val_dim, BF16)])[0]


def _pool_out_grad(name, dh, y_pre, scale, tr):
    def fn(d, y, s):
        return d * s, _colsum(d * y)

    dm = dh.shape[1]
    return _rowwise(name, fn, n_rows=dh.shape[0], tr=tr, rows=[(dh, dm, 0), (y_pre, dm, 0)], bcast=[scale],
                    outs=[(dm, BF16)], accs=[(1, dm)])


def _mesh_place():
    x, y, c = lax.axis_index("x"), lax.axis_index("y"), lax.axis_index("c")
    return x, y, c, 4 * x + 2 * y + c


def _peer(x, y, c, p):
    px, py, pc = (x if not p & 4 else 1 - x), (y if not p & 2 else 1 - y), (c if not p & 1 else 1 - c)
    return (px, py, pc), 4 * px + 2 * py + pc


CHIP_BITS = (2, 4, 6)
N_CHIP = N_DEV // 2
GATHER_COPIES = 1 + 2 * len(CHIP_BITS)


def _transfer_sems(n_gather, n_scatter, n_pair):
    n_remote = n_gather * GATHER_COPIES + n_scatter * len(CHIP_BITS) + n_pair * N_CHIP
    return [pltpu.SemaphoreType.DMA((n_remote,)), pltpu.SemaphoreType.DMA((n_remote,)),
            pltpu.SemaphoreType.DMA((max(n_gather, 1),))]


def _transfers(g_in, g_out, s_in, s_out, p_in, p_out, send, recv, loc):
    x, y, c, me = _mesh_place()
    sibling, _ = _peer(x, y, c, 1)

    def remote(src, dst, k, peer):
        return pltpu.make_async_remote_copy(src_ref=src, dst_ref=dst, send_sem=send.at[k], recv_sem=recv.at[k],
                                            device_id=peer, device_id_type=MESH_ID)

    local, first, arrivals, forwards, forward_arrivals = [], [], [], [], []
    k = 0
    for src, out in zip(g_in, g_out):
        local.append(pltpu.make_async_copy(src, out.at[me], loc.at[len(local)]))
        for p in (1,) + CHIP_BITS:
            peer, block = _peer(x, y, c, p)
            first.append(remote(src, out.at[me], k, peer))
            arrivals.append(remote(src, out.at[block], k, peer))
            k += 1
        for p in CHIP_BITS:
            _, block = _peer(x, y, c, p)
            _, sibling_block = _peer(x, y, c, p ^ 1)
            forwards.append(remote(out.at[block], out.at[block], k, sibling))
            forward_arrivals.append(remote(out.at[sibling_block], out.at[sibling_block], k, sibling))
            k += 1
    for src, out in zip(s_in, s_out):
        for j, p in enumerate(CHIP_BITS):
            peer, _ = _peer(x, y, c, p)
            peer_chip = 2 * peer[0] + peer[1]
            first.append(remote(src.at[peer_chip], out.at[j], k, peer))
            arrivals.append(remote(src.at[peer_chip], out.at[j], k, peer))
            k += 1
    for src, out in zip(p_in, p_out):
        for q in range(N_CHIP):
            first.append(remote(src.at[2 * q + 1 - c], out.at[q], k, sibling))
            arrivals.append(remote(src.at[2 * q + 1 - c], out.at[q], k, sibling))
            k += 1

    def start():
        for cp in local + first:
            cp.start()

    def finish():
        for cp in arrivals:
            cp.wait_recv()
        for cp in forwards:
            cp.start()
        for cp in forward_arrivals:
            cp.wait_recv()
        for cp in first + forwards:
            cp.wait_send()
        for cp in local:
            cp.wait()

    return start, finish


def _exchange(name, gathers=(), scatters=(), pairs=()):
    n_g, n_s, n_p = len(gathers), len(scatters), len(pairs)
    n_in = n_g + n_s + n_p

    def body(*refs):
        ins, outs = refs[:n_in], refs[n_in:-3]
        start, finish = _transfers(ins[:n_g], outs[:n_g], ins[n_g:n_g + n_s], outs[n_g:n_g + n_s], ins[n_g + n_s:],
                                   outs[n_g + n_s:], *refs[-3:])
        start()
        finish()

    hbm = pl.BlockSpec(memory_space=pl.ANY)
    out_shape = [jax.ShapeDtypeStruct((N_DEV,) + g.shape, g.dtype) for g in gathers]
    out_shape += [jax.ShapeDtypeStruct((len(CHIP_BITS),) + t.shape[1:], t.dtype) for t in scatters]
    out_shape += [jax.ShapeDtypeStruct((N_CHIP,) + t.shape[1:], t.dtype) for t in pairs]
    return pl.pallas_call(
        body, name=name, in_specs=[hbm] * n_in, out_specs=[hbm] * len(out_shape), out_shape=out_shape,
        scratch_shapes=_transfer_sems(n_g, n_s, n_p),
    )(*gathers, *scatters, *pairs)


def _place_scalar(axis_value):
    return jnp.asarray(axis_value, jnp.int32).reshape(1)


def _chip_sums(name, grad):
    received = _exchange(name + "_pair", pairs=[grad])[0]
    cols = grad.shape[-1]
    rows = received.size // cols // N_CHIP
    tr = _row_tile(rows, (512 * 1024) // cols)

    def body(core_ref, mine_ref, got_ref, out_ref):
        out_ref[...] = (mine_ref[...].astype(F32) + got_ref[...].astype(F32)).astype(out_ref.dtype)

    block = (None, tr, cols)
    out = pl.pallas_call(
        body, name=name + "_add", out_shape=jax.ShapeDtypeStruct((N_CHIP, rows, cols), grad.dtype),
        grid_spec=pltpu.PrefetchScalarGridSpec(
            num_scalar_prefetch=1, grid=(N_CHIP, rows // tr),
            in_specs=[pl.BlockSpec(block, lambda q, i, core: (2 * q + core[0], i, 0)),
                      pl.BlockSpec(block, lambda q, i, core: (q, i, 0))],
            out_specs=pl.BlockSpec(block, lambda q, i, core: (q, i, 0))),
        compiler_params=_params("parallel", "parallel"),
    )(_place_scalar(lax.axis_index("c")), grad.reshape(N_DEV, rows, cols), received.reshape(N_CHIP, rows, cols))
    return out.reshape(received.shape)


def _all_reduce_small(name, pack):
    def body(in_ref, out_ref, parts, send_sems, recv_sems):
        x, y, c, me = _mesh_place()
        parts[me] = in_ref[...]

        def copy(p, arriving):
            peer, peer_block = _peer(x, y, c, p)
            return pltpu.make_async_remote_copy(src_ref=in_ref, dst_ref=parts.at[peer_block if arriving else me],
                                                send_sem=send_sems.at[p - 1], recv_sem=recv_sems.at[p - 1],
                                                device_id=peer, device_id_type=MESH_ID)

        sends = [copy(p, False) for p in range(1, N_DEV)]
        for cp in sends:
            cp.start()
        for p in range(1, N_DEV):
            copy(p, True).wait_recv()
        for cp in sends:
            cp.wait_send()
        total = parts[0]
        for j in range(1, N_DEV):
            total = total + parts[j]
        out_ref[...] = total

    vmem = pl.BlockSpec(memory_space=pltpu.VMEM)
    return pl.pallas_call(
        body, name=name, in_specs=[vmem], out_specs=vmem, out_shape=jax.ShapeDtypeStruct(pack.shape, F32),
        scratch_shapes=[pltpu.VMEM((N_DEV,) + pack.shape, F32), pltpu.SemaphoreType.DMA((N_PEER,)),
                        pltpu.SemaphoreType.DMA((N_PEER,))],
        compiler_params=pltpu.CompilerParams(vmem_limit_bytes=VMEM_LIMIT_BYTES),
    )(pack)


def _adamw_math(w, g, m, v):
    m = ADAM_B1 * m + (1.0 - ADAM_B1) * g
    v = ADAM_B2 * v + (1.0 - ADAM_B2) * jnp.square(g)
    m_hat = m / (1.0 - ADAM_B1 ** ADAM_STEP)
    v_hat = v / (1.0 - ADAM_B2 ** ADAM_STEP)
    delta = -ADAM_LR * (m_hat / (jnp.sqrt(v_hat) + ADAM_EPS) + ADAM_WD * w)
    return delta, m, v


def _adamw_shard(name, w, m, v, grad_parts, layer=0, into=None):
    landed, sums = grad_parts
    shape = w.shape
    n_layers, cols = shape[0], shape[-1]
    rows = w.size // cols // n_layers
    tr = _row_tile(rows, (256 * 1024) // cols)
    n_into = 0 if into is None else 4
    n_landed = landed.shape[0]

    def body(chip_ref, w_ref, m_ref, v_ref, mine_ref, got_ref, *refs):
        g_ref, d_ref, nm_ref, nv_ref = refs[n_into:]
        g = mine_ref[...].astype(F32)
        for j in range(n_landed):
            g = g + got_ref[j].astype(F32)
        delta, new_m, new_v = _adamw_math(w_ref[...], g, m_ref[...], v_ref[...])
        g_ref[...], d_ref[...], nm_ref[...], nv_ref[...] = g, delta, new_m, new_v

    view = (n_layers, rows, cols)
    spec = pl.BlockSpec((None, tr, cols), lambda i, chip: (layer, i, 0))
    operands = [w.reshape(view), m.reshape(view), v.reshape(view), sums.reshape(N_CHIP, rows, cols),
                landed.reshape(n_landed, rows, cols)]
    in_specs = [spec, spec, spec, pl.BlockSpec((None, tr, cols), lambda i, chip: (chip[0], i, 0)),
                pl.BlockSpec((n_landed, tr, cols), lambda i, chip: (0, i, 0))]
    if into is not None:
        operands += [t.reshape(view) for t in into]
        in_specs += [pl.BlockSpec(memory_space=pl.ANY)] * 4
    first_into = 1 + 5
    res = pl.pallas_call(
        body, name=name, out_shape=[jax.ShapeDtypeStruct(view, F32)] * 4,
        grid_spec=pltpu.PrefetchScalarGridSpec(num_scalar_prefetch=1, grid=(rows // tr,), in_specs=in_specs, out_specs=[spec] * 4),
        input_output_aliases={first_into + t: t for t in range(n_into)}, compiler_params=_params("parallel"),
    )(_place_scalar(2 * lax.axis_index("x") + lax.axis_index("y")), *operands)
    return [r.reshape(shape) for r in res]


def _adamw_small(name, w, m, v, g):
    shape = w.shape
    two_d = (1, shape[0]) if len(shape) == 1 else (w.size // shape[-1], shape[-1])

    def body(w_ref, m_ref, v_ref, g_ref, d_ref, nm_ref, nv_ref):
        d_ref[...], nm_ref[...], nv_ref[...] = _adamw_math(w_ref[...], g_ref[...], m_ref[...], v_ref[...])

    res = pl.pallas_call(body, name=name, out_shape=[jax.ShapeDtypeStruct(two_d, F32)] * 3)(
        w.reshape(two_d), m.reshape(two_d), v.reshape(two_d), g.reshape(two_d))
    return [r.reshape(shape) for r in res]


def _mlp_fwd(tag, h_in, gain, w1, w2_mine, down_gathers, tm, tr):
    tokens, d = h_in.shape
    fs = w1.shape[2]
    f = N_DEV * fs
    hn = _rms_fwd(f"rms_mlp{tag}", h_in, gain, BF16, tr)
    tn, tk = _tile(fs, 1024), _tile(d, TILE_K)
    per = fs // tn

    def relu2(acc):
        a = jnp.maximum(acc, 0.0)
        return acc, a * a

    a_pre, act, w2 = _mm(
        f"mlp_up{tag}", hn, w1, grid=(tokens // tm, f // tn, d // tk), a_spec=_spec2(tm, tk, lambda m, n, k: (m, k)),
        b_spec=pl.BlockSpec((None, tk, tn), lambda m, n, k: (n // per, k, n % per)),
        o_spec=_spec2(tm, tn, lambda m, n, k: (m, n)), acc_shape=(tm, tn), outs=[((tokens, f), BF16)] * 2, dims=NN, epi=relu2,
        gathers=[w2_mine])
    tn2, tk2 = _tile(d, 1024), _tile(fs, TILE_K)
    kper = fs // tk2
    h_out, *got = _mm(
        f"mlp_down{tag}", act, w2, grid=(tokens // tm, d // tn2, f // tk2), a_spec=_spec2(tm, tk2, lambda m, n, k: (m, k)),
        b_spec=pl.BlockSpec((None, tk2, tn2), lambda m, n, k: (k // kper, k % kper, n)),
        o_spec=_spec2(tm, tn2, lambda m, n, k: (m, n)), acc_shape=(tm, tn2), outs=[((tokens, d), F32)], dims=NN,
        extras=[(h_in, _spec2(tm, tn2, lambda m, n, k: (m, n)))], epi=lambda acc, res: (res + acc,), gathers=down_gathers)
    return hn, a_pre, act, h_out, w2, got


def _mlp_bwd(tag, h_in, gain, hn, a_pre, act, dh, dh16, w1, w2, tm, tr):
    tokens, d = h_in.shape
    fs = w1.shape[2]
    f = N_DEV * fs
    tn, tk = _tile(fs, 1024), _tile(d, TILE_K)
    nper = fs // tn
    da = _mm(
        f"mlp_dact{tag}", dh16, w2, grid=(tokens // tm, f // tn, d // tk), a_spec=_spec2(tm, tk, lambda m, n, k: (m, k)),
        b_spec=pl.BlockSpec((None, tn, tk), lambda m, n, k: (n // nper, n % nper, k)),
        o_spec=_spec2(tm, tn, lambda m, n, k: (m, n)), acc_shape=(tm, tn), outs=[((tokens, f), BF16)], dims=NT,
        extras=[(a_pre, _spec2(tm, tn, lambda m, n, k: (m, n)))],
        epi=lambda acc, pre: (acc * (2.0 * jnp.maximum(pre.astype(F32), 0.0)),))[0]
    tkt = _tile(tokens, TILE_K)
    tm2, tn2 = _tile(fs, 1024), _tile(d, 1024)
    mper = fs // tm2
    dw2 = _mm(
        f"mlp_dw_out{tag}", act, dh16, grid=(f // tm2, d // tn2, tokens // tkt), a_spec=_spec2(tkt, tm2, lambda m, n, k: (k, m)),
        b_spec=_spec2(tkt, tn2, lambda m, n, k: (k, n)),
        o_spec=pl.BlockSpec((None, tm2, tn2), lambda m, n, k: (m // mper, m % mper, n)), acc_shape=(tm2, tn2),
        outs=[((N_DEV, fs, d), BF16)], dims=TN)[0]
    dw2_sums = _chip_sums(f"mlp_dw_out{tag}", dw2)
    tm3 = _tile(d, 1024)
    dw1, dw2_landed = _mm(
        f"mlp_dw_in{tag}", hn, da, grid=(d // tm3, f // tn, tokens // tkt), a_spec=_spec2(tkt, tm3, lambda m, n, k: (k, m)),
        b_spec=_spec2(tkt, tn, lambda m, n, k: (k, n)),
        o_spec=pl.BlockSpec((None, tm3, tn), lambda m, n, k: (n // nper, m, n % nper)), acc_shape=(tm3, tn),
        outs=[((N_DEV, d, fs), BF16)], dims=TN, scatters=[dw2_sums])
    dw1_sums = _chip_sums(f"mlp_dw_in{tag}", dw1)
    tn4, tk4 = _tile(d, 1024), _tile(fs, TILE_K)
    kper = fs // tk4
    dhn, dw1_landed = _mm(
        f"mlp_dx{tag}", da, w1, grid=(tokens // tm, d // tn4, f // tk4), a_spec=_spec2(tm, tk4, lambda m, n, k: (m, k)),
        b_spec=pl.BlockSpec((None, tn4, tk4), lambda m, n, k: (k // kper, n, k % kper)),
        o_spec=_spec2(tm, tn4, lambda m, n, k: (m, n)), acc_shape=(tm, tn4), outs=[((tokens, d), F32)], dims=NT,
        scatters=[dw1_sums])
    dh_in, dh_in16, dgain = _rms_bwd(f"rms_mlp_bwd{tag}", h_in, gain, dhn, dh, tr)
    return dh_in, dh_in16, dgain, (dw1_landed, dw1_sums), (dw2_landed, dw2_sums)


def _plain_mm(name, a, b, dims, out_dtype, *, tm, tn_pref=1024, tk_pref=TILE_K, extras=None, epi=None, gathers=(), scatters=()):
    if dims == NN:
        (m, k), n = a.shape, b.shape[1]
    elif dims == NT:
        (m, k), n = a.shape, b.shape[0]
    else:
        (k, m), n = a.shape, b.shape[1]
    tm, tn, tk = _tile(m, tm), _tile(n, tn_pref), _tile(k, tk_pref)
    a_spec = _spec2(tk, tm, lambda i, j, l: (l, i)) if dims == TN else _spec2(tm, tk, lambda i, j, l: (i, l))
    b_spec = _spec2(tn, tk, lambda i, j, l: (j, l)) if dims == NT else _spec2(tk, tn, lambda i, j, l: (l, j))
    o_spec = _spec2(tm, tn, lambda i, j, l: (i, j))
    ex = [(e, o_spec) for e in (extras or [])]
    res = _mm(name, a, b, grid=(m // tm, n // tn, k // tk), a_spec=a_spec, b_spec=b_spec, o_spec=o_spec, acc_shape=(tm, tn),
              outs=[((m, n), out_dtype)], dims=dims, extras=ex, epi=epi, gathers=gathers, scatters=scatters)
    return res if (gathers or scatters) else res[0]


def kernel(x, norm_mix, norm_mlp, norm_final, pool_w, pool_scale, gla_w_in, gla_w_up_f, gla_b_up_f, gla_w_up_b, gla_b_up_b, gla_g_norm, gla_w_out, mlp_w_in, mlp_w_out, loss_target, m_norm_mix, m_norm_mlp, m_norm_final, m_pool_w, m_pool_scale, m_gla_w_in, m_gla_w_up_f, m_gla_b_up_f, m_gla_w_up_b, m_gla_b_up_b, m_gla_g_norm, m_gla_w_out, m_mlp_w_in, m_mlp_w_out, v_norm_mix, v_norm_mlp, v_norm_final, v_pool_w, v_pool_scale, v_gla_w_in, v_gla_w_up_f, v_gla_b_up_f, v_gla_w_up_b, v_gla_b_up_b, v_gla_g_norm, v_gla_w_out, v_mlp_w_in, v_mlp_w_out):
    n_seq, seq, d = x.shape
    tokens = n_seq * seq
    group = d // N_POOL_GROUPS
    key_dim, val_dim = d // 2, d
    dv = val_dim // GLA_HEADS
    n_main = 2 * key_dim + 2 * val_dim
    proj_shard = gla_w_in.shape[2]
    proj_width = N_DEV * proj_shard
    kd_shard = key_dim // N_DEV
    assert proj_width == n_main + 2 * GATE_RANK and seq % CHUNK == 0 and 2 * GATE_RANK <= LANES
    assert mlp_w_in.shape[0] == 2 and pool_w.shape[0] == 1 and gla_w_in.shape[0] == 1
    me = 4 * lax.axis_index("x") + 2 * lax.axis_index("y") + lax.axis_index("c")
    tm = _tile(tokens, 1024)
    tr = _tile(tokens, 128)
    x2, target = x.reshape(tokens, d), loss_target.reshape(tokens, d)

    small_rows = 2 * GATE_RANK + 3
    small = jnp.concatenate([
        gla_w_up_f[0], gla_w_up_b[0], gla_b_up_f, gla_b_up_b,
        jnp.pad(gla_g_norm, ((0, 0), (0, kd_shard - gla_g_norm.shape[1])))], axis=0)
    w1_mine = [mlp_w_in[l].astype(BF16) for l in range(2)]
    w2_mine = [mlp_w_out[l].astype(BF16) for l in range(2)]
    pool_g, small_g, w1_0 = _exchange("gather_first", gathers=[pool_w[0].astype(BF16), small, w1_mine[0]])
    w_pool = pool_g.transpose(1, 0, 2, 3).reshape(N_POOL_GROUPS, group, group)
    small_full = small_g.transpose(1, 0, 2).reshape(small_rows, key_dim)
    pad_rows = lambda w, r0: jnp.pad(w, ((r0, LANES - r0 - GATE_RANK), (0, 0)))
    w_up_f, w_up_b = pad_rows(small_full[:GATE_RANK], 0), pad_rows(small_full[GATE_RANK:2 * GATE_RANK], GATE_RANK)
    b_up_f, b_up_b = small_full[2 * GATE_RANK:2 * GATE_RANK + 1], small_full[2 * GATE_RANK + 1:2 * GATE_RANK + 2]
    g_norm = small_g[:, 2 * GATE_RANK + 2, :gla_g_norm.shape[1]].reshape(1, dv)

    hn0 = _rms_fwd("rms_mix0", x2, norm_mix[0:1], F32, tr)
    pooled = _pool_slabs("pool_fwd", hn0, BF16, n_seq=n_seq, seq=seq, backward=False)
    tn, tk = _tile(group, 1024), _tile(group, TILE_K)
    nper, kper = group // tn, group // tk
    y_pre, h1 = _mm(
        "pool_mix", pooled, w_pool, grid=(tokens // tm, d // tn, group // tk),
        a_spec=_spec2(tm, tk, lambda m, n, k: (m, (n // nper) * kper + k)),
        b_spec=pl.BlockSpec((None, tk, tn), lambda m, n, k: (n // nper, k, n % nper)),
        o_spec=_spec2(tm, tn, lambda m, n, k: (m, n)), acc_shape=(tm, tn), outs=[((tokens, d), F32)] * 2, dims=NN,
        extras=[(x2, _spec2(tm, tn, lambda m, n, k: (m, n))), (pool_scale, _spec2(1, tn, lambda m, n, k: (0, n)))],
        epi=lambda acc, res, s: (acc, res + acc * s))
    hn1, a_pre0, act0, h2, w2_0, (win_g, wout_g) = _mlp_fwd(
        "0", h1, norm_mlp[0:1], w1_0, w2_mine[0], [gla_w_in[0].astype(BF16), gla_w_out[0].astype(BF16)], tm, tr)
    w_in = win_g.transpose(1, 0, 2).reshape(d, proj_width)
    w_main = w_in[:, :n_main]
    w_r = jnp.pad(w_in[:, n_main:], ((0, 0), (0, LANES - 2 * GATE_RANK)))
    w_out = wout_g.reshape(val_dim, d)

    hn2 = _rms_fwd("rms_mix1", h2, norm_mix[1:2], BF16, tr)
    pm, w1_1 = _plain_mm("gla_proj", hn2, w_main, NN, BF16, tm=tm, gathers=[w1_mine[1]])
    pr = _plain_mm("gla_proj_gate", hn2, w_r, NN, F32, tm=tm)
    log_a_f, log_a_b = _gate_fwd("gla_gate", pr, w_up_f, b_up_f, w_up_b, b_up_b, tr)
    dims = dict(n_seq=n_seq, seq=seq, key_dim=key_dim, val_dim=val_dim)
    o_f, st_f = _gla_fwd("gla_scan_f", pm, log_a_f, rev=False, **dims)
    o_b, st_b = _gla_fwd("gla_scan_b", pm, log_a_b, rev=True, **dims)
    gate_block = (2 * key_dim + val_dim) // val_dim
    z = _head_norm_fwd("gla_head_norm", o_f, o_b, pm, g_norm, val_dim=val_dim, gate_block=gate_block, tr=tr)
    h3 = _plain_mm("gla_out", z, w_out, NN, F32, tm=tm, extras=[h2], epi=lambda acc, res: (res + acc,))
    hn3, a_pre1, act1, h4, w2_1, _ = _mlp_fwd("1", h3, norm_mlp[1:2], w1_1, w2_mine[1], [], tm, tr)

    dh4, dh4_16, loss_row, d_norm_final = _loss_head("loss_head", h4, target, norm_final.reshape(1, d), tr)

    dh3, dh3_16, d_norm_mlp1, dw1_1, dw2_1 = _mlp_bwd("1", h3, norm_mlp[1:2], hn3, a_pre1, act1, dh4, dh4_16, w1_1, w2_1, tm, tr)
    dz = _plain_mm("gla_out_dx", dh3_16, w_out, NT, F32, tm=tm)
    dw_out = _plain_mm("gla_out_dw", z, dh3_16, TN, BF16, tm=1024)
    dw_out_sums = _chip_sums("gla_out_dw", dw_out.reshape(N_DEV, val_dim // N_DEV, d))
    d_o, d_gate, d_g_norm = _head_norm_bwd("gla_head_norm_bwd", o_f, o_b, pm, g_norm, dz, val_dim=val_dim,
                                           gate_block=gate_block, tr=tr)
    dq_f, dk_f, dv_f, dla_f = _gla_bwd("gla_scan_bwd_f", pm, log_a_f, st_f, d_o, rev=False, **dims)
    dq_b, dk_b, dv_b, dla_b = _gla_bwd("gla_scan_bwd_b", pm, log_a_b, st_b, d_o, rev=True, **dims)
    dpr, dw_up_f, db_up_f, dw_up_b, db_up_b = _gate_bwd("gla_gate_bwd", pr, w_up_f, b_up_f, w_up_b, b_up_b, dla_f, dla_b, tr)
    dpm = _proj_grad("gla_proj_grad", dq_f, dq_b, dk_f, dk_b, dv_f, dv_b, d_gate, key_dim=key_dim, val_dim=val_dim, tr=tr)
    dw_main, dw_out_landed = _plain_mm("gla_proj_dw", hn2, dpm, TN, BF16, tm=1024, scatters=[dw_out_sums])
    dw_r = _plain_mm("gla_proj_gate_dw", hn2, dpr, TN, BF16, tm=1024)
    dw_in = jnp.concatenate([dw_main, dw_r[:, :2 * GATE_RANK]], axis=1)
    dw_in_sums = _chip_sums("gla_proj_dw", dw_in.reshape(d, N_DEV, proj_shard).transpose(1, 0, 2))
    dhn2, dw_in_landed = _plain_mm("gla_proj_dx", dpm, w_main, NT, F32, tm=tm, scatters=[dw_in_sums])
    dhn2 = _plain_mm("gla_proj_gate_dx", dpr, w_r, NT, F32, tm=tm, extras=[dhn2], epi=lambda acc, res: (res + acc,))
    dh2, dh2_16, d_norm_mix1 = _rms_bwd("rms_mix1_bwd", h2, norm_mix[1:2], dhn2, dh3, tr)

    dh1, _, d_norm_mlp0, dw1_0, dw2_0 = _mlp_bwd("0", h1, norm_mlp[0:1], hn1, a_pre0, act0, dh2, dh2_16, w1_0, w2_0, tm, tr)
    dyp, d_pool_scale = _pool_out_grad("pool_out_grad", dh1, y_pre, pool_scale, tr)
    dpooled = _mm(
        "pool_mix_dx", dyp, w_pool, grid=(tokens // tm, d // tn, group // tk),
        a_spec=_spec2(tm, tk, lambda m, n, k: (m, (n // nper) * kper + k)),
        b_spec=pl.BlockSpec((None, tn, tk), lambda m, n, k: (n // nper, n % nper, k)),
        o_spec=_spec2(tm, tn, lambda m, n, k: (m, n)), acc_shape=(tm, tn), outs=[((tokens, d), F32)], dims=NT)[0]
    tmw, tkt = _tile(group, 1024), _tile(tokens, TILE_K)
    mper = group // tmw
    dw_pool = _mm(
        "pool_mix_dw", pooled, dyp, grid=(d // tmw, group // tn, tokens // tkt),
        a_spec=_spec2(tkt, tmw, lambda m, n, k: (k, m)), b_spec=_spec2(tkt, tn, lambda m, n, k: (k, (m // mper) * nper + n)),
        o_spec=pl.BlockSpec((None, tmw, tn), lambda m, n, k: (m // mper, m % mper, n)), acc_shape=(tmw, tn),
        outs=[((N_POOL_GROUPS, group, group), BF16)], dims=TN)[0]
    dw_pool_sums = _chip_sums("pool_mix_dw", dw_pool.reshape(N_POOL_GROUPS, N_DEV, group // N_DEV, group).transpose(1, 0, 2, 3))
    dw_pool_landed = _exchange("scatter_pool", scatters=[dw_pool_sums])[0]
    dhn0 = _pool_slabs("pool_bwd", dpooled, F32, n_seq=n_seq, seq=seq, backward=True)
    dx, _, d_norm_mix0 = _rms_bwd("rms_mix0_bwd", x2, norm_mix[0:1], dhn0, dh1, tr)

    pack_cols = key_dim
    rows_of = lambda a: a.reshape(-1, pack_cols)
    pieces = [rows_of(t) for t in (d_norm_mix0, d_norm_mix1, d_norm_mlp0, d_norm_mlp1, d_norm_final, d_pool_scale,
                                   dw_up_f[:GATE_RANK], dw_up_b[GATE_RANK:2 * GATE_RANK], db_up_f, db_up_b)]
    pieces += [jnp.pad(d_g_norm, ((0, 0), (0, pack_cols - dv))), jnp.pad(loss_row, ((0, 0), (0, pack_cols - LANES)))]
    n_rows = sum(p.shape[0] for p in pieces)
    pack = jnp.concatenate(pieces + [jnp.zeros((-n_rows % 8, pack_cols), F32)], axis=0)
    red = _all_reduce_small("reduce_small", pack)
    cuts, r0 = [], 0
    for p in pieces:
        cuts.append(red[r0:r0 + p.shape[0]])
        r0 += p.shape[0]
    g_norm_mix = jnp.concatenate([cuts[0].reshape(1, d), cuts[1].reshape(1, d)], axis=0)
    g_norm_mlp = jnp.concatenate([cuts[2].reshape(1, d), cuts[3].reshape(1, d)], axis=0)
    g_norm_final = cuts[4].reshape(d)
    g_pool_scale = cuts[5].reshape(1, d)
    shard_cols = lambda t, width: lax.dynamic_slice_in_dim(t, me * width, width, axis=1)
    g_w_up_f, g_w_up_b = shard_cols(cuts[6], kd_shard)[None], shard_cols(cuts[7], kd_shard)[None]
    g_b_up_f, g_b_up_b = shard_cols(cuts[8], kd_shard), shard_cols(cuts[9], kd_shard)
    g_g_norm = shard_cols(cuts[10][:, :dv], dv // N_DEV)
    loss = cuts[11][0, 0]

    big = {
        "pool_w": _adamw_shard("adamw_pool_w", pool_w, m_pool_w, v_pool_w, (dw_pool_landed, dw_pool_sums)),
        "gla_w_in": _adamw_shard("adamw_gla_w_in", gla_w_in, m_gla_w_in, v_gla_w_in, (dw_in_landed, dw_in_sums)),
        "gla_w_out": _adamw_shard("adamw_gla_w_out", gla_w_out, m_gla_w_out, v_gla_w_out, (dw_out_landed, dw_out_sums)),
    }
    for nm, w, m, v, landed in (("mlp_w_in", mlp_w_in, m_mlp_w_in, v_mlp_w_in, (dw1_0, dw1_1)),
                                ("mlp_w_out", mlp_w_out, m_mlp_w_out, v_mlp_w_out, (dw2_0, dw2_1))):
        layer1 = _adamw_shard(f"adamw_{nm}1", w, m, v, landed[1], layer=1)
        big[nm] = _adamw_shard(f"adamw_{nm}0", w, m, v, landed[0], layer=0, into=layer1)
    grads = {"norm_mix": g_norm_mix, "norm_mlp": g_norm_mlp, "norm_final": g_norm_final, "pool_scale": g_pool_scale,
             "gla_w_up_f": g_w_up_f, "gla_b_up_f": g_b_up_f, "gla_w_up_b": g_w_up_b, "gla_b_up_b": g_b_up_b, "gla_g_norm": g_g_norm}
    state = {"norm_mix": (norm_mix, m_norm_mix, v_norm_mix), "norm_mlp": (norm_mlp, m_norm_mlp, v_norm_mlp),
             "norm_final": (norm_final, m_norm_final, v_norm_final), "pool_scale": (pool_scale, m_pool_scale, v_pool_scale),
             "gla_w_up_f": (gla_w_up_f, m_gla_w_up_f, v_gla_w_up_f), "gla_b_up_f": (gla_b_up_f, m_gla_b_up_f, v_gla_b_up_f),
             "gla_w_up_b": (gla_w_up_b, m_gla_w_up_b, v_gla_w_up_b), "gla_b_up_b": (gla_b_up_b, m_gla_b_up_b, v_gla_b_up_b),
             "gla_g_norm": (gla_g_norm, m_gla_g_norm, v_gla_g_norm)}
    order = ["norm_mix", "norm_mlp", "norm_final", "pool_w", "pool_scale", "gla_w_in", "gla_w_up_f", "gla_b_up_f", "gla_w_up_b",
             "gla_b_up_b", "gla_g_norm", "gla_w_out", "mlp_w_in", "mlp_w_out"]
    out_g, out_d, out_m, out_v = [], [], [], []
    for nm in order:
        if nm in big:
            g, dl, new_m, new_v = big[nm]
        else:
            g = grads[nm]
            dl, new_m, new_v = _adamw_small(f"adamw_{nm}", *state[nm], g)
        out_g.append(g), out_d.append(dl), out_m.append(new_m), out_v.append(new_v)
    return (loss, dx.reshape(x.shape), *out_g, *out_d, *out_m, *out_v)
```

```python
import functools

import jax
import jax.numpy as jnp
from jax import lax
from jax.experimental import pallas as pl
from jax.experimental.pallas import tpu as pltpu

F32 = jnp.float32
BF16 = jnp.bfloat16
N_DEV = 8
N_PEER = N_DEV - 1
LANES = 128
VMEM_LIMIT_BYTES = 48 * 1024 * 1024
MESH_ID = pl.DeviceIdType.MESH
TILE_K = 1024

POOL_WINDOWS = (2, 4, 8, 16)
N_POOL_GROUPS = len(POOL_WINDOWS)
GLA_HEADS = 4
GATE_RANK = 16
GATE_TAU = 16.0
CHUNK = 64
EPS = 1e-6
ADAM_LR = 0.001
ADAM_B1 = 0.9
ADAM_B2 = 0.999
ADAM_EPS = 1e-08
ADAM_WD = 0.01
ADAM_STEP = 10

NN = ((1,), (0,))
NT = ((1,), (1,))
TN = ((0,), (0,))


def _tile(dim, pref):
    if dim <= pref:
        return dim
    t = (pref // LANES) * LANES
    while t > LANES and dim % t:
        t -= LANES
    assert dim % t == 0, (dim, pref)
    return t


def _row_tile(rows, pref):
    if rows <= pref:
        return rows
    t = max(8, pref // 8 * 8)
    while t > 8 and rows % t:
        t -= 8
    assert rows % t == 0, (rows, pref)
    return t


def _params(*sem):
    return pltpu.CompilerParams(dimension_semantics=sem, vmem_limit_bytes=VMEM_LIMIT_BYTES)


def _dot(a, b, dims):
    return lax.dot_general(a, b, (dims, ((), ())), preferred_element_type=F32)


def _mm(name, a, b, *, grid, a_spec, b_spec, o_spec, acc_shape, outs, dims, extras=(), epi=None, gathers=(), scatters=()):
    n_ex, n_out, nk = len(extras), len(outs), grid[2]
    n_g, n_s = len(gathers), len(scatters)
    carries = bool(n_g or n_s)

    def body(*refs):
        a_ref, b_ref = refs[0], refs[1]
        ex = refs[2:2 + n_ex]
        r0 = 2 + n_ex
        g_in, s_in = refs[r0:r0 + n_g], refs[r0 + n_g:r0 + n_g + n_s]
        r0 += n_g + n_s
        o_refs = refs[r0:r0 + n_out]
        g_out, s_out = refs[r0 + n_out:r0 + n_out + n_g], refs[r0 + n_out + n_g:r0 + n_out + n_g + n_s]
        acc = refs[r0 + n_out + n_g + n_s]
        i, j, k = pl.program_id(0), pl.program_id(1), pl.program_id(2)
        if carries:
            start, middle, finish = _transfers(g_in, g_out, s_in, s_out, (), (), *refs[-3:])
            step = (i * grid[1] + j) * nk + k

            @pl.when(step == 0)
            def _():
                start()

            if n_g:
                @pl.when(step == (grid[0] * grid[1] * nk) // 2)
                def _():
                    middle()

        def part():
            return _dot(a_ref[...].astype(BF16), b_ref[...].astype(BF16), dims)

        def finish_tile(total):
            res = epi(total, *[e[...] for e in ex]) if epi is not None else (total,)
            for o, r in zip(o_refs, res):
                o[...] = r.astype(o.dtype)

        if nk == 1:
            finish_tile(part())
        else:
            @pl.when(k == 0)
            def _():
                acc[...] = part()

            @pl.when((k > 0) & (k < nk - 1))
            def _():
                acc[...] += part()

            @pl.when(k == nk - 1)
            def _():
                finish_tile(acc[...] + part())

        if carries:
            @pl.when((i == grid[0] - 1) & (j == grid[1] - 1) & (k == nk - 1))
            def _():
                finish()

    hbm = pl.BlockSpec(memory_space=pl.ANY)
    operands = [a, b] + [e for e, _ in extras] + list(gathers) + list(scatters)
    in_specs = [a_spec, b_spec] + [s for _, s in extras] + [hbm] * (n_g + n_s)
    out_shape = [jax.ShapeDtypeStruct(s, d) for s, d in outs]
    out_shape += [jax.ShapeDtypeStruct((N_DEV,) + g.shape, g.dtype) for g in gathers]
    out_shape += [jax.ShapeDtypeStruct((len(CHIP_BITS),) + t.shape[1:], t.dtype) for t in scatters]
    scratch = [pltpu.VMEM(acc_shape, F32)] + (_transfer_sems(n_g, n_s, 0) if carries else [])
    sem = ("arbitrary",) * 3 if carries else ("parallel", "parallel", "arbitrary")
    return pl.pallas_call(
        body, name=name, grid=grid, in_specs=in_specs, out_specs=[o_spec] * n_out + [hbm] * (n_g + n_s),
        out_shape=out_shape, scratch_shapes=scratch, compiler_params=_params(*sem),
    )(*operands)


def _spec2(rows, cols, fn):
    return pl.BlockSpec((rows, cols), fn)


def _rowwise(name, fn, *, n_rows, tr, rows, bcast=(), outs=(), accs=()):
    n_r, n_b, n_o, n_a = len(rows), len(bcast), len(outs), len(accs)

    def body(*refs):
        r_refs, b_refs = refs[:n_r], refs[n_r:n_r + n_b]
        o_refs, a_refs = refs[n_r + n_b:n_r + n_b + n_o], refs[n_r + n_b + n_o:]
        res = fn(*[r[...] for r in r_refs], *[b[...] for b in b_refs])
        for o, r in zip(o_refs, res[:n_o]):
            o[...] = r.astype(o.dtype)
        if n_a:
            @pl.when(pl.program_id(0) == 0)
            def _():
                for acc in a_refs:
                    acc[...] = jnp.zeros_like(acc)

            for acc, r in zip(a_refs, res[n_o:]):
                acc[...] += r

    in_specs = [pl.BlockSpec((tr, w), functools.partial(lambda i, cb: (i, cb), cb=cb)) for _, w, cb in rows]
    in_specs += [pl.BlockSpec(b.shape, lambda i: (0, 0)) for b in bcast]
    out_specs = [pl.BlockSpec((tr, w), lambda i: (i, 0)) for w, _ in outs]
    out_specs += [pl.BlockSpec(s, lambda i: (0, 0)) for s in accs]
    out_shape = [jax.ShapeDtypeStruct((n_rows, w), d) for w, d in outs]
    out_shape += [jax.ShapeDtypeStruct(s, F32) for s in accs]
    return pl.pallas_call(
        body, name=name, grid=(n_rows // tr,), in_specs=in_specs, out_specs=out_specs, out_shape=out_shape,
        compiler_params=_params("arbitrary"),
    )(*[r for r, _, _ in rows], *bcast)


def _colsum(t):
    return jnp.sum(t, axis=0, keepdims=True)


def _rms_fwd(name, h, g, dtype, tr):
    def fn(x, gain):
        r = lax.rsqrt(jnp.mean(x * x, axis=-1, keepdims=True) + EPS)
        return (x * r * gain,)

    d = h.shape[1]
    return _rowwise(name, fn, n_rows=h.shape[0], tr=tr, rows=[(h, d, 0)], bcast=[g], outs=[(d, dtype)])[0]


def _rms_bwd(name, h, g, dhn, dres, tr):
    def fn(x, dy, dr, gain):
        r = lax.rsqrt(jnp.mean(x * x, axis=-1, keepdims=True) + EPS)
        xhat = x * r
        gd = gain * dy
        dh = r * (gd - xhat * jnp.mean(gd * xhat, axis=-1, keepdims=True)) + dr
        return dh, dh, _colsum(dy * xhat)

    d = h.shape[1]
    return _rowwise(name, fn, n_rows=h.shape[0], tr=tr, rows=[(h, d, 0), (dhn, d, 0), (dres, d, 0)], bcast=[g],
                    outs=[(d, F32), (d, BF16)], accs=[(1, d)])


def _loss_head(name, h, target, g, tr):
    d = h.shape[1]

    def fn(x, tgt, gain):
        r = lax.rsqrt(jnp.mean(x * x, axis=-1, keepdims=True) + EPS)
        xhat = x * r
        err = xhat * gain - tgt
        loss = 0.5 * jnp.sum(jnp.mean(err * err, axis=-1, keepdims=True), axis=0, keepdims=True)
        dy = err / d
        gd = gain * dy
        dh = r * (gd - xhat * jnp.mean(gd * xhat, axis=-1, keepdims=True))
        return dh, dh, jnp.broadcast_to(loss, (1, LANES)), _colsum(dy * xhat)

    return _rowwise(name, fn, n_rows=h.shape[0], tr=tr, rows=[(h, d, 0), (target, d, 0)], bcast=[g],
                    outs=[(d, F32), (d, BF16)], accs=[(1, LANES), (1, d)])


def _shift_rows(h, k, seq):
    if k == 0:
        return h
    rolled = pltpu.roll(h, (-k) % seq, 0)
    t = lax.broadcasted_iota(jnp.int32, h.shape, 0)
    return jnp.where((t + k >= 0) & (t + k < seq), rolled, 0.0)


def _window_halves(h, half, seq):
    right = h
    left = _shift_rows(h, -1, seq)
    step = 1
    while step < half:
        right = right + _shift_rows(right, step, seq)
        left = left + _shift_rows(left, -step, seq)
        step *= 2
    return left + right


def _window_count(shape, half, seq):
    t = lax.broadcasted_iota(jnp.int32, shape, 0)
    return (jnp.minimum(t + half, seq) - jnp.maximum(t - half, 0)).astype(F32)


def _pool_slabs(name, src, out_dtype, *, n_seq, seq, backward):
    d = src.shape[1]
    group = d // N_POOL_GROUPS
    tc = _tile(group, 256)

    def body(s_ref, o_ref):
        g = (pl.program_id(1) * tc) // group
        for gi, w in enumerate(POOL_WINDOWS):
            half = w // 2

            @pl.when(g == gi)
            def _():
                v = s_ref[...].astype(F32)
                cnt = _window_count(v.shape, half, seq)
                if backward:
                    u = v / cnt
                    res = (_window_halves(u, half, seq) - _shift_rows(u, -half, seq) + _shift_rows(u, half, seq)) - v
                else:
                    res = _window_halves(v, half, seq) / cnt - v
                o_ref[...] = res.astype(o_ref.dtype)

    spec = pl.BlockSpec((seq, tc), lambda b, j: (b, j))
    return pl.pallas_call(
        body, name=name, grid=(n_seq, d // tc), in_specs=[spec], out_specs=spec,
        out_shape=jax.ShapeDtypeStruct(src.shape, out_dtype), compiler_params=_params("parallel", "parallel"),
    )(src)


def _split3(x):
    x1 = x.astype(BF16)
    r1 = x - x1.astype(F32)
    x2 = r1.astype(BF16)
    x3 = (r1 - x2.astype(F32)).astype(BF16)
    return x1, x2, x3


def _tri_dot(tri, x):
    x1, x2, x3 = _split3(x)
    return _dot(tri, x1, NN) + _dot(tri, x2, NN) + _dot(tri, x3, NN)


def _chunk_terms(q_ref, k_ref, g_ref, rev, scale):
    c = q_ref.shape[0]
    ri = lax.broadcasted_iota(jnp.int32, (c, c), 0)
    ci = lax.broadcasted_iota(jnp.int32, (c, c), 1)
    causal = (ci >= ri) if rev else (ci <= ri)
    mid = c // 2 if rev else c // 2 - 1
    last = 0 if rev else c - 1
    b = _tri_dot(causal.astype(BF16), g_ref[...])
    b_mid, b_last = b[mid:mid + 1, :], b[last:last + 1, :]
    e_qm, e_km, e_qi, e_ks = jnp.exp(b - b_mid), jnp.exp(b_mid - b), jnp.exp(b), jnp.exp(b_last - b)
    q = q_ref[...].astype(F32) * scale
    k = k_ref[...].astype(F32)
    return dict(causal=causal, mid=mid, last=last, decay=jnp.exp(b_last), e_qm=e_qm, e_km=e_km, e_qi=e_qi, e_ks=e_ks,
                qm=q * e_qm, km=k * e_km, qi=q * e_qi, ks=k * e_ks)


def _gla_specs(n_chunks, heads, dk, dv, key_dim, rev, backward):
    def chunk(n):
        scan_pos = (n_chunks - 1 - n) if backward else n
        return (n_chunks - 1 - scan_pos) if rev else scan_pos

    def at(width, col0):
        return pl.BlockSpec((CHUNK, width), lambda b, h, n: (b * n_chunks + chunk(n), col0 // width + h))

    state = pl.BlockSpec((None, dv, dk), lambda b, h, n: ((b * heads + h) * n_chunks + chunk(n), 0, 0))
    return at(dk, 0), at(dk, key_dim), at(dv, 2 * key_dim), at(dk, 0), at(dv, 0), state


def _gla_fwd(name, pm, log_a, *, rev, n_seq, seq, key_dim, val_dim):
    heads, n_chunks = GLA_HEADS, seq // CHUNK
    dk, dv = key_dim // heads, val_dim // heads
    scale = dk ** -0.5
    q_spec, k_spec, v_spec, g_spec, o_spec, st_spec = _gla_specs(n_chunks, heads, dk, dv, key_dim, rev, False)

    def body(q_ref, k_ref, v_ref, g_ref, o_ref, st_ref, state):
        @pl.when(pl.program_id(2) == 0)
        def _():
            state[...] = jnp.zeros_like(state)

        t = _chunk_terms(q_ref, k_ref, g_ref, rev, scale)
        v = v_ref[...]
        p = jnp.where(t["causal"], _dot(t["qm"].astype(BF16), t["km"].astype(BF16), NT), 0.0)
        entering = state[...].astype(BF16)
        st_ref[...] = entering
        o_ref[...] = _dot(p.astype(BF16), v, NN) + _dot(t["qi"].astype(BF16), entering, NT)
        state[...] = t["decay"] * state[...] + _dot(v, t["ks"].astype(BF16), TN)

    tokens = n_seq * seq
    return pl.pallas_call(
        body, name=name, grid=(n_seq, heads, n_chunks), in_specs=[q_spec, k_spec, v_spec, g_spec],
        out_specs=[o_spec, st_spec],
        out_shape=[jax.ShapeDtypeStruct((tokens, val_dim), F32),
                   jax.ShapeDtypeStruct((n_seq * heads * n_chunks, dv, dk), BF16)],
        scratch_shapes=[pltpu.VMEM((dv, dk), F32)], compiler_params=_params("parallel", "parallel", "arbitrary"),
    )(pm, pm, pm, log_a)


def _gla_bwd(name, pm, log_a, states, d_o, *, rev, n_seq, seq, key_dim, val_dim):
    heads, n_chunks = GLA_HEADS, seq // CHUNK
    dk, dv = key_dim // heads, val_dim // heads
    scale = dk ** -0.5
    q_spec, k_spec, v_spec, g_spec, o_spec, st_spec = _gla_specs(n_chunks, heads, dk, dv, key_dim, rev, True)

    def body(q_ref, k_ref, v_ref, g_ref, do_ref, st_ref, dq_ref, dk_ref, dv_ref, dg_ref, dstate):
        @pl.when(pl.program_id(2) == 0)
        def _():
            dstate[...] = jnp.zeros_like(dstate)

        t = _chunk_terms(q_ref, k_ref, g_ref, rev, scale)
        causal = t["causal"]
        v, d_out, entering = v_ref[...], do_ref[...], st_ref[...]
        qm, km, qi, ks = (t[n].astype(BF16) for n in ("qm", "km", "qi", "ks"))
        dst = dstate[...]
        dst16 = dst.astype(BF16)
        p = jnp.where(causal, _dot(qm, km, NT), 0.0).astype(BF16)
        dp = jnp.where(causal, _dot(d_out, v, NT), 0.0).astype(BF16)
        dv_ref[...] = _dot(p, d_out, TN) + _dot(ks, dst16, NT)
        dqm, dkm = _dot(dp, km, NN), _dot(dp, qm, TN)
        dqi, dks = _dot(d_out, entering, NN), _dot(v, dst16, NN)
        d_decay = _colsum(dst * entering.astype(F32))
        dq_ref[...] = (dqi * t["e_qi"] + dqm * t["e_qm"]) * scale
        dk_ref[...] = dkm * t["e_km"] + dks * t["e_ks"]
        qm_term, km_term = dqm * t["qm"], dkm * t["km"]
        ks_term = dks * t["ks"]
        db = dqi * t["qi"] + qm_term - km_term - ks_term
        row = lax.broadcasted_iota(jnp.int32, db.shape, 0)
        db = db + jnp.where(row == t["mid"], _colsum(km_term - qm_term), 0.0)
        db = db + jnp.where(row == t["last"], _colsum(ks_term) + d_decay * t["decay"], 0.0)
        ri = lax.broadcasted_iota(jnp.int32, causal.shape, 0)
        ci = lax.broadcasted_iota(jnp.int32, causal.shape, 1)
        anti = (ci <= ri) if rev else (ci >= ri)
        dg_ref[...] = _tri_dot(anti.astype(BF16), db)
        dstate[...] = t["decay"] * dst + _dot(d_out, qi, TN)

    tokens = n_seq * seq
    kd = jax.ShapeDtypeStruct((tokens, key_dim), F32)
    return pl.pallas_call(
        body, name=name, grid=(n_seq, heads, n_chunks),
        in_specs=[q_spec, k_spec, v_spec, g_spec, o_spec, st_spec], out_specs=[g_spec, g_spec, o_spec, g_spec],
        out_shape=[kd, kd, jax.ShapeDtypeStruct((tokens, val_dim), F32), kd],
        scratch_shapes=[pltpu.VMEM((dv, dk), F32)], compiler_params=_params("parallel", "parallel", "arbitrary"),
    )(pm, pm, pm, log_a, d_o, states)


def _sigmoid_neg(z):
    e = jnp.exp(-jnp.abs(z))
    return jnp.where(z >= 0, e / (1.0 + e), 1.0 / (1.0 + e))


def _gate_fwd(name, pr, wf, bf, wb, bb, tr):
    kd = wf.shape[1]

    def fn(r, w_f, b_f, w_b, b_b):
        r16 = r.astype(BF16)
        out = []
        for w, b in ((w_f, b_f), (w_b, b_b)):
            z = _dot(r16, w.astype(BF16), NN) + b
            out.append((jnp.minimum(z, 0.0) - jnp.log1p(jnp.exp(-jnp.abs(z)))) / GATE_TAU)
        return tuple(out)

    return _rowwise(name, fn, n_rows=pr.shape[0], tr=tr, rows=[(pr, LANES, 0)], bcast=[wf, bf, wb, bb],
                    outs=[(kd, F32), (kd, F32)])


def _gate_bwd(name, pr, wf, bf, wb, bb, dgf, dgb, tr):
    kd = wf.shape[1]

    def fn(r, d_f, d_b, w_f, b_f, w_b, b_b):
        r16 = r.astype(BF16)
        dpr = jnp.zeros(r.shape, F32)
        sums = []
        for w, b, dg in ((w_f, b_f, d_f), (w_b, b_b, d_b)):
            w16 = w.astype(BF16)
            z = _dot(r16, w16, NN) + b
            dz = dg / GATE_TAU * _sigmoid_neg(z)
            dz16 = dz.astype(BF16)
            dpr = dpr + _dot(dz16, w16, NT)
            sums += [_dot(r16, dz16, TN), _colsum(dz)]
        return (dpr, *sums)

    return _rowwise(name, fn, n_rows=pr.shape[0], tr=tr, rows=[(pr, LANES, 0), (dgf, kd, 0), (dgb, kd, 0)],
                    bcast=[wf, bf, wb, bb], outs=[(LANES, BF16)], accs=[(LANES, kd), (1, kd), (LANES, kd), (1, kd)])


def _head_norm_fwd(name, of, ob, pm, gn, *, val_dim, gate_block, tr):
    dv = val_dim // GLA_HEADS

    def fn(o_f, o_b, gate, gain):
        out = []
        for h in range(GLA_HEADS):
            sl = slice(h * dv, (h + 1) * dv)
            o = o_f[:, sl] + o_b[:, sl]
            gt = gate[:, sl].astype(F32)
            on = o * lax.rsqrt(jnp.mean(o * o, axis=-1, keepdims=True) + EPS) * gain
            out.append(on * (gt * jax.nn.sigmoid(gt)))
        return (jnp.concatenate(out, axis=1),)

    return _rowwise(name, fn, n_rows=of.shape[0], tr=tr, rows=[(of, val_dim, 0), (ob, val_dim, 0), (pm, val_dim, gate_block)],
                    bcast=[gn], outs=[(val_dim, BF16)])[0]


def _head_norm_bwd(name, of, ob, pm, gn, dz, *, val_dim, gate_block, tr):
    dv = val_dim // GLA_HEADS

    def fn(o_f, o_b, gate, d_z, gain):
        d_o, d_gate = [], []
        d_gain = jnp.zeros((1, dv), F32)
        for h in range(GLA_HEADS):
            sl = slice(h * dv, (h + 1) * dv)
            o = o_f[:, sl] + o_b[:, sl]
            gt = gate[:, sl].astype(F32)
            dzh = d_z[:, sl]
            r = lax.rsqrt(jnp.mean(o * o, axis=-1, keepdims=True) + EPS)
            ohat = o * r
            sg = jax.nn.sigmoid(gt)
            d_gate.append(dzh * (ohat * gain) * (sg * (1.0 + gt * (1.0 - sg))))
            don = dzh * (gt * sg)
            d_gain = d_gain + _colsum(don * ohat)
            gd = gain * don
            d_o.append(r * (gd - ohat * jnp.mean(gd * ohat, axis=-1, keepdims=True)))
        return jnp.concatenate(d_o, axis=1), jnp.concatenate(d_gate, axis=1), d_gain

    return _rowwise(name, fn, n_rows=of.shape[0], tr=tr,
                    rows=[(of, val_dim, 0), (ob, val_dim, 0), (pm, val_dim, gate_block), (dz, val_dim, 0)], bcast=[gn],
                    outs=[(val_dim, BF16), (val_dim, BF16)], accs=[(1, dv)])


def _proj_grad(name, dqf, dqb, dkf, dkb, dvf, dvb, dgate, *, key_dim, val_dim, tr):
    def fn(q1, q2, k1, k2, v1, v2, gt):
        return (jnp.concatenate([(q1 + q2).astype(BF16), (k1 + k2).astype(BF16), (v1 + v2).astype(BF16), gt], axis=1),)

    rows = [(dqf, key_dim, 0), (dqb, key_dim, 0), (dkf, key_dim, 0), (dkb, key_dim, 0), (dvf, val_dim, 0),
            (dvb, val_dim, 0), (dgate, val_dim, 0)]
    return _rowwise(name, fn, n_rows=dqf.shape[0], tr=tr, rows=rows, outs=[(2 * key_dim + 2 * val_dim, BF16)])[0]


def _pool_out_grad(name, dh, y_pre, scale, tr):
    def fn(d, y, s):
        return d * s, _colsum(d * y)

    dm = dh.shape[1]
    return _rowwise(name, fn, n_rows=dh.shape[0], tr=tr, rows=[(dh, dm, 0), (y_pre, dm, 0)], bcast=[scale],
                    outs=[(dm, BF16)], accs=[(1, dm)])


def _mesh_place():
    x, y, c = lax.axis_index("x"), lax.axis_index("y"), lax.axis_index("c")
    return x, y, c, 4 * x + 2 * y + c


def _peer(x, y, c, p):
    px, py, pc = (x if not p & 4 else 1 - x), (y if not p & 2 else 1 - y), (c if not p & 1 else 1 - c)
    return (px, py, pc), 4 * px + 2 * py + pc


CHIP_BITS = (2, 4, 6)
N_CHIP = N_DEV // 2
GATHER_COPIES = 8
FLIP_C, FLIP_Y, FLIP_X = 1, 2, 4


def _transfer_sems(n_gather, n_scatter, n_pair):
    n_remote = n_gather * GATHER_COPIES + n_scatter * len(CHIP_BITS) + n_pair * N_CHIP
    return [pltpu.SemaphoreType.DMA((n_remote,)), pltpu.SemaphoreType.DMA((n_remote,)),
            pltpu.SemaphoreType.DMA((max(n_gather, 1),))]


def _transfers(g_in, g_out, s_in, s_out, p_in, p_out, send, recv, loc):
    x, y, c, me = _mesh_place()
    sibling, _ = _peer(x, y, c, FLIP_C)

    def remote(src, dst, k, peer):
        return pltpu.make_async_remote_copy(src_ref=src, dst_ref=dst, send_sem=send.at[k], recv_sem=recv.at[k],
                                            device_id=peer, device_id_type=MESH_ID)

    local, first, arrivals = [], [], []
    passed, passed_arrivals, last, last_arrivals = [], [], [], []
    k = 0
    for src, out in zip(g_in, g_out):
        half = src.shape[0] // 2
        halves = (pl.ds(0, half), pl.ds(half, half))
        x_chip, x_block = _peer(x, y, c, FLIP_X)
        y_chip, y_block = _peer(x, y, c, FLIP_Y)
        _, far_block = _peer(x, y, c, FLIP_X | FLIP_Y)
        local.append(pltpu.make_async_copy(src, out.at[me], loc.at[len(local)]))
        for peer, block in ((sibling, me ^ FLIP_C), (x_chip, x_block), (y_chip, y_block)):
            first.append(remote(src, out.at[me], k, peer))
            arrivals.append(remote(src, out.at[block], k, peer))
            k += 1
        for peer, block, part in ((x_chip, y_block, halves[0]), (y_chip, x_block, halves[1])):
            passed.append(remote(out.at[block, part], out.at[block, part], k, peer))
            passed_arrivals.append(remote(out.at[far_block, part], out.at[far_block, part], k, peer))
            k += 1
        for block in (x_block, y_block):
            passed.append(remote(out.at[block], out.at[block], k, sibling))
            passed_arrivals.append(remote(out.at[block ^ FLIP_C], out.at[block ^ FLIP_C], k, sibling))
            k += 1
        last.append(remote(out.at[far_block], out.at[far_block], k, sibling))
        last_arrivals.append(remote(out.at[far_block ^ FLIP_C], out.at[far_block ^ FLIP_C], k, sibling))
        k += 1
    n_gather_first = len(first)
    for src, out in zip(s_in, s_out):
        for j, p in enumerate(CHIP_BITS):
            peer, _ = _peer(x, y, c, p)
            peer_chip = 2 * peer[0] + peer[1]
            first.append(remote(src.at[peer_chip], out.at[j], k, peer))
            arrivals.append(remote(src.at[peer_chip], out.at[j], k, peer))
            k += 1
    for src, out in zip(p_in, p_out):
        for q in range(N_CHIP):
            first.append(remote(src.at[2 * q + 1 - c], out.at[q], k, sibling))
            arrivals.append(remote(src.at[2 * q + 1 - c], out.at[q], k, sibling))
            k += 1

    def start():
        for cp in local + first:
            cp.start()

    def middle():
        for cp in arrivals[:n_gather_first]:
            cp.wait_recv()
        for cp in passed:
            cp.start()

    def finish():
        for cp in passed_arrivals:
            cp.wait_recv()
        for cp in last:
            cp.start()
        for cp in last_arrivals + arrivals[n_gather_first:]:
            cp.wait_recv()
        for cp in first + passed + last:
            cp.wait_send()
        for cp in local:
            cp.wait()

    return start, middle, finish


def _exchange(name, gathers=(), scatters=(), pairs=()):
    n_g, n_s, n_p = len(gathers), len(scatters), len(pairs)
    n_in = n_g + n_s + n_p

    def body(*refs):
        ins, outs = refs[:n_in], refs[n_in:-3]
        start, middle, finish = _transfers(ins[:n_g], outs[:n_g], ins[n_g:n_g + n_s], outs[n_g:n_g + n_s], ins[n_g + n_s:],
                                           outs[n_g + n_s:], *refs[-3:])
        start()
        middle()
        finish()

    hbm = pl.BlockSpec(memory_space=pl.ANY)
    out_shape = [jax.ShapeDtypeStruct((N_DEV,) + g.shape, g.dtype) for g in gathers]
    out_shape += [jax.ShapeDtypeStruct((len(CHIP_BITS),) + t.shape[1:], t.dtype) for t in scatters]
    out_shape += [jax.ShapeDtypeStruct((N_CHIP,) + t.shape[1:], t.dtype) for t in pairs]
    return pl.pallas_call(
        body, name=name, in_specs=[hbm] * n_in, out_specs=[hbm] * len(out_shape), out_shape=out_shape,
        scratch_shapes=_transfer_sems(n_g, n_s, n_p),
    )(*gathers, *scatters, *pairs)


def _place_scalar(axis_value):
    return jnp.asarray(axis_value, jnp.int32).reshape(1)


def _chip_sums(name, grad):
    received = _exchange(name + "_pair", pairs=[grad])[0]
    cols = grad.shape[-1]
    rows = received.size // cols // N_CHIP
    tr = _row_tile(rows, (512 * 1024) // cols)

    def body(core_ref, mine_ref, got_ref, out_ref):
        out_ref[...] = (mine_ref[...].astype(F32) + got_ref[...].astype(F32)).astype(out_ref.dtype)

    block = (None, tr, cols)
    out = pl.pallas_call(
        body, name=name + "_add", out_shape=jax.ShapeDtypeStruct((N_CHIP, rows, cols), grad.dtype),
        grid_spec=pltpu.PrefetchScalarGridSpec(
            num_scalar_prefetch=1, grid=(N_CHIP, rows // tr),
            in_specs=[pl.BlockSpec(block, lambda q, i, core: (2 * q + core[0], i, 0)),
                      pl.BlockSpec(block, lambda q, i, core: (q, i, 0))],
            out_specs=pl.BlockSpec(block, lambda q, i, core: (q, i, 0))),
        compiler_params=_params("parallel", "parallel"),
    )(_place_scalar(lax.axis_index("c")), grad.reshape(N_DEV, rows, cols), received.reshape(N_CHIP, rows, cols))
    return out.reshape(received.shape)


def _all_reduce_small(name, pack):
    def body(in_ref, out_ref, parts, send_sems, recv_sems):
        x, y, c, me = _mesh_place()
        parts[me] = in_ref[...]

        def copy(p, arriving):
            peer, peer_block = _peer(x, y, c, p)
            return pltpu.make_async_remote_copy(src_ref=in_ref, dst_ref=parts.at[peer_block if arriving else me],
                                                send_sem=send_sems.at[p - 1], recv_sem=recv_sems.at[p - 1],
                                                device_id=peer, device_id_type=MESH_ID)

        sends = [copy(p, False) for p in range(1, N_DEV)]
        for cp in sends:
            cp.start()
        for p in range(1, N_DEV):
            copy(p, True).wait_recv()
        for cp in sends:
            cp.wait_send()
        total = parts[0]
        for j in range(1, N_DEV):
            total = total + parts[j]
        out_ref[...] = total

    vmem = pl.BlockSpec(memory_space=pltpu.VMEM)
    return pl.pallas_call(
        body, name=name, in_specs=[vmem], out_specs=vmem, out_shape=jax.ShapeDtypeStruct(pack.shape, F32),
        scratch_shapes=[pltpu.VMEM((N_DEV,) + pack.shape, F32), pltpu.SemaphoreType.DMA((N_PEER,)),
                        pltpu.SemaphoreType.DMA((N_PEER,))],
        compiler_params=pltpu.CompilerParams(vmem_limit_bytes=VMEM_LIMIT_BYTES),
    )(pack)


def _adamw_math(w, g, m, v):
    m = ADAM_B1 * m + (1.0 - ADAM_B1) * g
    v = ADAM_B2 * v + (1.0 - ADAM_B2) * jnp.square(g)
    m_hat = m / (1.0 - ADAM_B1 ** ADAM_STEP)
    v_hat = v / (1.0 - ADAM_B2 ** ADAM_STEP)
    delta = -ADAM_LR * (m_hat / (jnp.sqrt(v_hat) + ADAM_EPS) + ADAM_WD * w)
    return delta, m, v


def _adamw_shard(name, w, m, v, grad_parts, layer=0, into=None):
    landed, sums = grad_parts
    shape = w.shape
    n_layers, cols = shape[0], shape[-1]
    rows = w.size // cols // n_layers
    tr = _row_tile(rows, (256 * 1024) // cols)
    n_into = 0 if into is None else 4
    n_landed = landed.shape[0]

    def body(chip_ref, w_ref, m_ref, v_ref, mine_ref, got_ref, *refs):
        g_ref, d_ref, nm_ref, nv_ref = refs[n_into:]
        g = mine_ref[...].astype(F32)
        for j in range(n_landed):
            g = g + got_ref[j].astype(F32)
        delta, new_m, new_v = _adamw_math(w_ref[...], g, m_ref[...], v_ref[...])
        g_ref[...], d_ref[...], nm_ref[...], nv_ref[...] = g, delta, new_m, new_v

    if n_layers == 1:
        view = (rows, cols)
        spec = pl.BlockSpec((tr, cols), lambda i, chip: (i, 0))
    else:
        view = (n_layers, rows, cols)
        spec = pl.BlockSpec((None, tr, cols), lambda i, chip: (layer, i, 0))
    operands = [w.reshape(view), m.reshape(view), v.reshape(view), sums.reshape(N_CHIP, rows, cols),
                landed.reshape(n_landed, rows, cols)]
    in_specs = [spec, spec, spec, pl.BlockSpec((None, tr, cols), lambda i, chip: (chip[0], i, 0)),
                pl.BlockSpec((n_landed, tr, cols), lambda i, chip: (0, i, 0))]
    if into is not None:
        operands += [t.reshape(view) for t in into]
        in_specs += [pl.BlockSpec(memory_space=pl.ANY)] * 4
    first_into = 1 + 5
    res = pl.pallas_call(
        body, name=name, out_shape=[jax.ShapeDtypeStruct(view, F32)] * 4,
        grid_spec=pltpu.PrefetchScalarGridSpec(num_scalar_prefetch=1, grid=(rows // tr,), in_specs=in_specs, out_specs=[spec] * 4),
        input_output_aliases={first_into + t: t for t in range(n_into)}, compiler_params=_params("parallel"),
    )(_place_scalar(2 * lax.axis_index("x") + lax.axis_index("y")), *operands)
    return [r.reshape(shape) for r in res]


def _adamw_small(name, w, m, v, g):
    shape = w.shape
    two_d = (1, shape[0]) if len(shape) == 1 else (w.size // shape[-1], shape[-1])

    def body(w_ref, m_ref, v_ref, g_ref, d_ref, nm_ref, nv_ref):
        d_ref[...], nm_ref[...], nv_ref[...] = _adamw_math(w_ref[...], g_ref[...], m_ref[...], v_ref[...])

    res = pl.pallas_call(body, name=name, out_shape=[jax.ShapeDtypeStruct(two_d, F32)] * 3)(
        w.reshape(two_d), m.reshape(two_d), v.reshape(two_d), g.reshape(two_d))
    return [r.reshape(shape) for r in res]


def _mlp_fwd(tag, h_in, gain, w1, w2_mine, down_gathers, tm, tr):
    tokens, d = h_in.shape
    fs = w1.shape[2]
    f = N_DEV * fs
    hn = _rms_fwd(f"rms_mlp{tag}", h_in, gain, BF16, tr)
    tn, tk = _tile(fs, 1024), _tile(d, TILE_K)
    per = fs // tn

    def relu2(acc):
        a = jnp.maximum(acc, 0.0)
        return acc, a * a

    a_pre, act, w2 = _mm(
        f"mlp_up{tag}", hn, w1, grid=(tokens // tm, f // tn, d // tk), a_spec=_spec2(tm, tk, lambda m, n, k: (m, k)),
        b_spec=pl.BlockSpec((None, tk, tn), lambda m, n, k: (n // per, k, n % per)),
        o_spec=_spec2(tm, tn, lambda m, n, k: (m, n)), acc_shape=(tm, tn), outs=[((tokens, f), BF16)] * 2, dims=NN, epi=relu2,
        gathers=[w2_mine])
    tn2, tk2 = _tile(d, 1024), _tile(fs, TILE_K)
    kper = fs // tk2
    h_out, *got = _mm(
        f"mlp_down{tag}", act, w2, grid=(tokens // tm, d // tn2, f // tk2), a_spec=_spec2(tm, tk2, lambda m, n, k: (m, k)),
        b_spec=pl.BlockSpec((None, tk2, tn2), lambda m, n, k: (k // kper, k % kper, n)),
        o_spec=_spec2(tm, tn2, lambda m, n, k: (m, n)), acc_shape=(tm, tn2), outs=[((tokens, d), F32)], dims=NN,
        extras=[(h_in, _spec2(tm, tn2, lambda m, n, k: (m, n)))], epi=lambda acc, res: (res + acc,), gathers=down_gathers)
    return hn, a_pre, act, h_out, w2, got


def _mlp_bwd(tag, h_in, gain, hn, a_pre, act, dh, dh16, w1, w2, tm, tr):
    tokens, d = h_in.shape
    fs = w1.shape[2]
    f = N_DEV * fs
    tn, tk = _tile(fs, 1024), _tile(d, TILE_K)
    nper = fs // tn
    da = _mm(
        f"mlp_dact{tag}", dh16, w2, grid=(tokens // tm, f // tn, d // tk), a_spec=_spec2(tm, tk, lambda m, n, k: (m, k)),
        b_spec=pl.BlockSpec((None, tn, tk), lambda m, n, k: (n // nper, n % nper, k)),
        o_spec=_spec2(tm, tn, lambda m, n, k: (m, n)), acc_shape=(tm, tn), outs=[((tokens, f), BF16)], dims=NT,
        extras=[(a_pre, _spec2(tm, tn, lambda m, n, k: (m, n)))],
        epi=lambda acc, pre: (acc * (2.0 * jnp.maximum(pre.astype(F32), 0.0)),))[0]
    tkt = _tile(tokens, TILE_K)
    tm2, tn2 = _tile(fs, 1024), _tile(d, 1024)
    mper = fs // tm2
    dw2 = _mm(
        f"mlp_dw_out{tag}", act, dh16, grid=(f // tm2, d // tn2, tokens // tkt), a_spec=_spec2(tkt, tm2, lambda m, n, k: (k, m)),
        b_spec=_spec2(tkt, tn2, lambda m, n, k: (k, n)),
        o_spec=pl.BlockSpec((None, tm2, tn2), lambda m, n, k: (m // mper, m % mper, n)), acc_shape=(tm2, tn2),
        outs=[((N_DEV, fs, d), BF16)], dims=TN)[0]
    dw2_sums = _chip_sums(f"mlp_dw_out{tag}", dw2)
    tm3 = _tile(d, 1024)
    dw1, dw2_landed = _mm(
        f"mlp_dw_in{tag}", hn, da, grid=(d // tm3, f // tn, tokens // tkt), a_spec=_spec2(tkt, tm3, lambda m, n, k: (k, m)),
        b_spec=_spec2(tkt, tn, lambda m, n, k: (k, n)),
        o_spec=pl.BlockSpec((None, tm3, tn), lambda m, n, k: (n // nper, m, n % nper)), acc_shape=(tm3, tn),
        outs=[((N_DEV, d, fs), BF16)], dims=TN, scatters=[dw2_sums])
    dw1_sums = _chip_sums(f"mlp_dw_in{tag}", dw1)
    tn4, tk4 = _tile(d, 1024), _tile(fs, TILE_K)
    kper = fs // tk4
    dhn, dw1_landed = _mm(
        f"mlp_dx{tag}", da, w1, grid=(tokens // tm, d // tn4, f // tk4), a_spec=_spec2(tm, tk4, lambda m, n, k: (m, k)),
        b_spec=pl.BlockSpec((None, tn4, tk4), lambda m, n, k: (k // kper, n, k % kper)),
        o_spec=_spec2(tm, tn4, lambda m, n, k: (m, n)), acc_shape=(tm, tn4), outs=[((tokens, d), F32)], dims=NT,
        scatters=[dw1_sums])
    dh_in, dh_in16, dgain = _rms_bwd(f"rms_mlp_bwd{tag}", h_in, gain, dhn, dh, tr)
    return dh_in, dh_in16, dgain, (dw1_landed, dw1_sums), (dw2_landed, dw2_sums)


def _plain_mm(name, a, b, dims, out_dtype, *, tm, tn_pref=1024, tk_pref=TILE_K, extras=None, epi=None, gathers=(), scatters=()):
    if dims == NN:
        (m, k), n = a.shape, b.shape[1]
    elif dims == NT:
        (m, k), n = a.shape, b.shape[0]
    else:
        (k, m), n = a.shape, b.shape[1]
    tm, tn, tk = _tile(m, tm), _tile(n, tn_pref), _tile(k, tk_pref)
    a_spec = _spec2(tk, tm, lambda i, j, l: (l, i)) if dims == TN else _spec2(tm, tk, lambda i, j, l: (i, l))
    b_spec = _spec2(tn, tk, lambda i, j, l: (j, l)) if dims == NT else _spec2(tk, tn, lambda i, j, l: (l, j))
    o_spec = _spec2(tm, tn, lambda i, j, l: (i, j))
    ex = [(e, o_spec) for e in (extras or [])]
    res = _mm(name, a, b, grid=(m // tm, n // tn, k // tk), a_spec=a_spec, b_spec=b_spec, o_spec=o_spec, acc_shape=(tm, tn),
              outs=[((m, n), out_dtype)], dims=dims, extras=ex, epi=epi, gathers=gathers, scatters=scatters)
    return res if (gathers or scatters) else res[0]


def kernel(x, norm_mix, norm_mlp, norm_final, pool_w, pool_scale, gla_w_in, gla_w_up_f, gla_b_up_f, gla_w_up_b, gla_b_up_b, gla_g_norm, gla_w_out, mlp_w_in, mlp_w_out, loss_target, m_norm_mix, m_norm_mlp, m_norm_final, m_pool_w, m_pool_scale, m_gla_w_in, m_gla_w_up_f, m_gla_b_up_f, m_gla_w_up_b, m_gla_b_up_b, m_gla_g_norm, m_gla_w_out, m_mlp_w_in, m_mlp_w_out, v_norm_mix, v_norm_mlp, v_norm_final, v_pool_w, v_pool_scale, v_gla_w_in, v_gla_w_up_f, v_gla_b_up_f, v_gla_w_up_b, v_gla_b_up_b, v_gla_g_norm, v_gla_w_out, v_mlp_w_in, v_mlp_w_out):
    n_seq, seq, d = x.shape
    tokens = n_seq * seq
    group = d // N_POOL_GROUPS
    key_dim, val_dim = d // 2, d
    dv = val_dim // GLA_HEADS
    n_main = 2 * key_dim + 2 * val_dim
    proj_shard = gla_w_in.shape[2]
    proj_width = N_DEV * proj_shard
    kd_shard = key_dim // N_DEV
    assert proj_width == n_main + 2 * GATE_RANK and seq % CHUNK == 0 and 2 * GATE_RANK <= LANES
    assert mlp_w_in.shape[0] == 2 and pool_w.shape[0] == 1 and gla_w_in.shape[0] == 1
    me = 4 * lax.axis_index("x") + 2 * lax.axis_index("y") + lax.axis_index("c")
    tm = _tile(tokens, 1024)
    tr = _tile(tokens, 128)
    x2, target = x.reshape(tokens, d), loss_target.reshape(tokens, d)

    small_rows = 2 * GATE_RANK + 4
    small = jnp.concatenate([
        gla_w_up_f[0], gla_w_up_b[0], gla_b_up_f, gla_b_up_b,
        jnp.pad(gla_g_norm, ((0, 0), (0, kd_shard - gla_g_norm.shape[1]))), jnp.zeros((1, kd_shard), F32)], axis=0)
    w1_mine = [mlp_w_in[l].astype(BF16) for l in range(2)]
    w2_mine = [mlp_w_out[l].astype(BF16) for l in range(2)]
    pool_g, small_g, w1_0 = _exchange(
        "gather_first", gathers=[pool_w[0].astype(BF16), small.reshape(2, small_rows // 2, kd_shard), w1_mine[0]])
    small_g = small_g.reshape(N_DEV, small_rows, kd_shard)
    w_pool = pool_g.transpose(1, 0, 2, 3).reshape(N_POOL_GROUPS, group, group)
    small_full = small_g.transpose(1, 0, 2).reshape(small_rows, key_dim)
    pad_rows = lambda w, r0: jnp.pad(w, ((r0, LANES - r0 - GATE_RANK), (0, 0)))
    w_up_f, w_up_b = pad_rows(small_full[:GATE_RANK], 0), pad_rows(small_full[GATE_RANK:2 * GATE_RANK], GATE_RANK)
    b_up_f, b_up_b = small_full[2 * GATE_RANK:2 * GATE_RANK + 1], small_full[2 * GATE_RANK + 1:2 * GATE_RANK + 2]
    g_norm = small_g[:, 2 * GATE_RANK + 2, :gla_g_norm.shape[1]].reshape(1, dv)

    hn0 = _rms_fwd("rms_mix0", x2, norm_mix[0:1], F32, tr)
    pooled = _pool_slabs("pool_fwd", hn0, BF16, n_seq=n_seq, seq=seq, backward=False)
    tn, tk = _tile(group, 1024), _tile(group, TILE_K)
    nper, kper = group // tn, group // tk
    y_pre, h1 = _mm(
        "pool_mix", pooled, w_pool, grid=(tokens // tm, d // tn, group // tk),
        a_spec=_spec2(tm, tk, lambda m, n, k: (m, (n // nper) * kper + k)),
        b_spec=pl.BlockSpec((None, tk, tn), lambda m, n, k: (n // nper, k, n % nper)),
        o_spec=_spec2(tm, tn, lambda m, n, k: (m, n)), acc_shape=(tm, tn), outs=[((tokens, d), F32)] * 2, dims=NN,
        extras=[(x2, _spec2(tm, tn, lambda m, n, k: (m, n))), (pool_scale, _spec2(1, tn, lambda m, n, k: (0, n)))],
        epi=lambda acc, res, s: (acc, res + acc * s))
    hn1, a_pre0, act0, h2, w2_0, (win_g, wout_g) = _mlp_fwd(
        "0", h1, norm_mlp[0:1], w1_0, w2_mine[0], [gla_w_in[0].astype(BF16), gla_w_out[0].astype(BF16)], tm, tr)
    w_in = win_g.transpose(1, 0, 2).reshape(d, proj_width)
    w_main = w_in[:, :n_main]
    w_r = jnp.pad(w_in[:, n_main:], ((0, 0), (0, LANES - 2 * GATE_RANK)))
    w_out = wout_g.reshape(val_dim, d)

    hn2 = _rms_fwd("rms_mix1", h2, norm_mix[1:2], BF16, tr)
    pm, w1_1 = _plain_mm("gla_proj", hn2, w_main, NN, BF16, tm=tm, gathers=[w1_mine[1]])
    pr = _plain_mm("gla_proj_gate", hn2, w_r, NN, F32, tm=tm)
    log_a_f, log_a_b = _gate_fwd("gla_gate", pr, w_up_f, b_up_f, w_up_b, b_up_b, tr)
    dims = dict(n_seq=n_seq, seq=seq, key_dim=key_dim, val_dim=val_dim)
    o_f, st_f = _gla_fwd("gla_scan_f", pm, log_a_f, rev=False, **dims)
    o_b, st_b = _gla_fwd("gla_scan_b", pm, log_a_b, rev=True, **dims)
    gate_block = (2 * key_dim + val_dim) // val_dim
    z = _head_norm_fwd("gla_head_norm", o_f, o_b, pm, g_norm, val_dim=val_dim, gate_block=gate_block, tr=tr)
    h3 = _plain_mm("gla_out", z, w_out, NN, F32, tm=tm, extras=[h2], epi=lambda acc, res: (res + acc,))
    hn3, a_pre1, act1, h4, w2_1, _ = _mlp_fwd("1", h3, norm_mlp[1:2], w1_1, w2_mine[1], [], tm, tr)

    dh4, dh4_16, loss_row, d_norm_final = _loss_head("loss_head", h4, target, norm_final.reshape(1, d), tr)

    dh3, dh3_16, d_norm_mlp1, dw1_1, dw2_1 = _mlp_bwd("1", h3, norm_mlp[1:2], hn3, a_pre1, act1, dh4, dh4_16, w1_1, w2_1, tm, tr)
    dz = _plain_mm("gla_out_dx", dh3_16, w_out, NT, F32, tm=tm)
    dw_out = _plain_mm("gla_out_dw", z, dh3_16, TN, BF16, tm=1024)
    dw_out_sums = _chip_sums("gla_out_dw", dw_out.reshape(N_DEV, val_dim // N_DEV, d))
    d_o, d_gate, d_g_norm = _head_norm_bwd("gla_head_norm_bwd", o_f, o_b, pm, g_norm, dz, val_dim=val_dim,
                                           gate_block=gate_block, tr=tr)
    dq_f, dk_f, dv_f, dla_f = _gla_bwd("gla_scan_bwd_f", pm, log_a_f, st_f, d_o, rev=False, **dims)
    dq_b, dk_b, dv_b, dla_b = _gla_bwd("gla_scan_bwd_b", pm, log_a_b, st_b, d_o, rev=True, **dims)
    dpr, dw_up_f, db_up_f, dw_up_b, db_up_b = _gate_bwd("gla_gate_bwd", pr, w_up_f, b_up_f, w_up_b, b_up_b, dla_f, dla_b, tr)
    dpm = _proj_grad("gla_proj_grad", dq_f, dq_b, dk_f, dk_b, dv_f, dv_b, d_gate, key_dim=key_dim, val_dim=val_dim, tr=tr)
    dw_main, dw_out_landed = _plain_mm("gla_proj_dw", hn2, dpm, TN, BF16, tm=1024, scatters=[dw_out_sums])
    dw_r = _plain_mm("gla_proj_gate_dw", hn2, dpr, TN, BF16, tm=1024)
    dw_in = jnp.concatenate([dw_main, dw_r[:, :2 * GATE_RANK]], axis=1)
    dw_in_sums = _chip_sums("gla_proj_dw", dw_in.reshape(d, N_DEV, proj_shard).transpose(1, 0, 2))
    dhn2, dw_in_landed = _plain_mm("gla_proj_dx", dpm, w_main, NT, F32, tm=tm, scatters=[dw_in_sums])
    dhn2 = _plain_mm("gla_proj_gate_dx", dpr, w_r, NT, F32, tm=tm, extras=[dhn2], epi=lambda acc, res: (res + acc,))
    dh2, dh2_16, d_norm_mix1 = _rms_bwd("rms_mix1_bwd", h2, norm_mix[1:2], dhn2, dh3, tr)

    dh1, _, d_norm_mlp0, dw1_0, dw2_0 = _mlp_bwd("0", h1, norm_mlp[0:1], hn1, a_pre0, act0, dh2, dh2_16, w1_0, w2_0, tm, tr)
    dyp, d_pool_scale = _pool_out_grad("pool_out_grad", dh1, y_pre, pool_scale, tr)
    dpooled = _mm(
        "pool_mix_dx", dyp, w_pool, grid=(tokens // tm, d // tn, group // tk),
        a_spec=_spec2(tm, tk, lambda m, n, k: (m, (n // nper) * kper + k)),
        b_spec=pl.BlockSpec((None, tn, tk), lambda m, n, k: (n // nper, n % nper, k)),
        o_spec=_spec2(tm, tn, lambda m, n, k: (m, n)), acc_shape=(tm, tn), outs=[((tokens, d), F32)], dims=NT)[0]
    tmw, tkt = _tile(group, 1024), _tile(tokens, TILE_K)
    mper = group // tmw
    dw_pool = _mm(
        "pool_mix_dw", pooled, dyp, grid=(d // tmw, group // tn, tokens // tkt),
        a_spec=_spec2(tkt, tmw, lambda m, n, k: (k, m)), b_spec=_spec2(tkt, tn, lambda m, n, k: (k, (m // mper) * nper + n)),
        o_spec=pl.BlockSpec((None, tmw, tn), lambda m, n, k: (m // mper, m % mper, n)), acc_shape=(tmw, tn),
        outs=[((N_POOL_GROUPS, group, group), BF16)], dims=TN)[0]
    dw_pool_sums = _chip_sums("pool_mix_dw", dw_pool.reshape(N_POOL_GROUPS, N_DEV, group // N_DEV, group).transpose(1, 0, 2, 3))
    dw_pool_landed = _exchange("scatter_pool", scatters=[dw_pool_sums])[0]
    dhn0 = _pool_slabs("pool_bwd", dpooled, F32, n_seq=n_seq, seq=seq, backward=True)
    dx, _, d_norm_mix0 = _rms_bwd("rms_mix0_bwd", x2, norm_mix[0:1], dhn0, dh1, tr)

    pack_cols = key_dim
    rows_of = lambda a: a.reshape(-1, pack_cols)
    pieces = [rows_of(t) for t in (d_norm_mix0, d_norm_mix1, d_norm_mlp0, d_norm_mlp1, d_norm_final, d_pool_scale,
                                   dw_up_f[:GATE_RANK], dw_up_b[GATE_RANK:2 * GATE_RANK], db_up_f, db_up_b)]
    pieces += [jnp.pad(d_g_norm, ((0, 0), (0, pack_cols - dv))), jnp.pad(loss_row, ((0, 0), (0, pack_cols - LANES)))]
    n_rows = sum(p.shape[0] for p in pieces)
    pack = jnp.concatenate(pieces + [jnp.zeros((-n_rows % 8, pack_cols), F32)], axis=0)
    red = _all_reduce_small("reduce_small", pack)
    cuts, r0 = [], 0
    for p in pieces:
        cuts.append(red[r0:r0 + p.shape[0]])
        r0 += p.shape[0]
    g_norm_mix = jnp.concatenate([cuts[0].reshape(1, d), cuts[1].reshape(1, d)], axis=0)
    g_norm_mlp = jnp.concatenate([cuts[2].reshape(1, d), cuts[3].reshape(1, d)], axis=0)
    g_norm_final = cuts[4].reshape(d)
    g_pool_scale = cuts[5].reshape(1, d)
    shard_cols = lambda t, width: lax.dynamic_slice_in_dim(t, me * width, width, axis=1)
    g_w_up_f, g_w_up_b = shard_cols(cuts[6], kd_shard)[None], shard_cols(cuts[7], kd_shard)[None]
    g_b_up_f, g_b_up_b = shard_cols(cuts[8], kd_shard), shard_cols(cuts[9], kd_shard)
    g_g_norm = shard_cols(cuts[10][:, :dv], dv // N_DEV)
    loss = cuts[11][0, 0]

    big = {
        "pool_w": _adamw_shard("adamw_pool_w", pool_w, m_pool_w, v_pool_w, (dw_pool_landed, dw_pool_sums)),
        "gla_w_in": _adamw_shard("adamw_gla_w_in", gla_w_in, m_gla_w_in, v_gla_w_in, (dw_in_landed, dw_in_sums)),
        "gla_w_out": _adamw_shard("adamw_gla_w_out", gla_w_out, m_gla_w_out, v_gla_w_out, (dw_out_landed, dw_out_sums)),
    }
    for nm, w, m, v, landed in (("mlp_w_in", mlp_w_in, m_mlp_w_in, v_mlp_w_in, (dw1_0, dw1_1)),
                                ("mlp_w_out", mlp_w_out, m_mlp_w_out, v_mlp_w_out, (dw2_0, dw2_1))):
        layer1 = _adamw_shard(f"adamw_{nm}1", w, m, v, landed[1], layer=1)
        big[nm] = _adamw_shard(f"adamw_{nm}0", w, m, v, landed[0], layer=0, into=layer1)
    grads = {"norm_mix": g_norm_mix, "norm_mlp": g_norm_mlp, "norm_final": g_norm_final, "pool_scale": g_pool_scale,
             "gla_w_up_f": g_w_up_f, "gla_b_up_f": g_b_up_f, "gla_w_up_b": g_w_up_b, "gla_b_up_b": g_b_up_b, "gla_g_norm": g_g_norm}
    state = {"norm_mix": (norm_mix, m_norm_mix, v_norm_mix), "norm_mlp": (norm_mlp, m_norm_mlp, v_norm_mlp),
             "norm_final": (norm_final, m_norm_final, v_norm_final), "pool_scale": (pool_scale, m_pool_scale, v_pool_scale),
             "gla_w_up_f": (gla_w_up_f, m_gla_w_up_f, v_gla_w_up_f), "gla_b_up_f": (gla_b_up_f, m_gla_b_up_f, v_gla_b_up_f),
             "gla_w_up_b": (gla_w_up_b, m_gla_w_up_b, v_gla_w_up_b), "gla_b_up_b": (gla_b_up_b, m_gla_b_up_b, v_gla_b_up_b),
             "gla_g_norm": (gla_g_norm, m_gla_g_norm, v_gla_g_norm)}
    order = ["norm_mix", "norm_mlp", "norm_final", "pool_w", "pool_scale", "gla_w_in", "gla_w_up_f", "gla_b_up_f", "gla_w_up_b",
             "gla_b_up_b", "gla_g_norm", "gla_w_out", "mlp_w_in", "mlp_w_out"]
    out_g, out_d, out_m, out_v = [], [], [], []
    for nm in order:
        if nm in big:
            g, dl, new_m, new_v = big[nm]
        else:
            g = grads[nm]
            dl, new_m, new_v = _adamw_small(f"adamw_{nm}", *state[nm], g)
        out_g.append(g), out_d.append(dl), out_m.append(new_m), out_v.append(new_v)
    return (loss, dx.reshape(x.shape), *out_g, *out_d, *out_m, *out_v)
```

```python
import functools

import jax
import jax.numpy as jnp
from jax import lax
from jax.experimental import pallas as pl
from jax.experimental.pallas import tpu as pltpu

F32 = jnp.float32
BF16 = jnp.bfloat16
N_DEV = 8
N_PEER = N_DEV - 1
LANES = 128
VMEM_LIMIT_BYTES = 48 * 1024 * 1024
MESH_ID = pl.DeviceIdType.MESH
TILE_K = 1024

POOL_WINDOWS = (2, 4, 8, 16)
N_POOL_GROUPS = len(POOL_WINDOWS)
GLA_HEADS = 4
GATE_RANK = 16
GATE_TAU = 16.0
CHUNK = 64
EPS = 1e-6
ADAM_LR = 0.001
ADAM_B1 = 0.9
ADAM_B2 = 0.999
ADAM_EPS = 1e-08
ADAM_WD = 0.01
ADAM_STEP = 10

NN = ((1,), (0,))
NT = ((1,), (1,))
TN = ((0,), (0,))


def _tile(dim, pref):
    if dim <= pref:
        return dim
    t = (pref // LANES) * LANES
    while t > LANES and dim % t:
        t -= LANES
    assert dim % t == 0, (dim, pref)
    return t


def _row_tile(rows, pref):
    if rows <= pref:
        return rows
    t = max(8, pref // 8 * 8)
    while t > 8 and rows % t:
        t -= 8
    assert rows % t == 0, (rows, pref)
    return t


def _params(*sem):
    return pltpu.CompilerParams(dimension_semantics=sem, vmem_limit_bytes=VMEM_LIMIT_BYTES)


def _dot(a, b, dims):
    return lax.dot_general(a, b, (dims, ((), ())), preferred_element_type=F32)


def _mm(name, a, b, *, grid, a_spec, b_spec, o_spec, acc_shape, outs, dims, extras=(), epi=None, gathers=(), scatters=()):
    n_ex, n_out, nk = len(extras), len(outs), grid[2]
    n_g, n_s = len(gathers), len(scatters)
    carries = bool(n_g or n_s)

    def body(*refs):
        a_ref, b_ref = refs[0], refs[1]
        ex = refs[2:2 + n_ex]
        r0 = 2 + n_ex
        g_in, s_in = refs[r0:r0 + n_g], refs[r0 + n_g:r0 + n_g + n_s]
        r0 += n_g + n_s
        o_refs = refs[r0:r0 + n_out]
        g_out, s_out = refs[r0 + n_out:r0 + n_out + n_g], refs[r0 + n_out + n_g:r0 + n_out + n_g + n_s]
        acc = refs[r0 + n_out + n_g + n_s]
        i, j, k = pl.program_id(0), pl.program_id(1), pl.program_id(2)
        if carries:
            start, middle, finish = _transfers(g_in, g_out, s_in, s_out, (), (), *refs[-3:])
            step = (i * grid[1] + j) * nk + k

            @pl.when(step == 0)
            def _():
                start()

            if n_g:
                @pl.when(step == (grid[0] * grid[1] * nk * 5) // 8)
                def _():
                    middle()

        def part():
            return _dot(a_ref[...].astype(BF16), b_ref[...].astype(BF16), dims)

        def finish_tile(total):
            res = epi(total, *[e[...] for e in ex]) if epi is not None else (total,)
            for o, r in zip(o_refs, res):
                o[...] = r.astype(o.dtype)

        if nk == 1:
            finish_tile(part())
        else:
            @pl.when(k == 0)
            def _():
                acc[...] = part()

            @pl.when((k > 0) & (k < nk - 1))
            def _():
                acc[...] += part()

            @pl.when(k == nk - 1)
            def _():
                finish_tile(acc[...] + part())

        if carries:
            @pl.when((i == grid[0] - 1) & (j == grid[1] - 1) & (k == nk - 1))
            def _():
                finish()

    hbm = pl.BlockSpec(memory_space=pl.ANY)
    operands = [a, b] + [e for e, _ in extras] + list(gathers) + list(scatters)
    in_specs = [a_spec, b_spec] + [s for _, s in extras] + [hbm] * (n_g + n_s)
    out_shape = [jax.ShapeDtypeStruct(s, d) for s, d in outs]
    out_shape += [jax.ShapeDtypeStruct((N_DEV,) + g.shape, g.dtype) for g in gathers]
    out_shape += [jax.ShapeDtypeStruct((len(CHIP_BITS),) + t.shape[1:], t.dtype) for t in scatters]
    scratch = [pltpu.VMEM(acc_shape, F32)] + (_transfer_sems(n_g, n_s, 0) if carries else [])
    sem = ("arbitrary",) * 3 if carries else ("parallel", "parallel", "arbitrary")
    return pl.pallas_call(
        body, name=name, grid=grid, in_specs=in_specs, out_specs=[o_spec] * n_out + [hbm] * (n_g + n_s),
        out_shape=out_shape, scratch_shapes=scratch, compiler_params=_params(*sem),
    )(*operands)


def _spec2(rows, cols, fn):
    return pl.BlockSpec((rows, cols), fn)


def _rowwise(name, fn, *, n_rows, tr, rows, bcast=(), outs=(), accs=()):
    n_r, n_b, n_o, n_a = len(rows), len(bcast), len(outs), len(accs)

    def body(*refs):
        r_refs, b_refs = refs[:n_r], refs[n_r:n_r + n_b]
        o_refs, a_refs = refs[n_r + n_b:n_r + n_b + n_o], refs[n_r + n_b + n_o:]
        res = fn(*[r[...] for r in r_refs], *[b[...] for b in b_refs])
        for o, r in zip(o_refs, res[:n_o]):
            o[...] = r.astype(o.dtype)
        if n_a:
            @pl.when(pl.program_id(0) == 0)
            def _():
                for acc in a_refs:
                    acc[...] = jnp.zeros_like(acc)

            for acc, r in zip(a_refs, res[n_o:]):
                acc[...] += r

    in_specs = [pl.BlockSpec((tr, w), functools.partial(lambda i, cb: (i, cb), cb=cb)) for _, w, cb in rows]
    in_specs += [pl.BlockSpec(b.shape, lambda i: (0, 0)) for b in bcast]
    out_specs = [pl.BlockSpec((tr, w), lambda i: (i, 0)) for w, _ in outs]
    out_specs += [pl.BlockSpec(s, lambda i: (0, 0)) for s in accs]
    out_shape = [jax.ShapeDtypeStruct((n_rows, w), d) for w, d in outs]
    out_shape += [jax.ShapeDtypeStruct(s, F32) for s in accs]
    return pl.pallas_call(
        body, name=name, grid=(n_rows // tr,), in_specs=in_specs, out_specs=out_specs, out_shape=out_shape,
        compiler_params=_params("arbitrary"),
    )(*[r for r, _, _ in rows], *bcast)


def _colsum(t):
    return jnp.sum(t, axis=0, keepdims=True)


def _rms_fwd(name, h, g, dtype, tr):
    def fn(x, gain):
        r = lax.rsqrt(jnp.mean(x * x, axis=-1, keepdims=True) + EPS)
        return (x * r * gain,)

    d = h.shape[1]
    return _rowwise(name, fn, n_rows=h.shape[0], tr=tr, rows=[(h, d, 0)], bcast=[g], outs=[(d, dtype)])[0]


def _rms_bwd(name, h, g, dhn, dres, tr):
    def fn(x, dy, dr, gain):
        r = lax.rsqrt(jnp.mean(x * x, axis=-1, keepdims=True) + EPS)
        xhat = x * r
        gd = gain * dy
        dh = r * (gd - xhat * jnp.mean(gd * xhat, axis=-1, keepdims=True)) + dr
        return dh, dh, _colsum(dy * xhat)

    d = h.shape[1]
    return _rowwise(name, fn, n_rows=h.shape[0], tr=tr, rows=[(h, d, 0), (dhn, d, 0), (dres, d, 0)], bcast=[g],
                    outs=[(d, F32), (d, BF16)], accs=[(1, d)])


def _loss_head(name, h, target, g, tr):
    d = h.shape[1]

    def fn(x, tgt, gain):
        r = lax.rsqrt(jnp.mean(x * x, axis=-1, keepdims=True) + EPS)
        xhat = x * r
        err = xhat * gain - tgt
        loss = 0.5 * jnp.sum(jnp.mean(err * err, axis=-1, keepdims=True), axis=0, keepdims=True)
        dy = err / d
        gd = gain * dy
        dh = r * (gd - xhat * jnp.mean(gd * xhat, axis=-1, keepdims=True))
        return dh, dh, jnp.broadcast_to(loss, (1, LANES)), _colsum(dy * xhat)

    return _rowwise(name, fn, n_rows=h.shape[0], tr=tr, rows=[(h, d, 0), (target, d, 0)], bcast=[g],
                    outs=[(d, F32), (d, BF16)], accs=[(1, LANES), (1, d)])


def _shift_rows(h, k, seq):
    if k == 0:
        return h
    rolled = pltpu.roll(h, (-k) % seq, 0)
    t = lax.broadcasted_iota(jnp.int32, h.shape, 0)
    return jnp.where((t + k >= 0) & (t + k < seq), rolled, 0.0)


def _window_halves(h, half, seq):
    right = h
    left = _shift_rows(h, -1, seq)
    step = 1
    while step < half:
        right = right + _shift_rows(right, step, seq)
        left = left + _shift_rows(left, -step, seq)
        step *= 2
    return left + right


def _window_count(shape, half, seq):
    t = lax.broadcasted_iota(jnp.int32, shape, 0)
    return (jnp.minimum(t + half, seq) - jnp.maximum(t - half, 0)).astype(F32)


def _pool_slabs(name, src, out_dtype, *, n_seq, seq, backward):
    d = src.shape[1]
    group = d // N_POOL_GROUPS
    tc = _tile(group, 256)

    def body(s_ref, o_ref):
        g = (pl.program_id(1) * tc) // group
        for gi, w in enumerate(POOL_WINDOWS):
            half = w // 2

            @pl.when(g == gi)
            def _():
                v = s_ref[...].astype(F32)
                cnt = _window_count(v.shape, half, seq)
                if backward:
                    u = v / cnt
                    res = (_window_halves(u, half, seq) - _shift_rows(u, -half, seq) + _shift_rows(u, half, seq)) - v
                else:
                    res = _window_halves(v, half, seq) / cnt - v
                o_ref[...] = res.astype(o_ref.dtype)

    spec = pl.BlockSpec((seq, tc), lambda b, j: (b, j))
    return pl.pallas_call(
        body, name=name, grid=(n_seq, d // tc), in_specs=[spec], out_specs=spec,
        out_shape=jax.ShapeDtypeStruct(src.shape, out_dtype), compiler_params=_params("parallel", "parallel"),
    )(src)


def _split3(x):
    x1 = x.astype(BF16)
    r1 = x - x1.astype(F32)
    x2 = r1.astype(BF16)
    x3 = (r1 - x2.astype(F32)).astype(BF16)
    return x1, x2, x3


def _tri_dot(tri, x):
    x1, x2, x3 = _split3(x)
    return _dot(tri, x1, NN) + _dot(tri, x2, NN) + _dot(tri, x3, NN)


def _chunk_terms(q_ref, k_ref, g_ref, rev, scale):
    c = q_ref.shape[0]
    ri = lax.broadcasted_iota(jnp.int32, (c, c), 0)
    ci = lax.broadcasted_iota(jnp.int32, (c, c), 1)
    causal = (ci >= ri) if rev else (ci <= ri)
    mid = c // 2 if rev else c // 2 - 1
    last = 0 if rev else c - 1
    b = _tri_dot(causal.astype(BF16), g_ref[...])
    b_mid, b_last = b[mid:mid + 1, :], b[last:last + 1, :]
    e_qm, e_km, e_qi, e_ks = jnp.exp(b - b_mid), jnp.exp(b_mid - b), jnp.exp(b), jnp.exp(b_last - b)
    q = q_ref[...].astype(F32) * scale
    k = k_ref[...].astype(F32)
    return dict(causal=causal, mid=mid, last=last, decay=jnp.exp(b_last), e_qm=e_qm, e_km=e_km, e_qi=e_qi, e_ks=e_ks,
                qm=q * e_qm, km=k * e_km, qi=q * e_qi, ks=k * e_ks)


HEADS_PER_STEP = 2


def _gla_specs(n_chunks, heads, dk, dv, key_dim, rev, backward):
    hps = HEADS_PER_STEP

    def chunk(n):
        scan_pos = (n_chunks - 1 - n) if backward else n
        return (n_chunks - 1 - scan_pos) if rev else scan_pos

    def at(width, col0):
        return pl.BlockSpec((CHUNK, hps * width), lambda b, h, n: (b * n_chunks + chunk(n), col0 // (hps * width) + h))

    state = pl.BlockSpec((hps, dv, dk), lambda b, h, n: ((b * n_chunks + chunk(n)) * (heads // hps) + h, 0, 0))
    return at(dk, 0), at(dk, key_dim), at(dv, 2 * key_dim), at(dk, 0), at(dv, 0), state


def _gla_fwd(name, pm, log_a, *, rev, n_seq, seq, key_dim, val_dim):
    heads, n_chunks = GLA_HEADS, seq // CHUNK
    dk, dv = key_dim // heads, val_dim // heads
    scale = dk ** -0.5
    q_spec, k_spec, v_spec, g_spec, o_spec, st_spec = _gla_specs(n_chunks, heads, dk, dv, key_dim, rev, False)

    def body(q_ref, k_ref, v_ref, g_ref, o_ref, st_ref, state):
        @pl.when(pl.program_id(2) == 0)
        def _():
            state[...] = jnp.zeros_like(state)

        for i in range(HEADS_PER_STEP):
            kc, vc = pl.ds(i * dk, dk), pl.ds(i * dv, dv)
            t = _chunk_terms(q_ref.at[:, kc], k_ref.at[:, kc], g_ref.at[:, kc], rev, scale)
            v = v_ref[:, vc]
            p = jnp.where(t["causal"], _dot(t["qm"].astype(BF16), t["km"].astype(BF16), NT), 0.0)
            entering = state[i].astype(BF16)
            st_ref[i] = entering
            o_ref[:, vc] = _dot(p.astype(BF16), v, NN) + _dot(t["qi"].astype(BF16), entering, NT)
            state[i] = t["decay"] * state[i] + _dot(v, t["ks"].astype(BF16), TN)

    tokens = n_seq * seq
    return pl.pallas_call(
        body, name=name, grid=(n_seq, heads // HEADS_PER_STEP, n_chunks), in_specs=[q_spec, k_spec, v_spec, g_spec],
        out_specs=[o_spec, st_spec],
        out_shape=[jax.ShapeDtypeStruct((tokens, val_dim), F32),
                   jax.ShapeDtypeStruct((n_seq * n_chunks * heads, dv, dk), BF16)],
        scratch_shapes=[pltpu.VMEM((HEADS_PER_STEP, dv, dk), F32)],
        compiler_params=_params("parallel", "parallel", "arbitrary"),
    )(pm, pm, pm, log_a)


def _gla_bwd(name, pm, log_a, states, d_o, *, rev, n_seq, seq, key_dim, val_dim):
    heads, n_chunks = GLA_HEADS, seq // CHUNK
    dk, dv = key_dim // heads, val_dim // heads
    scale = dk ** -0.5
    q_spec, k_spec, v_spec, g_spec, o_spec, st_spec = _gla_specs(n_chunks, heads, dk, dv, key_dim, rev, True)

    def body(q_ref, k_ref, v_ref, g_ref, do_ref, st_ref, dq_ref, dk_ref, dv_ref, dg_ref, dstate):
        @pl.when(pl.program_id(2) == 0)
        def _():
            dstate[...] = jnp.zeros_like(dstate)

        for i in range(HEADS_PER_STEP):
            kc, vc = pl.ds(i * dk, dk), pl.ds(i * dv, dv)
            t = _chunk_terms(q_ref.at[:, kc], k_ref.at[:, kc], g_ref.at[:, kc], rev, scale)
            causal = t["causal"]
            v, d_out, entering = v_ref[:, vc], do_ref[:, vc], st_ref[i]
            qm, km, qi, ks = (t[n].astype(BF16) for n in ("qm", "km", "qi", "ks"))
            dst = dstate[i]
            dst16 = dst.astype(BF16)
            p = jnp.where(causal, _dot(qm, km, NT), 0.0).astype(BF16)
            dp = jnp.where(causal, _dot(d_out, v, NT), 0.0).astype(BF16)
            dv_ref[:, vc] = _dot(p, d_out, TN) + _dot(ks, dst16, NT)
            dqm, dkm = _dot(dp, km, NN), _dot(dp, qm, TN)
            dqi, dks = _dot(d_out, entering, NN), _dot(v, dst16, NN)
            d_decay = _colsum(dst * entering.astype(F32))
            dq_ref[:, kc] = (dqi * t["e_qi"] + dqm * t["e_qm"]) * scale
            dk_ref[:, kc] = dkm * t["e_km"] + dks * t["e_ks"]
            qm_term, km_term = dqm * t["qm"], dkm * t["km"]
            ks_term = dks * t["ks"]
            db = dqi * t["qi"] + qm_term - km_term - ks_term
            row = lax.broadcasted_iota(jnp.int32, db.shape, 0)
            db = db + jnp.where(row == t["mid"], _colsum(km_term - qm_term), 0.0)
            db = db + jnp.where(row == t["last"], _colsum(ks_term) + d_decay * t["decay"], 0.0)
            ri = lax.broadcasted_iota(jnp.int32, causal.shape, 0)
            ci = lax.broadcasted_iota(jnp.int32, causal.shape, 1)
            anti = (ci <= ri) if rev else (ci >= ri)
            dg_ref[:, kc] = _tri_dot(anti.astype(BF16), db)
            dstate[i] = t["decay"] * dst + _dot(d_out, qi, TN)

    tokens = n_seq * seq
    kd = jax.ShapeDtypeStruct((tokens, key_dim), F32)
    return pl.pallas_call(
        body, name=name, grid=(n_seq, heads // HEADS_PER_STEP, n_chunks),
        in_specs=[q_spec, k_spec, v_spec, g_spec, o_spec, st_spec], out_specs=[g_spec, g_spec, o_spec, g_spec],
        out_shape=[kd, kd, jax.ShapeDtypeStruct((tokens, val_dim), F32), kd],
        scratch_shapes=[pltpu.VMEM((HEADS_PER_STEP, dv, dk), F32)],
        compiler_params=_params("parallel", "parallel", "arbitrary"),
    )(pm, pm, pm, log_a, d_o, states)


def _sigmoid_neg(z):
    e = jnp.exp(-jnp.abs(z))
    return jnp.where(z >= 0, e / (1.0 + e), 1.0 / (1.0 + e))


def _gate_fwd(name, pr, wf, bf, wb, bb, tr):
    kd = wf.shape[1]

    def fn(r, w_f, b_f, w_b, b_b):
        r16 = r.astype(BF16)
        out = []
        for w, b in ((w_f, b_f), (w_b, b_b)):
            z = _dot(r16, w.astype(BF16), NN) + b
            out.append((jnp.minimum(z, 0.0) - jnp.log1p(jnp.exp(-jnp.abs(z)))) / GATE_TAU)
        return tuple(out)

    return _rowwise(name, fn, n_rows=pr.shape[0], tr=tr, rows=[(pr, LANES, 0)], bcast=[wf, bf, wb, bb],
                    outs=[(kd, F32), (kd, F32)])


def _gate_bwd(name, pr, wf, bf, wb, bb, dgf, dgb, tr):
    kd = wf.shape[1]

    def fn(r, d_f, d_b, w_f, b_f, w_b, b_b):
        r16 = r.astype(BF16)
        dpr = jnp.zeros(r.shape, F32)
        sums = []
        for w, b, dg in ((w_f, b_f, d_f), (w_b, b_b, d_b)):
            w16 = w.astype(BF16)
            z = _dot(r16, w16, NN) + b
            dz = dg / GATE_TAU * _sigmoid_neg(z)
            dz16 = dz.astype(BF16)
            dpr = dpr + _dot(dz16, w16, NT)
            sums += [_dot(r16, dz16, TN), _colsum(dz)]
        return (dpr, *sums)

    return _rowwise(name, fn, n_rows=pr.shape[0], tr=tr, rows=[(pr, LANES, 0), (dgf, kd, 0), (dgb, kd, 0)],
                    bcast=[wf, bf, wb, bb], outs=[(LANES, BF16)], accs=[(LANES, kd), (1, kd), (LANES, kd), (1, kd)])


def _head_norm_fwd(name, of, ob, pm, gn, *, val_dim, gate_block, tr):
    dv = val_dim // GLA_HEADS

    def fn(o_f, o_b, gate, gain):
        out = []
        for h in range(GLA_HEADS):
            sl = slice(h * dv, (h + 1) * dv)
            o = o_f[:, sl] + o_b[:, sl]
            gt = gate[:, sl].astype(F32)
            on = o * lax.rsqrt(jnp.mean(o * o, axis=-1, keepdims=True) + EPS) * gain
            out.append(on * (gt * jax.nn.sigmoid(gt)))
        return (jnp.concatenate(out, axis=1),)

    return _rowwise(name, fn, n_rows=of.shape[0], tr=tr, rows=[(of, val_dim, 0), (ob, val_dim, 0), (pm, val_dim, gate_block)],
                    bcast=[gn], outs=[(val_dim, BF16)])[0]


def _head_norm_bwd(name, of, ob, pm, gn, dz, *, val_dim, gate_block, tr):
    dv = val_dim // GLA_HEADS

    def fn(o_f, o_b, gate, d_z, gain):
        d_o, d_gate = [], []
        d_gain = jnp.zeros((1, dv), F32)
        for h in range(GLA_HEADS):
            sl = slice(h * dv, (h + 1) * dv)
            o = o_f[:, sl] + o_b[:, sl]
            gt = gate[:, sl].astype(F32)
            dzh = d_z[:, sl]
            r = lax.rsqrt(jnp.mean(o * o, axis=-1, keepdims=True) + EPS)
            ohat = o * r
            sg = jax.nn.sigmoid(gt)
            d_gate.append(dzh * (ohat * gain) * (sg * (1.0 + gt * (1.0 - sg))))
            don = dzh * (gt * sg)
            d_gain = d_gain + _colsum(don * ohat)
            gd = gain * don
            d_o.append(r * (gd - ohat * jnp.mean(gd * ohat, axis=-1, keepdims=True)))
        return jnp.concatenate(d_o, axis=1), jnp.concatenate(d_gate, axis=1), d_gain

    return _rowwise(name, fn, n_rows=of.shape[0], tr=tr,
                    rows=[(of, val_dim, 0), (ob, val_dim, 0), (pm, val_dim, gate_block), (dz, val_dim, 0)], bcast=[gn],
                    outs=[(val_dim, BF16), (val_dim, BF16)], accs=[(1, dv)])


def _proj_grad(name, dqf, dqb, dkf, dkb, dvf, dvb, dgate, *, key_dim, val_dim, tr):
    def fn(q1, q2, k1, k2, v1, v2, gt):
        return (jnp.concatenate([(q1 + q2).astype(BF16), (k1 + k2).astype(BF16), (v1 + v2).astype(BF16), gt], axis=1),)

    rows = [(dqf, key_dim, 0), (dqb, key_dim, 0), (dkf, key_dim, 0), (dkb, key_dim, 0), (dvf, val_dim, 0),
            (dvb, val_dim, 0), (dgate, val_dim, 0)]
    return _rowwise(name, fn, n_rows=dqf.shape[0], tr=tr, rows=rows, outs=[(2 * key_dim + 2 * val_dim, BF16)])[0]


def _pool_out_grad(name, dh, y_pre, scale, tr):
    def fn(d, y, s):
        return d * s, _colsum(d * y)

    dm = dh.shape[1]
    return _rowwise(name, fn, n_rows=dh.shape[0], tr=tr, rows=[(dh, dm, 0), (y_pre, dm, 0)], bcast=[scale],
                    outs=[(dm, BF16)], accs=[(1, dm)])


def _mesh_place():
    x, y, c = lax.axis_index("x"), lax.axis_index("y"), lax.axis_index("c")
    return x, y, c, 4 * x + 2 * y + c


def _peer(x, y, c, p):
    px, py, pc = (x if not p & 4 else 1 - x), (y if not p & 2 else 1 - y), (c if not p & 1 else 1 - c)
    return (px, py, pc), 4 * px + 2 * py + pc


CHIP_BITS = (2, 4, 6)
N_CHIP = N_DEV // 2
GATHER_COPIES = 8
FLIP_C, FLIP_Y, FLIP_X = 1, 2, 4


def _transfer_sems(n_gather, n_scatter, n_pair):
    n_remote = n_gather * GATHER_COPIES + n_scatter * len(CHIP_BITS) + n_pair * N_CHIP
    return [pltpu.SemaphoreType.DMA((n_remote,)), pltpu.SemaphoreType.DMA((n_remote,)),
            pltpu.SemaphoreType.DMA((max(n_gather, 1),))]


def _transfers(g_in, g_out, s_in, s_out, p_in, p_out, send, recv, loc):
    x, y, c, me = _mesh_place()
    sibling, _ = _peer(x, y, c, FLIP_C)

    def remote(src, dst, k, peer):
        return pltpu.make_async_remote_copy(src_ref=src, dst_ref=dst, send_sem=send.at[k], recv_sem=recv.at[k],
                                            device_id=peer, device_id_type=MESH_ID)

    local, first, arrivals = [], [], []
    passed, passed_arrivals, last, last_arrivals = [], [], [], []
    k = 0
    for src, out in zip(g_in, g_out):
        half = src.shape[0] // 2
        halves = (pl.ds(0, half), pl.ds(half, half))
        x_chip, x_block = _peer(x, y, c, FLIP_X)
        y_chip, y_block = _peer(x, y, c, FLIP_Y)
        _, far_block = _peer(x, y, c, FLIP_X | FLIP_Y)
        local.append(pltpu.make_async_copy(src, out.at[me], loc.at[len(local)]))
        for peer, block in ((sibling, me ^ FLIP_C), (x_chip, x_block), (y_chip, y_block)):
            first.append(remote(src, out.at[me], k, peer))
            arrivals.append(remote(src, out.at[block], k, peer))
            k += 1
        for peer, block, part in ((x_chip, y_block, halves[0]), (y_chip, x_block, halves[1])):
            passed.append(remote(out.at[block, part], out.at[block, part], k, peer))
            passed_arrivals.append(remote(out.at[far_block, part], out.at[far_block, part], k, peer))
            k += 1
        for block in (x_block, y_block):
            passed.append(remote(out.at[block], out.at[block], k, sibling))
            passed_arrivals.append(remote(out.at[block ^ FLIP_C], out.at[block ^ FLIP_C], k, sibling))
            k += 1
        last.append(remote(out.at[far_block], out.at[far_block], k, sibling))
        last_arrivals.append(remote(out.at[far_block ^ FLIP_C], out.at[far_block ^ FLIP_C], k, sibling))
        k += 1
    n_gather_first = len(first)
    for src, out in zip(s_in, s_out):
        for j, p in enumerate(CHIP_BITS):
            peer, _ = _peer(x, y, c, p)
            peer_chip = 2 * peer[0] + peer[1]
            first.append(remote(src.at[peer_chip], out.at[j], k, peer))
            arrivals.append(remote(src.at[peer_chip], out.at[j], k, peer))
            k += 1
    for src, out in zip(p_in, p_out):
        for q in range(N_CHIP):
            first.append(remote(src.at[2 * q + 1 - c], out.at[q], k, sibling))
            arrivals.append(remote(src.at[2 * q + 1 - c], out.at[q], k, sibling))
            k += 1

    def start():
        for cp in local + first:
            cp.start()

    def middle():
        for cp in arrivals[:n_gather_first]:
            cp.wait_recv()
        for cp in passed:
            cp.start()

    def finish():
        for cp in passed_arrivals:
            cp.wait_recv()
        for cp in last:
            cp.start()
        for cp in last_arrivals + arrivals[n_gather_first:]:
            cp.wait_recv()
        for cp in first + passed + last:
            cp.wait_send()
        for cp in local:
            cp.wait()

    return start, middle, finish


def _exchange(name, gathers=(), scatters=(), pairs=()):
    n_g, n_s, n_p = len(gathers), len(scatters), len(pairs)
    n_in = n_g + n_s + n_p

    def body(*refs):
        ins, outs = refs[:n_in], refs[n_in:-3]
        start, middle, finish = _transfers(ins[:n_g], outs[:n_g], ins[n_g:n_g + n_s], outs[n_g:n_g + n_s], ins[n_g + n_s:],
                                           outs[n_g + n_s:], *refs[-3:])
        start()
        middle()
        finish()

    hbm = pl.BlockSpec(memory_space=pl.ANY)
    out_shape = [jax.ShapeDtypeStruct((N_DEV,) + g.shape, g.dtype) for g in gathers]
    out_shape += [jax.ShapeDtypeStruct((len(CHIP_BITS),) + t.shape[1:], t.dtype) for t in scatters]
    out_shape += [jax.ShapeDtypeStruct((N_CHIP,) + t.shape[1:], t.dtype) for t in pairs]
    return pl.pallas_call(
        body, name=name, in_specs=[hbm] * n_in, out_specs=[hbm] * len(out_shape), out_shape=out_shape,
        scratch_shapes=_transfer_sems(n_g, n_s, n_p),
    )(*gathers, *scatters, *pairs)


def _place_scalar(axis_value):
    return jnp.asarray(axis_value, jnp.int32).reshape(1)


def _chip_sums(name, grad):
    received = _exchange(name + "_pair", pairs=[grad])[0]
    cols = grad.shape[-1]
    rows = received.size // cols // N_CHIP
    tr = _row_tile(rows, (512 * 1024) // cols)

    def body(core_ref, mine_ref, got_ref, out_ref):
        out_ref[...] = (mine_ref[...].astype(F32) + got_ref[...].astype(F32)).astype(out_ref.dtype)

    block = (None, tr, cols)
    out = pl.pallas_call(
        body, name=name + "_add", out_shape=jax.ShapeDtypeStruct((N_CHIP, rows, cols), grad.dtype),
        grid_spec=pltpu.PrefetchScalarGridSpec(
            num_scalar_prefetch=1, grid=(N_CHIP, rows // tr),
            in_specs=[pl.BlockSpec(block, lambda q, i, core: (2 * q + core[0], i, 0)),
                      pl.BlockSpec(block, lambda q, i, core: (q, i, 0))],
            out_specs=pl.BlockSpec(block, lambda q, i, core: (q, i, 0))),
        compiler_params=_params("parallel", "parallel"),
    )(_place_scalar(lax.axis_index("c")), grad.reshape(N_DEV, rows, cols), received.reshape(N_CHIP, rows, cols))
    return out.reshape(received.shape)


def _all_reduce_small(name, pack):
    def body(in_ref, out_ref, parts, send_sems, recv_sems):
        x, y, c, me = _mesh_place()
        parts[me] = in_ref[...]

        def copy(p, arriving):
            peer, peer_block = _peer(x, y, c, p)
            return pltpu.make_async_remote_copy(src_ref=in_ref, dst_ref=parts.at[peer_block if arriving else me],
                                                send_sem=send_sems.at[p - 1], recv_sem=recv_sems.at[p - 1],
                                                device_id=peer, device_id_type=MESH_ID)

        sends = [copy(p, False) for p in range(1, N_DEV)]
        for cp in sends:
            cp.start()
        for p in range(1, N_DEV):
            copy(p, True).wait_recv()
        for cp in sends:
            cp.wait_send()
        total = parts[0]
        for j in range(1, N_DEV):
            total = total + parts[j]
        out_ref[...] = total

    vmem = pl.BlockSpec(memory_space=pltpu.VMEM)
    return pl.pallas_call(
        body, name=name, in_specs=[vmem], out_specs=vmem, out_shape=jax.ShapeDtypeStruct(pack.shape, F32),
        scratch_shapes=[pltpu.VMEM((N_DEV,) + pack.shape, F32), pltpu.SemaphoreType.DMA((N_PEER,)),
                        pltpu.SemaphoreType.DMA((N_PEER,))],
        compiler_params=pltpu.CompilerParams(vmem_limit_bytes=VMEM_LIMIT_BYTES),
    )(pack)


def _adamw_math(w, g, m, v):
    m = ADAM_B1 * m + (1.0 - ADAM_B1) * g
    v = ADAM_B2 * v + (1.0 - ADAM_B2) * jnp.square(g)
    m_hat = m / (1.0 - ADAM_B1 ** ADAM_STEP)
    v_hat = v / (1.0 - ADAM_B2 ** ADAM_STEP)
    delta = -ADAM_LR * (m_hat / (jnp.sqrt(v_hat) + ADAM_EPS) + ADAM_WD * w)
    return delta, m, v


def _adamw_shard(name, w, m, v, grad_parts, layer=0, into=None):
    landed, sums = grad_parts
    shape = w.shape
    n_layers, cols = shape[0], shape[-1]
    rows = w.size // cols // n_layers
    tr = _row_tile(rows, (256 * 1024) // cols)
    n_into = 0 if into is None else 4
    n_landed = landed.shape[0]

    def body(chip_ref, w_ref, m_ref, v_ref, mine_ref, got_ref, *refs):
        g_ref, d_ref, nm_ref, nv_ref = refs[n_into:]
        g = mine_ref[...].astype(F32)
        for j in range(n_landed):
            g = g + got_ref[j].astype(F32)
        delta, new_m, new_v = _adamw_math(w_ref[...], g, m_ref[...], v_ref[...])
        g_ref[...], d_ref[...], nm_ref[...], nv_ref[...] = g, delta, new_m, new_v

    if n_layers == 1:
        view = (rows, cols)
        spec = pl.BlockSpec((tr, cols), lambda i, chip: (i, 0))
    else:
        view = (n_layers, rows, cols)
        spec = pl.BlockSpec((None, tr, cols), lambda i, chip: (layer, i, 0))
    operands = [w.reshape(view), m.reshape(view), v.reshape(view), sums.reshape(N_CHIP, rows, cols),
                landed.reshape(n_landed, rows, cols)]
    in_specs = [spec, spec, spec, pl.BlockSpec((None, tr, cols), lambda i, chip: (chip[0], i, 0)),
                pl.BlockSpec((n_landed, tr, cols), lambda i, chip: (0, i, 0))]
    if into is not None:
        operands += [t.reshape(view) for t in into]
        in_specs += [pl.BlockSpec(memory_space=pl.ANY)] * 4
    first_into = 1 + 5
    res = pl.pallas_call(
        body, name=name, out_shape=[jax.ShapeDtypeStruct(view, F32)] * 4,
        grid_spec=pltpu.PrefetchScalarGridSpec(num_scalar_prefetch=1, grid=(rows // tr,), in_specs=in_specs, out_specs=[spec] * 4),
        input_output_aliases={first_into + t: t for t in range(n_into)}, compiler_params=_params("parallel"),
    )(_place_scalar(2 * lax.axis_index("x") + lax.axis_index("y")), *operands)
    return [r.reshape(shape) for r in res]


def _adamw_small(name, w, m, v, g):
    shape = w.shape
    two_d = (1, shape[0]) if len(shape) == 1 else (w.size // shape[-1], shape[-1])

    def body(w_ref, m_ref, v_ref, g_ref, d_ref, nm_ref, nv_ref):
        d_ref[...], nm_ref[...], nv_ref[...] = _adamw_math(w_ref[...], g_ref[...], m_ref[...], v_ref[...])

    res = pl.pallas_call(body, name=name, out_shape=[jax.ShapeDtypeStruct(two_d, F32)] * 3)(
        w.reshape(two_d), m.reshape(two_d), v.reshape(two_d), g.reshape(two_d))
    return [r.reshape(shape) for r in res]


def _mlp_fwd(tag, h_in, gain, w1, w2_mine, down_gathers, tm, tr):
    tokens, d = h_in.shape
    fs = w1.shape[2]
    f = N_DEV * fs
    hn = _rms_fwd(f"rms_mlp{tag}", h_in, gain, BF16, tr)
    tn, tk = _tile(fs, 1024), _tile(d, TILE_K)
    per = fs // tn

    def relu2(acc):
        a = jnp.maximum(acc, 0.0)
        return acc, a * a

    a_pre, act, w2 = _mm(
        f"mlp_up{tag}", hn, w1, grid=(tokens // tm, f // tn, d // tk), a_spec=_spec2(tm, tk, lambda m, n, k: (m, k)),
        b_spec=pl.BlockSpec((None, tk, tn), lambda m, n, k: (n // per, k, n % per)),
        o_spec=_spec2(tm, tn, lambda m, n, k: (m, n)), acc_shape=(tm, tn), outs=[((tokens, f), BF16)] * 2, dims=NN, epi=relu2,
        gathers=[w2_mine])
    tn2, tk2 = _tile(d, 1024), _tile(fs, TILE_K)
    kper = fs // tk2
    h_out, *got = _mm(
        f"mlp_down{tag}", act, w2, grid=(tokens // tm, d // tn2, f // tk2), a_spec=_spec2(tm, tk2, lambda m, n, k: (m, k)),
        b_spec=pl.BlockSpec((None, tk2, tn2), lambda m, n, k: (k // kper, k % kper, n)),
        o_spec=_spec2(tm, tn2, lambda m, n, k: (m, n)), acc_shape=(tm, tn2), outs=[((tokens, d), F32)], dims=NN,
        extras=[(h_in, _spec2(tm, tn2, lambda m, n, k: (m, n)))], epi=lambda acc, res: (res + acc,), gathers=down_gathers)
    return hn, a_pre, act, h_out, w2, got


def _mlp_bwd(tag, h_in, gain, hn, a_pre, act, dh, dh16, w1, w2, tm, tr):
    tokens, d = h_in.shape
    fs = w1.shape[2]
    f = N_DEV * fs
    tn, tk = _tile(fs, 1024), _tile(d, TILE_K)
    nper = fs // tn
    da = _mm(
        f"mlp_dact{tag}", dh16, w2, grid=(tokens // tm, f // tn, d // tk), a_spec=_spec2(tm, tk, lambda m, n, k: (m, k)),
        b_spec=pl.BlockSpec((None, tn, tk), lambda m, n, k: (n // nper, n % nper, k)),
        o_spec=_spec2(tm, tn, lambda m, n, k: (m, n)), acc_shape=(tm, tn), outs=[((tokens, f), BF16)], dims=NT,
        extras=[(a_pre, _spec2(tm, tn, lambda m, n, k: (m, n)))],
        epi=lambda acc, pre: (acc * (2.0 * jnp.maximum(pre.astype(F32), 0.0)),))[0]
    tkt = _tile(tokens, TILE_K)
    tm2, tn2 = _tile(fs, 1024), _tile(d, 1024)
    mper = fs // tm2
    dw2 = _mm(
        f"mlp_dw_out{tag}", act, dh16, grid=(f // tm2, d // tn2, tokens // tkt), a_spec=_spec2(tkt, tm2, lambda m, n, k: (k, m)),
        b_spec=_spec2(tkt, tn2, lambda m, n, k: (k, n)),
        o_spec=pl.BlockSpec((None, tm2, tn2), lambda m, n, k: (m // mper, m % mper, n)), acc_shape=(tm2, tn2),
        outs=[((N_DEV, fs, d), BF16)], dims=TN)[0]
    dw2_sums = _chip_sums(f"mlp_dw_out{tag}", dw2)
    tm3 = _tile(d, 1024)
    dw1, dw2_landed = _mm(
        f"mlp_dw_in{tag}", hn, da, grid=(d // tm3, f // tn, tokens // tkt), a_spec=_spec2(tkt, tm3, lambda m, n, k: (k, m)),
        b_spec=_spec2(tkt, tn, lambda m, n, k: (k, n)),
        o_spec=pl.BlockSpec((None, tm3, tn), lambda m, n, k: (n // nper, m, n % nper)), acc_shape=(tm3, tn),
        outs=[((N_DEV, d, fs), BF16)], dims=TN, scatters=[dw2_sums])
    dw1_sums = _chip_sums(f"mlp_dw_in{tag}", dw1)
    tn4, tk4 = _tile(d, 1024), _tile(fs, TILE_K)
    kper = fs // tk4
    dhn, dw1_landed = _mm(
        f"mlp_dx{tag}", da, w1, grid=(tokens // tm, d // tn4, f // tk4), a_spec=_spec2(tm, tk4, lambda m, n, k: (m, k)),
        b_spec=pl.BlockSpec((None, tn4, tk4), lambda m, n, k: (k // kper, n, k % kper)),
        o_spec=_spec2(tm, tn4, lambda m, n, k: (m, n)), acc_shape=(tm, tn4), outs=[((tokens, d), F32)], dims=NT,
        scatters=[dw1_sums])
    dh_in, dh_in16, dgain = _rms_bwd(f"rms_mlp_bwd{tag}", h_in, gain, dhn, dh, tr)
    return dh_in, dh_in16, dgain, (dw1_landed, dw1_sums), (dw2_landed, dw2_sums)


def _plain_mm(name, a, b, dims, out_dtype, *, tm, tn_pref=1024, tk_pref=TILE_K, extras=None, epi=None, gathers=(), scatters=()):
    if dims == NN:
        (m, k), n = a.shape, b.shape[1]
    elif dims == NT:
        (m, k), n = a.shape, b.shape[0]
    else:
        (k, m), n = a.shape, b.shape[1]
    tm, tn, tk = _tile(m, tm), _tile(n, tn_pref), _tile(k, tk_pref)
    a_spec = _spec2(tk, tm, lambda i, j, l: (l, i)) if dims == TN else _spec2(tm, tk, lambda i, j, l: (i, l))
    b_spec = _spec2(tn, tk, lambda i, j, l: (j, l)) if dims == NT else _spec2(tk, tn, lambda i, j, l: (l, j))
    o_spec = _spec2(tm, tn, lambda i, j, l: (i, j))
    ex = [(e, o_spec) for e in (extras or [])]
    res = _mm(name, a, b, grid=(m // tm, n // tn, k // tk), a_spec=a_spec, b_spec=b_spec, o_spec=o_spec, acc_shape=(tm, tn),
              outs=[((m, n), out_dtype)], dims=dims, extras=ex, epi=epi, gathers=gathers, scatters=scatters)
    return res if (gathers or scatters) else res[0]


def kernel(x, norm_mix, norm_mlp, norm_final, pool_w, pool_scale, gla_w_in, gla_w_up_f, gla_b_up_f, gla_w_up_b, gla_b_up_b, gla_g_norm, gla_w_out, mlp_w_in, mlp_w_out, loss_target, m_norm_mix, m_norm_mlp, m_norm_final, m_pool_w, m_pool_scale, m_gla_w_in, m_gla_w_up_f, m_gla_b_up_f, m_gla_w_up_b, m_gla_b_up_b, m_gla_g_norm, m_gla_w_out, m_mlp_w_in, m_mlp_w_out, v_norm_mix, v_norm_mlp, v_norm_final, v_pool_w, v_pool_scale, v_gla_w_in, v_gla_w_up_f, v_gla_b_up_f, v_gla_w_up_b, v_gla_b_up_b, v_gla_g_norm, v_gla_w_out, v_mlp_w_in, v_mlp_w_out):
    n_seq, seq, d = x.shape
    tokens = n_seq * seq
    group = d // N_POOL_GROUPS
    key_dim, val_dim = d // 2, d
    dv = val_dim // GLA_HEADS
    n_main = 2 * key_dim + 2 * val_dim
    proj_shard = gla_w_in.shape[2]
    proj_width = N_DEV * proj_shard
    kd_shard = key_dim // N_DEV
    assert proj_width == n_main + 2 * GATE_RANK and seq % CHUNK == 0 and 2 * GATE_RANK <= LANES
    assert mlp_w_in.shape[0] == 2 and pool_w.shape[0] == 1 and gla_w_in.shape[0] == 1
    me = 4 * lax.axis_index("x") + 2 * lax.axis_index("y") + lax.axis_index("c")
    tm = _tile(tokens, 1024)
    tr = _tile(tokens, 128)
    x2, target = x.reshape(tokens, d), loss_target.reshape(tokens, d)

    small_rows = 2 * GATE_RANK + 4
    small = jnp.concatenate([
        gla_w_up_f[0], gla_w_up_b[0], gla_b_up_f, gla_b_up_b,
        jnp.pad(gla_g_norm, ((0, 0), (0, kd_shard - gla_g_norm.shape[1]))), jnp.zeros((1, kd_shard), F32)], axis=0)
    w1_mine = [mlp_w_in[l].astype(BF16) for l in range(2)]
    w2_mine = [mlp_w_out[l].astype(BF16) for l in range(2)]
    pool_g, small_g, w1_0 = _exchange(
        "gather_first", gathers=[pool_w[0].astype(BF16), small.reshape(2, small_rows // 2, kd_shard), w1_mine[0]])
    small_g = small_g.reshape(N_DEV, small_rows, kd_shard)
    w_pool = pool_g.transpose(1, 0, 2, 3).reshape(N_POOL_GROUPS, group, group)
    small_full = small_g.transpose(1, 0, 2).reshape(small_rows, key_dim)
    pad_rows = lambda w, r0: jnp.pad(w, ((r0, LANES - r0 - GATE_RANK), (0, 0)))
    w_up_f, w_up_b = pad_rows(small_full[:GATE_RANK], 0), pad_rows(small_full[GATE_RANK:2 * GATE_RANK], GATE_RANK)
    b_up_f, b_up_b = small_full[2 * GATE_RANK:2 * GATE_RANK + 1], small_full[2 * GATE_RANK + 1:2 * GATE_RANK + 2]
    g_norm = small_g[:, 2 * GATE_RANK + 2, :gla_g_norm.shape[1]].reshape(1, dv)

    hn0 = _rms_fwd("rms_mix0", x2, norm_mix[0:1], F32, tr)
    pooled = _pool_slabs("pool_fwd", hn0, BF16, n_seq=n_seq, seq=seq, backward=False)
    tn, tk = _tile(group, 1024), _tile(group, TILE_K)
    nper, kper = group // tn, group // tk
    y_pre, h1 = _mm(
        "pool_mix", pooled, w_pool, grid=(tokens // tm, d // tn, group // tk),
        a_spec=_spec2(tm, tk, lambda m, n, k: (m, (n // nper) * kper + k)),
        b_spec=pl.BlockSpec((None, tk, tn), lambda m, n, k: (n // nper, k, n % nper)),
        o_spec=_spec2(tm, tn, lambda m, n, k: (m, n)), acc_shape=(tm, tn), outs=[((tokens, d), F32)] * 2, dims=NN,
        extras=[(x2, _spec2(tm, tn, lambda m, n, k: (m, n))), (pool_scale, _spec2(1, tn, lambda m, n, k: (0, n)))],
        epi=lambda acc, res, s: (acc, res + acc * s))
    hn1, a_pre0, act0, h2, w2_0, (win_g, wout_g) = _mlp_fwd(
        "0", h1, norm_mlp[0:1], w1_0, w2_mine[0], [gla_w_in[0].astype(BF16), gla_w_out[0].astype(BF16)], tm, tr)
    w_in = win_g.transpose(1, 0, 2).reshape(d, proj_width)
    w_main = w_in[:, :n_main]
    w_r = jnp.pad(w_in[:, n_main:], ((0, 0), (0, LANES - 2 * GATE_RANK)))
    w_out = wout_g.reshape(val_dim, d)

    hn2 = _rms_fwd("rms_mix1", h2, norm_mix[1:2], BF16, tr)
    pm, w1_1 = _plain_mm("gla_proj", hn2, w_main, NN, BF16, tm=tm, gathers=[w1_mine[1]])
    pr = _plain_mm("gla_proj_gate", hn2, w_r, NN, F32, tm=tm)
    log_a_f, log_a_b = _gate_fwd("gla_gate", pr, w_up_f, b_up_f, w_up_b, b_up_b, tr)
    dims = dict(n_seq=n_seq, seq=seq, key_dim=key_dim, val_dim=val_dim)
    o_f, st_f = _gla_fwd("gla_scan_f", pm, log_a_f, rev=False, **dims)
    o_b, st_b = _gla_fwd("gla_scan_b", pm, log_a_b, rev=True, **dims)
    gate_block = (2 * key_dim + val_dim) // val_dim
    z = _head_norm_fwd("gla_head_norm", o_f, o_b, pm, g_norm, val_dim=val_dim, gate_block=gate_block, tr=tr)
    h3 = _plain_mm("gla_out", z, w_out, NN, F32, tm=tm, extras=[h2], epi=lambda acc, res: (res + acc,))
    hn3, a_pre1, act1, h4, w2_1, _ = _mlp_fwd("1", h3, norm_mlp[1:2], w1_1, w2_mine[1], [], tm, tr)

    dh4, dh4_16, loss_row, d_norm_final = _loss_head("loss_head", h4, target, norm_final.reshape(1, d), tr)

    dh3, dh3_16, d_norm_mlp1, dw1_1, dw2_1 = _mlp_bwd("1", h3, norm_mlp[1:2], hn3, a_pre1, act1, dh4, dh4_16, w1_1, w2_1, tm, tr)
    dz = _plain_mm("gla_out_dx", dh3_16, w_out, NT, F32, tm=tm)
    dw_out = _plain_mm("gla_out_dw", z, dh3_16, TN, BF16, tm=1024)
    dw_out_sums = _chip_sums("gla_out_dw", dw_out.reshape(N_DEV, val_dim // N_DEV, d))
    d_o, d_gate, d_g_norm = _head_norm_bwd("gla_head_norm_bwd", o_f, o_b, pm, g_norm, dz, val_dim=val_dim,
                                           gate_block=gate_block, tr=tr)
    dq_f, dk_f, dv_f, dla_f = _gla_bwd("gla_scan_bwd_f", pm, log_a_f, st_f, d_o, rev=False, **dims)
    dq_b, dk_b, dv_b, dla_b = _gla_bwd("gla_scan_bwd_b", pm, log_a_b, st_b, d_o, rev=True, **dims)
    dpr, dw_up_f, db_up_f, dw_up_b, db_up_b = _gate_bwd("gla_gate_bwd", pr, w_up_f, b_up_f, w_up_b, b_up_b, dla_f, dla_b, tr)
    dpm = _proj_grad("gla_proj_grad", dq_f, dq_b, dk_f, dk_b, dv_f, dv_b, d_gate, key_dim=key_dim, val_dim=val_dim, tr=tr)
    dw_main, dw_out_landed = _plain_mm("gla_proj_dw", hn2, dpm, TN, BF16, tm=1024, scatters=[dw_out_sums])
    dw_r = _plain_mm("gla_proj_gate_dw", hn2, dpr, TN, BF16, tm=1024)
    dw_in = jnp.concatenate([dw_main, dw_r[:, :2 * GATE_RANK]], axis=1)
    dw_in_sums = _chip_sums("gla_proj_dw", dw_in.reshape(d, N_DEV, proj_shard).transpose(1, 0, 2))
    dhn2, dw_in_landed = _plain_mm("gla_proj_dx", dpm, w_main, NT, F32, tm=tm, scatters=[dw_in_sums])
    dhn2 = _plain_mm("gla_proj_gate_dx", dpr, w_r, NT, F32, tm=tm, extras=[dhn2], epi=lambda acc, res: (res + acc,))
    dh2, dh2_16, d_norm_mix1 = _rms_bwd("rms_mix1_bwd", h2, norm_mix[1:2], dhn2, dh3, tr)

    dh1, _, d_norm_mlp0, dw1_0, dw2_0 = _mlp_bwd("0", h1, norm_mlp[0:1], hn1, a_pre0, act0, dh2, dh2_16, w1_0, w2_0, tm, tr)
    dyp, d_pool_scale = _pool_out_grad("pool_out_grad", dh1, y_pre, pool_scale, tr)
    dpooled = _mm(
        "pool_mix_dx", dyp, w_pool, grid=(tokens // tm, d // tn, group // tk),
        a_spec=_spec2(tm, tk, lambda m, n, k: (m, (n // nper) * kper + k)),
        b_spec=pl.BlockSpec((None, tn, tk), lambda m, n, k: (n // nper, n % nper, k)),
        o_spec=_spec2(tm, tn, lambda m, n, k: (m, n)), acc_shape=(tm, tn), outs=[((tokens, d), F32)], dims=NT)[0]
    tmw, tkt = _tile(group, 1024), _tile(tokens, TILE_K)
    mper = group // tmw
    dw_pool = _mm(
        "pool_mix_dw", pooled, dyp, grid=(d // tmw, group // tn, tokens // tkt),
        a_spec=_spec2(tkt, tmw, lambda m, n, k: (k, m)), b_spec=_spec2(tkt, tn, lambda m, n, k: (k, (m // mper) * nper + n)),
        o_spec=pl.BlockSpec((None, tmw, tn), lambda m, n, k: (m // mper, m % mper, n)), acc_shape=(tmw, tn),
        outs=[((N_POOL_GROUPS, group, group), BF16)], dims=TN)[0]
    dw_pool_sums = _chip_sums("pool_mix_dw", dw_pool.reshape(N_POOL_GROUPS, N_DEV, group // N_DEV, group).transpose(1, 0, 2, 3))
    dw_pool_landed = _exchange("scatter_pool", scatters=[dw_pool_sums])[0]
    dhn0 = _pool_slabs("pool_bwd", dpooled, F32, n_seq=n_seq, seq=seq, backward=True)
    dx, _, d_norm_mix0 = _rms_bwd("rms_mix0_bwd", x2, norm_mix[0:1], dhn0, dh1, tr)

    pack_cols = key_dim
    rows_of = lambda a: a.reshape(-1, pack_cols)
    pieces = [rows_of(t) for t in (d_norm_mix0, d_norm_mix1, d_norm_mlp0, d_norm_mlp1, d_norm_final, d_pool_scale,
                                   dw_up_f[:GATE_RANK], dw_up_b[GATE_RANK:2 * GATE_RANK], db_up_f, db_up_b)]
    pieces += [jnp.pad(d_g_norm, ((0, 0), (0, pack_cols - dv))), jnp.pad(loss_row, ((0, 0), (0, pack_cols - LANES)))]
    n_rows = sum(p.shape[0] for p in pieces)
    pack = jnp.concatenate(pieces + [jnp.zeros((-n_rows % 8, pack_cols), F32)], axis=0)
    red = _all_reduce_small("reduce_small", pack)
    cuts, r0 = [], 0
    for p in pieces:
        cuts.append(red[r0:r0 + p.shape[0]])
        r0 += p.shape[0]
    g_norm_mix = jnp.concatenate([cuts[0].reshape(1, d), cuts[1].reshape(1, d)], axis=0)
    g_norm_mlp = jnp.concatenate([cuts[2].reshape(1, d), cuts[3].reshape(1, d)], axis=0)
    g_norm_final = cuts[4].reshape(d)
    g_pool_scale = cuts[5].reshape(1, d)
    shard_cols = lambda t, width: lax.dynamic_slice_in_dim(t, me * width, width, axis=1)
    g_w_up_f, g_w_up_b = shard_cols(cuts[6], kd_shard)[None], shard_cols(cuts[7], kd_shard)[None]
    g_b_up_f, g_b_up_b = shard_cols(cuts[8], kd_shard), shard_cols(cuts[9], kd_shard)
    g_g_norm = shard_cols(cuts[10][:, :dv], dv // N_DEV)
    loss = cuts[11][0, 0]

    big = {
        "pool_w": _adamw_shard("adamw_pool_w", pool_w, m_pool_w, v_pool_w, (dw_pool_landed, dw_pool_sums)),
        "gla_w_in": _adamw_shard("adamw_gla_w_in", gla_w_in, m_gla_w_in, v_gla_w_in, (dw_in_landed, dw_in_sums)),
        "gla_w_out": _adamw_shard("adamw_gla_w_out", gla_w_out, m_gla_w_out, v_gla_w_out, (dw_out_landed, dw_out_sums)),
    }
    for nm, w, m, v, landed in (("mlp_w_in", mlp_w_in, m_mlp_w_in, v_mlp_w_in, (dw1_0, dw1_1)),
                                ("mlp_w_out", mlp_w_out, m_mlp_w_out, v_mlp_w_out, (dw2_0, dw2_1))):
        layer1 = _adamw_shard(f"adamw_{nm}1", w, m, v, landed[1], layer=1)
        big[nm] = _adamw_shard(f"adamw_{nm}0", w, m, v, landed[0], layer=0, into=layer1)
    grads = {"norm_mix": g_norm_mix, "norm_mlp": g_norm_mlp, "norm_final": g_norm_final, "pool_scale": g_pool_scale,
             "gla_w_up_f": g_w_up_f, "gla_b_up_f": g_b_up_f, "gla_w_up_b": g_w_up_b, "gla_b_up_b": g_b_up_b, "gla_g_norm": g_g_norm}
    state = {"norm_mix": (norm_mix, m_norm_mix, v_norm_mix), "norm_mlp": (norm_mlp, m_norm_mlp, v_norm_mlp),
             "norm_final": (norm_final, m_norm_final, v_norm_final), "pool_scale": (pool_scale, m_pool_scale, v_pool_scale),
             "gla_w_up_f": (gla_w_up_f, m_gla_w_up_f, v_gla_w_up_f), "gla_b_up_f": (gla_b_up_f, m_gla_b_up_f, v_gla_b_up_f),
             "gla_w_up_b": (gla_w_up_b, m_gla_w_up_b, v_gla_w_up_b), "gla_b_up_b": (gla_b_up_b, m_gla_b_up_b, v_gla_b_up_b),
             "gla_g_norm": (gla_g_norm, m_gla_g_norm, v_gla_g_norm)}
    order = ["norm_mix", "norm_mlp", "norm_final", "pool_w", "pool_scale", "gla_w_in", "gla_w_up_f", "gla_b_up_f", "gla_w_up_b",
             "gla_b_up_b", "gla_g_norm", "gla_w_out", "mlp_w_in", "mlp_w_out"]
    out_g, out_d, out_m, out_v = [], [], [], []
    for nm in order:
        if nm in big:
            g, dl, new_m, new_v = big[nm]
        else:
            g = grads[nm]
            dl, new_m, new_v = _adamw_small(f"adamw_{nm}", *state[nm], g)
        out_g.append(g), out_d.append(dl), out_m.append(new_m), out_v.append(new_v)
    return (loss, dx.reshape(x.shape), *out_g, *out_d, *out_m, *out_v)
```

```python
import functools

import jax
import jax.numpy as jnp
from jax import lax
from jax.experimental import pallas as pl
from jax.experimental.pallas import tpu as pltpu

F32 = jnp.float32
BF16 = jnp.bfloat16
N_DEV = 8
N_PEER = N_DEV - 1
LANES = 128
VMEM_LIMIT_BYTES = 48 * 1024 * 1024
MESH_ID = pl.DeviceIdType.MESH
TILE_K = 1024
TILE_M_BIG = 2048

POOL_WINDOWS = (2, 4, 8, 16)
N_POOL_GROUPS = len(POOL_WINDOWS)
GLA_HEADS = 4
GATE_RANK = 16
GATE_TAU = 16.0
CHUNK = 64
EPS = 1e-6
ADAM_LR = 0.001
ADAM_B1 = 0.9
ADAM_B2 = 0.999
ADAM_EPS = 1e-08
ADAM_WD = 0.01
ADAM_STEP = 10

NN = ((1,), (0,))
NT = ((1,), (1,))
TN = ((0,), (0,))


def _tile(dim, pref):
    if dim <= pref:
        return dim
    t = (pref // LANES) * LANES
    while t > LANES and dim % t:
        t -= LANES
    assert dim % t == 0, (dim, pref)
    return t


def _row_tile(rows, pref):
    if rows <= pref:
        return rows
    t = max(8, pref // 8 * 8)
    while t > 8 and rows % t:
        t -= 8
    assert rows % t == 0, (rows, pref)
    return t


def _params(*sem):
    return pltpu.CompilerParams(dimension_semantics=sem, vmem_limit_bytes=VMEM_LIMIT_BYTES)


def _dot(a, b, dims):
    return lax.dot_general(a, b, (dims, ((), ())), preferred_element_type=F32)


def _mm(name, a, b, *, grid, a_spec, b_spec, o_spec, acc_shape, outs, dims, extras=(), epi=None, gathers=(), scatters=()):
    n_ex, n_out, nk = len(extras), len(outs), grid[2]
    n_g, n_s = len(gathers), len(scatters)
    carries = bool(n_g or n_s)

    def body(*refs):
        a_ref, b_ref = refs[0], refs[1]
        ex = refs[2:2 + n_ex]
        r0 = 2 + n_ex
        g_in, s_in = refs[r0:r0 + n_g], refs[r0 + n_g:r0 + n_g + n_s]
        r0 += n_g + n_s
        o_refs = refs[r0:r0 + n_out]
        g_out, s_out = refs[r0 + n_out:r0 + n_out + n_g], refs[r0 + n_out + n_g:r0 + n_out + n_g + n_s]
        acc = refs[r0 + n_out + n_g + n_s]
        i, j, k = pl.program_id(0), pl.program_id(1), pl.program_id(2)
        if carries:
            start, middle, finish = _transfers(g_in, g_out, s_in, s_out, (), (), *refs[-3:])
            step = (i * grid[1] + j) * nk + k

            @pl.when(step == 0)
            def _():
                start()

            if n_g:
                @pl.when(step == (grid[0] * grid[1] * nk * 5) // 8)
                def _():
                    middle()

        def part():
            return _dot(a_ref[...].astype(BF16), b_ref[...].astype(BF16), dims)

        def finish_tile(total):
            res = epi(total, *[e[...] for e in ex]) if epi is not None else (total,)
            for o, r in zip(o_refs, res):
                o[...] = r.astype(o.dtype)

        if nk == 1:
            finish_tile(part())
        else:
            @pl.when(k == 0)
            def _():
                acc[...] = part()

            @pl.when((k > 0) & (k < nk - 1))
            def _():
                acc[...] += part()

            @pl.when(k == nk - 1)
            def _():
                finish_tile(acc[...] + part())

        if carries:
            @pl.when((i == grid[0] - 1) & (j == grid[1] - 1) & (k == nk - 1))
            def _():
                finish()

    hbm = pl.BlockSpec(memory_space=pl.ANY)
    operands = [a, b] + [e for e, _ in extras] + list(gathers) + list(scatters)
    in_specs = [a_spec, b_spec] + [s for _, s in extras] + [hbm] * (n_g + n_s)
    out_shape = [jax.ShapeDtypeStruct(s, d) for s, d in outs]
    out_shape += [jax.ShapeDtypeStruct((N_DEV,) + g.shape, g.dtype) for g in gathers]
    out_shape += [jax.ShapeDtypeStruct((len(CHIP_BITS),) + t.shape[1:], t.dtype) for t in scatters]
    scratch = [pltpu.VMEM(acc_shape, F32)] + (_transfer_sems(n_g, n_s, 0) if carries else [])
    sem = ("arbitrary",) * 3 if carries else ("parallel", "parallel", "arbitrary")
    return pl.pallas_call(
        body, name=name, grid=grid, in_specs=in_specs, out_specs=[o_spec] * n_out + [hbm] * (n_g + n_s),
        out_shape=out_shape, scratch_shapes=scratch, compiler_params=_params(*sem),
    )(*operands)


def _spec2(rows, cols, fn):
    return pl.BlockSpec((rows, cols), fn)


def _rowwise(name, fn, *, n_rows, tr, rows, bcast=(), outs=(), accs=()):
    n_r, n_b, n_o, n_a = len(rows), len(bcast), len(outs), len(accs)

    def body(*refs):
        r_refs, b_refs = refs[:n_r], refs[n_r:n_r + n_b]
        o_refs, a_refs = refs[n_r + n_b:n_r + n_b + n_o], refs[n_r + n_b + n_o:]
        res = fn(*[r[...] for r in r_refs], *[b[...] for b in b_refs])
        for o, r in zip(o_refs, res[:n_o]):
            o[...] = r.astype(o.dtype)
        if n_a:
            @pl.when(pl.program_id(0) == 0)
            def _():
                for acc in a_refs:
                    acc[...] = jnp.zeros_like(acc)

            for acc, r in zip(a_refs, res[n_o:]):
                acc[...] += r

    in_specs = [pl.BlockSpec((tr, w), functools.partial(lambda i, cb: (i, cb), cb=cb)) for _, w, cb in rows]
    in_specs += [pl.BlockSpec(b.shape, lambda i: (0, 0)) for b in bcast]
    out_specs = [pl.BlockSpec((tr, w), lambda i: (i, 0)) for w, _ in outs]
    out_specs += [pl.BlockSpec(s, lambda i: (0, 0)) for s in accs]
    out_shape = [jax.ShapeDtypeStruct((n_rows, w), d) for w, d in outs]
    out_shape += [jax.ShapeDtypeStruct(s, F32) for s in accs]
    return pl.pallas_call(
        body, name=name, grid=(n_rows // tr,), in_specs=in_specs, out_specs=out_specs, out_shape=out_shape,
        compiler_params=_params("arbitrary"),
    )(*[r for r, _, _ in rows], *bcast)


def _colsum(t):
    return jnp.sum(t, axis=0, keepdims=True)


def _rms_fwd(name, h, g, dtype, tr):
    def fn(x, gain):
        r = lax.rsqrt(jnp.mean(x * x, axis=-1, keepdims=True) + EPS)
        return (x * r * gain,)

    d = h.shape[1]
    return _rowwise(name, fn, n_rows=h.shape[0], tr=tr, rows=[(h, d, 0)], bcast=[g], outs=[(d, dtype)])[0]


def _rms_bwd(name, h, g, dhn, dres, tr):
    def fn(x, dy, dr, gain):
        r = lax.rsqrt(jnp.mean(x * x, axis=-1, keepdims=True) + EPS)
        xhat = x * r
        gd = gain * dy
        dh = r * (gd - xhat * jnp.mean(gd * xhat, axis=-1, keepdims=True)) + dr
        return dh, dh, _colsum(dy * xhat)

    d = h.shape[1]
    return _rowwise(name, fn, n_rows=h.shape[0], tr=tr, rows=[(h, d, 0), (dhn, d, 0), (dres, d, 0)], bcast=[g],
                    outs=[(d, F32), (d, BF16)], accs=[(1, d)])


def _loss_head(name, h, target, g, tr):
    d = h.shape[1]

    def fn(x, tgt, gain):
        r = lax.rsqrt(jnp.mean(x * x, axis=-1, keepdims=True) + EPS)
        xhat = x * r
        err = xhat * gain - tgt
        loss = 0.5 * jnp.sum(jnp.mean(err * err, axis=-1, keepdims=True), axis=0, keepdims=True)
        dy = err / d
        gd = gain * dy
        dh = r * (gd - xhat * jnp.mean(gd * xhat, axis=-1, keepdims=True))
        return dh, dh, jnp.broadcast_to(loss, (1, LANES)), _colsum(dy * xhat)

    return _rowwise(name, fn, n_rows=h.shape[0], tr=tr, rows=[(h, d, 0), (target, d, 0)], bcast=[g],
                    outs=[(d, F32), (d, BF16)], accs=[(1, LANES), (1, d)])


def _shift_rows(h, k, seq):
    if k == 0:
        return h
    rolled = pltpu.roll(h, (-k) % seq, 0)
    t = lax.broadcasted_iota(jnp.int32, h.shape, 0)
    return jnp.where((t + k >= 0) & (t + k < seq), rolled, 0.0)


def _window_halves(h, half, seq):
    right = h
    left = _shift_rows(h, -1, seq)
    step = 1
    while step < half:
        right = right + _shift_rows(right, step, seq)
        left = left + _shift_rows(left, -step, seq)
        step *= 2
    return left + right


def _window_count(shape, half, seq):
    t = lax.broadcasted_iota(jnp.int32, shape, 0)
    return (jnp.minimum(t + half, seq) - jnp.maximum(t - half, 0)).astype(F32)


def _pool_slabs(name, src, out_dtype, *, n_seq, seq, backward):
    d = src.shape[1]
    group = d // N_POOL_GROUPS
    tc = _tile(group, 256)

    def body(s_ref, o_ref):
        g = (pl.program_id(1) * tc) // group
        for gi, w in enumerate(POOL_WINDOWS):
            half = w // 2

            @pl.when(g == gi)
            def _():
                v = s_ref[...].astype(F32)
                cnt = _window_count(v.shape, half, seq)
                if backward:
                    u = v / cnt
                    res = (_window_halves(u, half, seq) - _shift_rows(u, -half, seq) + _shift_rows(u, half, seq)) - v
                else:
                    res = _window_halves(v, half, seq) / cnt - v
                o_ref[...] = res.astype(o_ref.dtype)

    spec = pl.BlockSpec((seq, tc), lambda b, j: (b, j))
    return pl.pallas_call(
        body, name=name, grid=(n_seq, d // tc), in_specs=[spec], out_specs=spec,
        out_shape=jax.ShapeDtypeStruct(src.shape, out_dtype), compiler_params=_params("parallel", "parallel"),
    )(src)


def _split3(x):
    x1 = x.astype(BF16)
    r1 = x - x1.astype(F32)
    x2 = r1.astype(BF16)
    x3 = (r1 - x2.astype(F32)).astype(BF16)
    return x1, x2, x3


def _tri_dot(tri, x):
    x1, x2, x3 = _split3(x)
    return _dot(tri, x1, NN) + _dot(tri, x2, NN) + _dot(tri, x3, NN)


def _chunk_terms(q_ref, k_ref, g_ref, rev, scale):
    c = q_ref.shape[0]
    ri = lax.broadcasted_iota(jnp.int32, (c, c), 0)
    ci = lax.broadcasted_iota(jnp.int32, (c, c), 1)
    causal = (ci >= ri) if rev else (ci <= ri)
    mid = c // 2 if rev else c // 2 - 1
    last = 0 if rev else c - 1
    b = _tri_dot(causal.astype(BF16), g_ref[...])
    b_mid, b_last = b[mid:mid + 1, :], b[last:last + 1, :]
    e_qm, e_km, e_qi, e_ks = jnp.exp(b - b_mid), jnp.exp(b_mid - b), jnp.exp(b), jnp.exp(b_last - b)
    q = q_ref[...].astype(F32) * scale
    k = k_ref[...].astype(F32)
    return dict(causal=causal, mid=mid, last=last, decay=jnp.exp(b_last), e_qm=e_qm, e_km=e_km, e_qi=e_qi, e_ks=e_ks,
                qm=q * e_qm, km=k * e_km, qi=q * e_qi, ks=k * e_ks)


HEADS_PER_STEP = 2


def _gla_specs(n_chunks, heads, dk, dv, key_dim, rev, backward):
    hps = HEADS_PER_STEP

    def chunk(n):
        scan_pos = (n_chunks - 1 - n) if backward else n
        return (n_chunks - 1 - scan_pos) if rev else scan_pos

    def at(width, col0):
        return pl.BlockSpec((CHUNK, hps * width), lambda b, h, n: (b * n_chunks + chunk(n), col0 // (hps * width) + h))

    state = pl.BlockSpec((hps, dv, dk), lambda b, h, n: ((b * n_chunks + chunk(n)) * (heads // hps) + h, 0, 0))
    return at(dk, 0), at(dk, key_dim), at(dv, 2 * key_dim), at(dk, 0), at(dv, 0), state


def _gla_fwd(name, pm, log_a, *, rev, n_seq, seq, key_dim, val_dim):
    heads, n_chunks = GLA_HEADS, seq // CHUNK
    dk, dv = key_dim // heads, val_dim // heads
    scale = dk ** -0.5
    q_spec, k_spec, v_spec, g_spec, o_spec, st_spec = _gla_specs(n_chunks, heads, dk, dv, key_dim, rev, False)

    def body(q_ref, k_ref, v_ref, g_ref, o_ref, st_ref, state):
        @pl.when(pl.program_id(2) == 0)
        def _():
            state[...] = jnp.zeros_like(state)

        for i in range(HEADS_PER_STEP):
            kc, vc = pl.ds(i * dk, dk), pl.ds(i * dv, dv)
            t = _chunk_terms(q_ref.at[:, kc], k_ref.at[:, kc], g_ref.at[:, kc], rev, scale)
            v = v_ref[:, vc]
            p = jnp.where(t["causal"], _dot(t["qm"].astype(BF16), t["km"].astype(BF16), NT), 0.0)
            entering = state[i].astype(BF16)
            st_ref[i] = entering
            o_ref[:, vc] = _dot(p.astype(BF16), v, NN) + _dot(t["qi"].astype(BF16), entering, NT)
            state[i] = t["decay"] * state[i] + _dot(v, t["ks"].astype(BF16), TN)

    tokens = n_seq * seq
    return pl.pallas_call(
        body, name=name, grid=(n_seq, heads // HEADS_PER_STEP, n_chunks), in_specs=[q_spec, k_spec, v_spec, g_spec],
        out_specs=[o_spec, st_spec],
        out_shape=[jax.ShapeDtypeStruct((tokens, val_dim), F32),
                   jax.ShapeDtypeStruct((n_seq * n_chunks * heads, dv, dk), BF16)],
        scratch_shapes=[pltpu.VMEM((HEADS_PER_STEP, dv, dk), F32)],
        compiler_params=_params("parallel", "parallel", "arbitrary"),
    )(pm, pm, pm, log_a)


def _gla_bwd(name, pm, log_a, states, d_o, *, rev, n_seq, seq, key_dim, val_dim):
    heads, n_chunks = GLA_HEADS, seq // CHUNK
    dk, dv = key_dim // heads, val_dim // heads
    scale = dk ** -0.5
    q_spec, k_spec, v_spec, g_spec, o_spec, st_spec = _gla_specs(n_chunks, heads, dk, dv, key_dim, rev, True)

    def body(q_ref, k_ref, v_ref, g_ref, do_ref, st_ref, dq_ref, dk_ref, dv_ref, dg_ref, dstate):
        @pl.when(pl.program_id(2) == 0)
        def _():
            dstate[...] = jnp.zeros_like(dstate)

        for i in range(HEADS_PER_STEP):
            kc, vc = pl.ds(i * dk, dk), pl.ds(i * dv, dv)
            t = _chunk_terms(q_ref.at[:, kc], k_ref.at[:, kc], g_ref.at[:, kc], rev, scale)
            causal = t["causal"]
            v, d_out, entering = v_ref[:, vc], do_ref[:, vc], st_ref[i]
            qm, km, qi, ks = (t[n].astype(BF16) for n in ("qm", "km", "qi", "ks"))
            dst = dstate[i]
            dst16 = dst.astype(BF16)
            p = jnp.where(causal, _dot(qm, km, NT), 0.0).astype(BF16)
            dp = jnp.where(causal, _dot(d_out, v, NT), 0.0).astype(BF16)
            dv_ref[:, vc] = _dot(p, d_out, TN) + _dot(ks, dst16, NT)
            dqm, dkm = _dot(dp, km, NN), _dot(dp, qm, TN)
            dqi, dks = _dot(d_out, entering, NN), _dot(v, dst16, NN)
            d_decay = _colsum(dst * entering.astype(F32))
            dq_ref[:, kc] = (dqi * t["e_qi"] + dqm * t["e_qm"]) * scale
            dk_ref[:, kc] = dkm * t["e_km"] + dks * t["e_ks"]
            qm_term, km_term = dqm * t["qm"], dkm * t["km"]
            ks_term = dks * t["ks"]
            db = dqi * t["qi"] + qm_term - km_term - ks_term
            row = lax.broadcasted_iota(jnp.int32, db.shape, 0)
            db = db + jnp.where(row == t["mid"], _colsum(km_term - qm_term), 0.0)
            db = db + jnp.where(row == t["last"], _colsum(ks_term) + d_decay * t["decay"], 0.0)
            ri = lax.broadcasted_iota(jnp.int32, causal.shape, 0)
            ci = lax.broadcasted_iota(jnp.int32, causal.shape, 1)
            anti = (ci <= ri) if rev else (ci >= ri)
            dg_ref[:, kc] = _tri_dot(anti.astype(BF16), db)
            dstate[i] = t["decay"] * dst + _dot(d_out, qi, TN)

    tokens = n_seq * seq
    kd = jax.ShapeDtypeStruct((tokens, key_dim), F32)
    return pl.pallas_call(
        body, name=name, grid=(n_seq, heads // HEADS_PER_STEP, n_chunks),
        in_specs=[q_spec, k_spec, v_spec, g_spec, o_spec, st_spec], out_specs=[g_spec, g_spec, o_spec, g_spec],
        out_shape=[kd, kd, jax.ShapeDtypeStruct((tokens, val_dim), F32), kd],
        scratch_shapes=[pltpu.VMEM((HEADS_PER_STEP, dv, dk), F32)],
        compiler_params=_params("parallel", "parallel", "arbitrary"),
    )(pm, pm, pm, log_a, d_o, states)


def _sigmoid_neg(z):
    e = jnp.exp(-jnp.abs(z))
    return jnp.where(z >= 0, e / (1.0 + e), 1.0 / (1.0 + e))


def _gate_fwd(name, pr, wf, bf, wb, bb, tr):
    kd = wf.shape[1]

    def fn(r, w_f, b_f, w_b, b_b):
        r16 = r.astype(BF16)
        out = []
        for w, b in ((w_f, b_f), (w_b, b_b)):
            z = _dot(r16, w.astype(BF16), NN) + b
            out.append((jnp.minimum(z, 0.0) - jnp.log1p(jnp.exp(-jnp.abs(z)))) / GATE_TAU)
        return tuple(out)

    return _rowwise(name, fn, n_rows=pr.shape[0], tr=tr, rows=[(pr, LANES, 0)], bcast=[wf, bf, wb, bb],
                    outs=[(kd, F32), (kd, F32)])


def _gate_bwd(name, pr, wf, bf, wb, bb, dgf, dgb, tr):
    kd = wf.shape[1]

    def fn(r, d_f, d_b, w_f, b_f, w_b, b_b):
        r16 = r.astype(BF16)
        dpr = jnp.zeros(r.shape, F32)
        sums = []
        for w, b, dg in ((w_f, b_f, d_f), (w_b, b_b, d_b)):
            w16 = w.astype(BF16)
            z = _dot(r16, w16, NN) + b
            dz = dg / GATE_TAU * _sigmoid_neg(z)
            dz16 = dz.astype(BF16)
            dpr = dpr + _dot(dz16, w16, NT)
            sums += [_dot(r16, dz16, TN), _colsum(dz)]
        return (dpr, *sums)

    return _rowwise(name, fn, n_rows=pr.shape[0], tr=tr, rows=[(pr, LANES, 0), (dgf, kd, 0), (dgb, kd, 0)],
                    bcast=[wf, bf, wb, bb], outs=[(LANES, BF16)], accs=[(LANES, kd), (1, kd), (LANES, kd), (1, kd)])


def _head_norm_fwd(name, of, ob, pm, gn, *, val_dim, gate_block, tr):
    dv = val_dim // GLA_HEADS

    def fn(o_f, o_b, gate, gain):
        out = []
        for h in range(GLA_HEADS):
            sl = slice(h * dv, (h + 1) * dv)
            o = o_f[:, sl] + o_b[:, sl]
            gt = gate[:, sl].astype(F32)
            on = o * lax.rsqrt(jnp.mean(o * o, axis=-1, keepdims=True) + EPS) * gain
            out.append(on * (gt * jax.nn.sigmoid(gt)))
        return (jnp.concatenate(out, axis=1),)

    return _rowwise(name, fn, n_rows=of.shape[0], tr=tr, rows=[(of, val_dim, 0), (ob, val_dim, 0), (pm, val_dim, gate_block)],
                    bcast=[gn], outs=[(val_dim, BF16)])[0]


def _head_norm_bwd(name, of, ob, pm, gn, dz, *, val_dim, gate_block, tr):
    dv = val_dim // GLA_HEADS

    def fn(o_f, o_b, gate, d_z, gain):
        d_o, d_gate = [], []
        d_gain = jnp.zeros((1, dv), F32)
        for h in range(GLA_HEADS):
            sl = slice(h * dv, (h + 1) * dv)
            o = o_f[:, sl] + o_b[:, sl]
            gt = gate[:, sl].astype(F32)
            dzh = d_z[:, sl]
            r = lax.rsqrt(jnp.mean(o * o, axis=-1, keepdims=True) + EPS)
            ohat = o * r
            sg = jax.nn.sigmoid(gt)
            d_gate.append(dzh * (ohat * gain) * (sg * (1.0 + gt * (1.0 - sg))))
            don = dzh * (gt * sg)
            d_gain = d_gain + _colsum(don * ohat)
            gd = gain * don
            d_o.append(r * (gd - ohat * jnp.mean(gd * ohat, axis=-1, keepdims=True)))
        return jnp.concatenate(d_o, axis=1), jnp.concatenate(d_gate, axis=1), d_gain

    return _rowwise(name, fn, n_rows=of.shape[0], tr=tr,
                    rows=[(of, val_dim, 0), (ob, val_dim, 0), (pm, val_dim, gate_block), (dz, val_dim, 0)], bcast=[gn],
                    outs=[(val_dim, BF16), (val_dim, BF16)], accs=[(1, dv)])


def _proj_grad(name, dqf, dqb, dkf, dkb, dvf, dvb, dgate, *, key_dim, val_dim, tr):
    def fn(q1, q2, k1, k2, v1, v2, gt):
        return (jnp.concatenate([(q1 + q2).astype(BF16), (k1 + k2).astype(BF16), (v1 + v2).astype(BF16), gt], axis=1),)

    rows = [(dqf, key_dim, 0), (dqb, key_dim, 0), (dkf, key_dim, 0), (dkb, key_dim, 0), (dvf, val_dim, 0),
            (dvb, val_dim, 0), (dgate, val_dim, 0)]
    return _rowwise(name, fn, n_rows=dqf.shape[0], tr=tr, rows=rows, outs=[(2 * key_dim + 2 * val_dim, BF16)])[0]


def _pool_out_grad(name, dh, y_pre, scale, tr):
    def fn(d, y, s):
        return d * s, _colsum(d * y)

    dm = dh.shape[1]
    return _rowwise(name, fn, n_rows=dh.shape[0], tr=tr, rows=[(dh, dm, 0), (y_pre, dm, 0)], bcast=[scale],
                    outs=[(dm, BF16)], accs=[(1, dm)])


def _mesh_place():
    x, y, c = lax.axis_index("x"), lax.axis_index("y"), lax.axis_index("c")
    return x, y, c, 4 * x + 2 * y + c


def _peer(x, y, c, p):
    px, py, pc = (x if not p & 4 else 1 - x), (y if not p & 2 else 1 - y), (c if not p & 1 else 1 - c)
    return (px, py, pc), 4 * px + 2 * py + pc


CHIP_BITS = (2, 4, 6)
N_CHIP = N_DEV // 2
GATHER_COPIES = 8
FLIP_C, FLIP_Y, FLIP_X = 1, 2, 4


def _transfer_sems(n_gather, n_scatter, n_pair):
    n_remote = n_gather * GATHER_COPIES + n_scatter * len(CHIP_BITS) + n_pair * N_CHIP
    return [pltpu.SemaphoreType.DMA((n_remote,)), pltpu.SemaphoreType.DMA((n_remote,)),
            pltpu.SemaphoreType.DMA((max(n_gather, 1),))]


def _transfers(g_in, g_out, s_in, s_out, p_in, p_out, send, recv, loc):
    x, y, c, me = _mesh_place()
    sibling, _ = _peer(x, y, c, FLIP_C)

    def remote(src, dst, k, peer):
        return pltpu.make_async_remote_copy(src_ref=src, dst_ref=dst, send_sem=send.at[k], recv_sem=recv.at[k],
                                            device_id=peer, device_id_type=MESH_ID)

    local, first, arrivals = [], [], []
    passed, passed_arrivals, last, last_arrivals = [], [], [], []
    k = 0
    for src, out in zip(g_in, g_out):
        half = src.shape[0] // 2
        halves = (pl.ds(0, half), pl.ds(half, half))
        x_chip, x_block = _peer(x, y, c, FLIP_X)
        y_chip, y_block = _peer(x, y, c, FLIP_Y)
        _, far_block = _peer(x, y, c, FLIP_X | FLIP_Y)
        local.append(pltpu.make_async_copy(src, out.at[me], loc.at[len(local)]))
        for peer, block in ((sibling, me ^ FLIP_C), (x_chip, x_block), (y_chip, y_block)):
            first.append(remote(src, out.at[me], k, peer))
            arrivals.append(remote(src, out.at[block], k, peer))
            k += 1
        for peer, block, part in ((x_chip, y_block, halves[0]), (y_chip, x_block, halves[1])):
            passed.append(remote(out.at[block, part], out.at[block, part], k, peer))
            passed_arrivals.append(remote(out.at[far_block, part], out.at[far_block, part], k, peer))
            k += 1
        for block in (x_block, y_block):
            passed.append(remote(out.at[block], out.at[block], k, sibling))
            passed_arrivals.append(remote(out.at[block ^ FLIP_C], out.at[block ^ FLIP_C], k, sibling))
            k += 1
        last.append(remote(out.at[far_block], out.at[far_block], k, sibling))
        last_arrivals.append(remote(out.at[far_block ^ FLIP_C], out.at[far_block ^ FLIP_C], k, sibling))
        k += 1
    n_gather_first = len(first)
    for src, out in zip(s_in, s_out):
        for j, p in enumerate(CHIP_BITS):
            peer, _ = _peer(x, y, c, p)
            peer_chip = 2 * peer[0] + peer[1]
            first.append(remote(src.at[peer_chip], out.at[j], k, peer))
            arrivals.append(remote(src.at[peer_chip], out.at[j], k, peer))
            k += 1
    for src, out in zip(p_in, p_out):
        for q in range(N_CHIP):
            first.append(remote(src.at[2 * q + 1 - c], out.at[q], k, sibling))
            arrivals.append(remote(src.at[2 * q + 1 - c], out.at[q], k, sibling))
            k += 1

    def start():
        for cp in local + first:
            cp.start()

    def middle():
        for cp in arrivals[:n_gather_first]:
            cp.wait_recv()
        for cp in passed:
            cp.start()

    def finish():
        for cp in passed_arrivals:
            cp.wait_recv()
        for cp in last:
            cp.start()
        for cp in last_arrivals + arrivals[n_gather_first:]:
            cp.wait_recv()
        for cp in first + passed + last:
            cp.wait_send()
        for cp in local:
            cp.wait()

    return start, middle, finish


def _exchange(name, gathers=(), scatters=(), pairs=()):
    n_g, n_s, n_p = len(gathers), len(scatters), len(pairs)
    n_in = n_g + n_s + n_p

    def body(*refs):
        ins, outs = refs[:n_in], refs[n_in:-3]
        start, middle, finish = _transfers(ins[:n_g], outs[:n_g], ins[n_g:n_g + n_s], outs[n_g:n_g + n_s], ins[n_g + n_s:],
                                           outs[n_g + n_s:], *refs[-3:])
        start()
        middle()
        finish()

    hbm = pl.BlockSpec(memory_space=pl.ANY)
    out_shape = [jax.ShapeDtypeStruct((N_DEV,) + g.shape, g.dtype) for g in gathers]
    out_shape += [jax.ShapeDtypeStruct((len(CHIP_BITS),) + t.shape[1:], t.dtype) for t in scatters]
    out_shape += [jax.ShapeDtypeStruct((N_CHIP,) + t.shape[1:], t.dtype) for t in pairs]
    return pl.pallas_call(
        body, name=name, in_specs=[hbm] * n_in, out_specs=[hbm] * len(out_shape), out_shape=out_shape,
        scratch_shapes=_transfer_sems(n_g, n_s, n_p),
    )(*gathers, *scatters, *pairs)


def _place_scalar(axis_value):
    return jnp.asarray(axis_value, jnp.int32).reshape(1)


def _chip_sums(name, grad):
    received = _exchange(name + "_pair", pairs=[grad])[0]
    cols = grad.shape[-1]
    rows = received.size // cols // N_CHIP
    tr = _row_tile(rows, (512 * 1024) // cols)

    def body(core_ref, mine_ref, got_ref, out_ref):
        out_ref[...] = (mine_ref[...].astype(F32) + got_ref[...].astype(F32)).astype(out_ref.dtype)

    block = (None, tr, cols)
    out = pl.pallas_call(
        body, name=name + "_add", out_shape=jax.ShapeDtypeStruct((N_CHIP, rows, cols), grad.dtype),
        grid_spec=pltpu.PrefetchScalarGridSpec(
            num_scalar_prefetch=1, grid=(N_CHIP, rows // tr),
            in_specs=[pl.BlockSpec(block, lambda q, i, core: (2 * q + core[0], i, 0)),
                      pl.BlockSpec(block, lambda q, i, core: (q, i, 0))],
            out_specs=pl.BlockSpec(block, lambda q, i, core: (q, i, 0))),
        compiler_params=_params("parallel", "parallel"),
    )(_place_scalar(lax.axis_index("c")), grad.reshape(N_DEV, rows, cols), received.reshape(N_CHIP, rows, cols))
    return out.reshape(received.shape)


def _all_reduce_small(name, pack):
    def body(in_ref, out_ref, parts, send_sems, recv_sems):
        x, y, c, me = _mesh_place()
        parts[me] = in_ref[...]

        def copy(p, arriving):
            peer, peer_block = _peer(x, y, c, p)
            return pltpu.make_async_remote_copy(src_ref=in_ref, dst_ref=parts.at[peer_block if arriving else me],
                                                send_sem=send_sems.at[p - 1], recv_sem=recv_sems.at[p - 1],
                                                device_id=peer, device_id_type=MESH_ID)

        sends = [copy(p, False) for p in range(1, N_DEV)]
        for cp in sends:
            cp.start()
        for p in range(1, N_DEV):
            copy(p, True).wait_recv()
        for cp in sends:
            cp.wait_send()
        total = parts[0]
        for j in range(1, N_DEV):
            total = total + parts[j]
        out_ref[...] = total

    vmem = pl.BlockSpec(memory_space=pltpu.VMEM)
    return pl.pallas_call(
        body, name=name, in_specs=[vmem], out_specs=vmem, out_shape=jax.ShapeDtypeStruct(pack.shape, F32),
        scratch_shapes=[pltpu.VMEM((N_DEV,) + pack.shape, F32), pltpu.SemaphoreType.DMA((N_PEER,)),
                        pltpu.SemaphoreType.DMA((N_PEER,))],
        compiler_params=pltpu.CompilerParams(vmem_limit_bytes=VMEM_LIMIT_BYTES),
    )(pack)


def _adamw_math(w, g, m, v):
    m = ADAM_B1 * m + (1.0 - ADAM_B1) * g
    v = ADAM_B2 * v + (1.0 - ADAM_B2) * jnp.square(g)
    m_hat = m / (1.0 - ADAM_B1 ** ADAM_STEP)
    v_hat = v / (1.0 - ADAM_B2 ** ADAM_STEP)
    delta = -ADAM_LR * (m_hat / (jnp.sqrt(v_hat) + ADAM_EPS) + ADAM_WD * w)
    return delta, m, v


def _adamw_shard(name, w, m, v, grad_parts, layer=0, into=None):
    landed, sums = grad_parts
    shape = w.shape
    n_layers, cols = shape[0], shape[-1]
    rows = w.size // cols // n_layers
    tr = _row_tile(rows, (256 * 1024) // cols)
    n_into = 0 if into is None else 4
    n_landed = landed.shape[0]

    def body(chip_ref, w_ref, m_ref, v_ref, mine_ref, got_ref, *refs):
        g_ref, d_ref, nm_ref, nv_ref = refs[n_into:]
        g = mine_ref[...].astype(F32)
        for j in range(n_landed):
            g = g + got_ref[j].astype(F32)
        delta, new_m, new_v = _adamw_math(w_ref[...], g, m_ref[...], v_ref[...])
        g_ref[...], d_ref[...], nm_ref[...], nv_ref[...] = g, delta, new_m, new_v

    if n_layers == 1:
        view = (rows, cols)
        spec = pl.BlockSpec((tr, cols), lambda i, chip: (i, 0))
    else:
        view = (n_layers, rows, cols)
        spec = pl.BlockSpec((None, tr, cols), lambda i, chip: (layer, i, 0))
    operands = [w.reshape(view), m.reshape(view), v.reshape(view), sums.reshape(N_CHIP, rows, cols),
                landed.reshape(n_landed, rows, cols)]
    in_specs = [spec, spec, spec, pl.BlockSpec((None, tr, cols), lambda i, chip: (chip[0], i, 0)),
                pl.BlockSpec((n_landed, tr, cols), lambda i, chip: (0, i, 0))]
    if into is not None:
        operands += [t.reshape(view) for t in into]
        in_specs += [pl.BlockSpec(memory_space=pl.ANY)] * 4
    first_into = 1 + 5
    res = pl.pallas_call(
        body, name=name, out_shape=[jax.ShapeDtypeStruct(view, F32)] * 4,
        grid_spec=pltpu.PrefetchScalarGridSpec(num_scalar_prefetch=1, grid=(rows // tr,), in_specs=in_specs, out_specs=[spec] * 4),
        input_output_aliases={first_into + t: t for t in range(n_into)}, compiler_params=_params("parallel"),
    )(_place_scalar(2 * lax.axis_index("x") + lax.axis_index("y")), *operands)
    return [r.reshape(shape) for r in res]


def _adamw_small(name, w, m, v, g):
    shape = w.shape
    two_d = (1, shape[0]) if len(shape) == 1 else (w.size // shape[-1], shape[-1])

    def body(w_ref, m_ref, v_ref, g_ref, d_ref, nm_ref, nv_ref):
        d_ref[...], nm_ref[...], nv_ref[...] = _adamw_math(w_ref[...], g_ref[...], m_ref[...], v_ref[...])

    res = pl.pallas_call(body, name=name, out_shape=[jax.ShapeDtypeStruct(two_d, F32)] * 3)(
        w.reshape(two_d), m.reshape(two_d), v.reshape(two_d), g.reshape(two_d))
    return [r.reshape(shape) for r in res]


def _mlp_fwd(tag, h_in, gain, w1, w2_mine, down_gathers, tm, tr):
    tokens, d = h_in.shape
    fs = w1.shape[2]
    f = N_DEV * fs
    hn = _rms_fwd(f"rms_mlp{tag}", h_in, gain, BF16, tr)
    tn, tk = _tile(fs, 1024), _tile(d, TILE_K)
    per = fs // tn
    tmu = _tile(tokens, TILE_M_BIG)
    act, w2 = _mm(
        f"mlp_up{tag}", hn, w1, grid=(tokens // tmu, f // tn, d // tk), a_spec=_spec2(tmu, tk, lambda m, n, k: (m, k)),
        b_spec=pl.BlockSpec((None, tk, tn), lambda m, n, k: (n // per, k, n % per)),
        o_spec=_spec2(tmu, tn, lambda m, n, k: (m, n)), acc_shape=(tmu, tn), outs=[((tokens, f), BF16)], dims=NN,
        epi=lambda acc: (jnp.square(jnp.maximum(acc, 0.0)),), gathers=[w2_mine])
    tn2, tk2 = _tile(d, 1024), _tile(fs, TILE_K)
    kper = fs // tk2
    h_out, *got = _mm(
        f"mlp_down{tag}", act, w2, grid=(tokens // tm, d // tn2, f // tk2), a_spec=_spec2(tm, tk2, lambda m, n, k: (m, k)),
        b_spec=pl.BlockSpec((None, tk2, tn2), lambda m, n, k: (k // kper, k % kper, n)),
        o_spec=_spec2(tm, tn2, lambda m, n, k: (m, n)), acc_shape=(tm, tn2), outs=[((tokens, d), F32)], dims=NN,
        extras=[(h_in, _spec2(tm, tn2, lambda m, n, k: (m, n)))], epi=lambda acc, res: (res + acc,), gathers=down_gathers)
    return hn, act, h_out, w2, got


def _mlp_bwd(tag, h_in, gain, hn, act, dh, dh16, w1, w2, tm, tr):
    tokens, d = h_in.shape
    fs = w1.shape[2]
    f = N_DEV * fs
    tn, tk = _tile(fs, 1024), _tile(d, TILE_K)
    nper = fs // tn
    tmb = _tile(tokens, TILE_M_BIG)
    da = _mm(
        f"mlp_dact{tag}", dh16, w2, grid=(tokens // tmb, f // tn, d // tk), a_spec=_spec2(tmb, tk, lambda m, n, k: (m, k)),
        b_spec=pl.BlockSpec((None, tn, tk), lambda m, n, k: (n // nper, n % nper, k)),
        o_spec=_spec2(tmb, tn, lambda m, n, k: (m, n)), acc_shape=(tmb, tn), outs=[((tokens, f), BF16)], dims=NT,
        extras=[(act, _spec2(tmb, tn, lambda m, n, k: (m, n)))],
        epi=lambda acc, a2: (acc * (2.0 * jnp.sqrt(a2.astype(F32))),))[0]
    tkt = _tile(tokens, TILE_K)
    tm2, tn2 = _tile(fs, TILE_M_BIG), _tile(d, 1024)
    mper = fs // tm2
    dw2 = _mm(
        f"mlp_dw_out{tag}", act, dh16, grid=(f // tm2, d // tn2, tokens // tkt), a_spec=_spec2(tkt, tm2, lambda m, n, k: (k, m)),
        b_spec=_spec2(tkt, tn2, lambda m, n, k: (k, n)),
        o_spec=pl.BlockSpec((None, tm2, tn2), lambda m, n, k: (m // mper, m % mper, n)), acc_shape=(tm2, tn2),
        outs=[((N_DEV, fs, d), BF16)], dims=TN)[0]
    dw2_sums = _chip_sums(f"mlp_dw_out{tag}", dw2)
    tm3 = _tile(d, TILE_M_BIG)
    dw1, dw2_landed = _mm(
        f"mlp_dw_in{tag}", hn, da, grid=(d // tm3, f // tn, tokens // tkt), a_spec=_spec2(tkt, tm3, lambda m, n, k: (k, m)),
        b_spec=_spec2(tkt, tn, lambda m, n, k: (k, n)),
        o_spec=pl.BlockSpec((None, tm3, tn), lambda m, n, k: (n // nper, m, n % nper)), acc_shape=(tm3, tn),
        outs=[((N_DEV, d, fs), BF16)], dims=TN, scatters=[dw2_sums])
    dw1_sums = _chip_sums(f"mlp_dw_in{tag}", dw1)
    tn4, tk4 = _tile(d, 1024), _tile(fs, TILE_K)
    kper = fs // tk4
    dhn, dw1_landed = _mm(
        f"mlp_dx{tag}", da, w1, grid=(tokens // tmb, d // tn4, f // tk4), a_spec=_spec2(tmb, tk4, lambda m, n, k: (m, k)),
        b_spec=pl.BlockSpec((None, tn4, tk4), lambda m, n, k: (k // kper, n, k % kper)),
        o_spec=_spec2(tmb, tn4, lambda m, n, k: (m, n)), acc_shape=(tmb, tn4), outs=[((tokens, d), F32)], dims=NT,
        scatters=[dw1_sums])
    dh_in, dh_in16, dgain = _rms_bwd(f"rms_mlp_bwd{tag}", h_in, gain, dhn, dh, tr)
    return dh_in, dh_in16, dgain, (dw1_landed, dw1_sums), (dw2_landed, dw2_sums)


def _plain_mm(name, a, b, dims, out_dtype, *, tm, tn_pref=1024, tk_pref=TILE_K, extras=None, epi=None, gathers=(), scatters=()):
    if dims == NN:
        (m, k), n = a.shape, b.shape[1]
    elif dims == NT:
        (m, k), n = a.shape, b.shape[0]
    else:
        (k, m), n = a.shape, b.shape[1]
    tm, tn, tk = _tile(m, tm), _tile(n, tn_pref), _tile(k, tk_pref)
    a_spec = _spec2(tk, tm, lambda i, j, l: (l, i)) if dims == TN else _spec2(tm, tk, lambda i, j, l: (i, l))
    b_spec = _spec2(tn, tk, lambda i, j, l: (j, l)) if dims == NT else _spec2(tk, tn, lambda i, j, l: (l, j))
    o_spec = _spec2(tm, tn, lambda i, j, l: (i, j))
    ex = [(e, o_spec) for e in (extras or [])]
    res = _mm(name, a, b, grid=(m // tm, n // tn, k // tk), a_spec=a_spec, b_spec=b_spec, o_spec=o_spec, acc_shape=(tm, tn),
              outs=[((m, n), out_dtype)], dims=dims, extras=ex, epi=epi, gathers=gathers, scatters=scatters)
    return res if (gathers or scatters) else res[0]


def kernel(x, norm_mix, norm_mlp, norm_final, pool_w, pool_scale, gla_w_in, gla_w_up_f, gla_b_up_f, gla_w_up_b, gla_b_up_b, gla_g_norm, gla_w_out, mlp_w_in, mlp_w_out, loss_target, m_norm_mix, m_norm_mlp, m_norm_final, m_pool_w, m_pool_scale, m_gla_w_in, m_gla_w_up_f, m_gla_b_up_f, m_gla_w_up_b, m_gla_b_up_b, m_gla_g_norm, m_gla_w_out, m_mlp_w_in, m_mlp_w_out, v_norm_mix, v_norm_mlp, v_norm_final, v_pool_w, v_pool_scale, v_gla_w_in, v_gla_w_up_f, v_gla_b_up_f, v_gla_w_up_b, v_gla_b_up_b, v_gla_g_norm, v_gla_w_out, v_mlp_w_in, v_mlp_w_out):
    n_seq, seq, d = x.shape
    tokens = n_seq * seq
    group = d // N_POOL_GROUPS
    key_dim, val_dim = d // 2, d
    dv = val_dim // GLA_HEADS
    n_main = 2 * key_dim + 2 * val_dim
    proj_shard = gla_w_in.shape[2]
    proj_width = N_DEV * proj_shard
    kd_shard = key_dim // N_DEV
    assert proj_width == n_main + 2 * GATE_RANK and seq % CHUNK == 0 and 2 * GATE_RANK <= LANES
    assert mlp_w_in.shape[0] == 2 and pool_w.shape[0] == 1 and gla_w_in.shape[0] == 1
    me = 4 * lax.axis_index("x") + 2 * lax.axis_index("y") + lax.axis_index("c")
    tm = _tile(tokens, 1024)
    tr = _tile(tokens, 128)
    x2, target = x.reshape(tokens, d), loss_target.reshape(tokens, d)

    small_rows = 2 * GATE_RANK + 4
    small = jnp.concatenate([
        gla_w_up_f[0], gla_w_up_b[0], gla_b_up_f, gla_b_up_b,
        jnp.pad(gla_g_norm, ((0, 0), (0, kd_shard - gla_g_norm.shape[1]))), jnp.zeros((1, kd_shard), F32)], axis=0)
    w1_mine = [mlp_w_in[l].astype(BF16) for l in range(2)]
    w2_mine = [mlp_w_out[l].astype(BF16) for l in range(2)]
    pool_g, small_g, w1_0 = _exchange(
        "gather_first", gathers=[pool_w[0].astype(BF16), small.reshape(2, small_rows // 2, kd_shard), w1_mine[0]])
    small_g = small_g.reshape(N_DEV, small_rows, kd_shard)
    w_pool = pool_g.transpose(1, 0, 2, 3).reshape(N_POOL_GROUPS, group, group)
    small_full = small_g.transpose(1, 0, 2).reshape(small_rows, key_dim)
    pad_rows = lambda w, r0: jnp.pad(w, ((r0, LANES - r0 - GATE_RANK), (0, 0)))
    w_up_f, w_up_b = pad_rows(small_full[:GATE_RANK], 0), pad_rows(small_full[GATE_RANK:2 * GATE_RANK], GATE_RANK)
    b_up_f, b_up_b = small_full[2 * GATE_RANK:2 * GATE_RANK + 1], small_full[2 * GATE_RANK + 1:2 * GATE_RANK + 2]
    g_norm = small_g[:, 2 * GATE_RANK + 2, :gla_g_norm.shape[1]].reshape(1, dv)

    hn0 = _rms_fwd("rms_mix0", x2, norm_mix[0:1], F32, tr)
    pooled = _pool_slabs("pool_fwd", hn0, BF16, n_seq=n_seq, seq=seq, backward=False)
    tn, tk = _tile(group, 1024), _tile(group, TILE_K)
    nper, kper = group // tn, group // tk
    y_pre, h1 = _mm(
        "pool_mix", pooled, w_pool, grid=(tokens // tm, d // tn, group // tk),
        a_spec=_spec2(tm, tk, lambda m, n, k: (m, (n // nper) * kper + k)),
        b_spec=pl.BlockSpec((None, tk, tn), lambda m, n, k: (n // nper, k, n % nper)),
        o_spec=_spec2(tm, tn, lambda m, n, k: (m, n)), acc_shape=(tm, tn), outs=[((tokens, d), F32)] * 2, dims=NN,
        extras=[(x2, _spec2(tm, tn, lambda m, n, k: (m, n))), (pool_scale, _spec2(1, tn, lambda m, n, k: (0, n)))],
        epi=lambda acc, res, s: (acc, res + acc * s))
    hn1, act0, h2, w2_0, (win_g, wout_g) = _mlp_fwd(
        "0", h1, norm_mlp[0:1], w1_0, w2_mine[0], [gla_w_in[0].astype(BF16), gla_w_out[0].astype(BF16)], tm, tr)
    w_in = win_g.transpose(1, 0, 2).reshape(d, proj_width)
    w_main = w_in[:, :n_main]
    w_r = jnp.pad(w_in[:, n_main:], ((0, 0), (0, LANES - 2 * GATE_RANK)))
    w_out = wout_g.reshape(val_dim, d)

    hn2 = _rms_fwd("rms_mix1", h2, norm_mix[1:2], BF16, tr)
    pm, w1_1 = _plain_mm("gla_proj", hn2, w_main, NN, BF16, tm=tm, gathers=[w1_mine[1]])
    pr = _plain_mm("gla_proj_gate", hn2, w_r, NN, F32, tm=tm)
    log_a_f, log_a_b = _gate_fwd("gla_gate", pr, w_up_f, b_up_f, w_up_b, b_up_b, tr)
    dims = dict(n_seq=n_seq, seq=seq, key_dim=key_dim, val_dim=val_dim)
    o_f, st_f = _gla_fwd("gla_scan_f", pm, log_a_f, rev=False, **dims)
    o_b, st_b = _gla_fwd("gla_scan_b", pm, log_a_b, rev=True, **dims)
    gate_block = (2 * key_dim + val_dim) // val_dim
    z = _head_norm_fwd("gla_head_norm", o_f, o_b, pm, g_norm, val_dim=val_dim, gate_block=gate_block, tr=tr)
    h3 = _plain_mm("gla_out", z, w_out, NN, F32, tm=tm, extras=[h2], epi=lambda acc, res: (res + acc,))
    hn3, act1, h4, w2_1, _ = _mlp_fwd("1", h3, norm_mlp[1:2], w1_1, w2_mine[1], [], tm, tr)

    dh4, dh4_16, loss_row, d_norm_final = _loss_head("loss_head", h4, target, norm_final.reshape(1, d), tr)

    dh3, dh3_16, d_norm_mlp1, dw1_1, dw2_1 = _mlp_bwd("1", h3, norm_mlp[1:2], hn3, act1, dh4, dh4_16, w1_1, w2_1, tm, tr)
    dz = _plain_mm("gla_out_dx", dh3_16, w_out, NT, F32, tm=tm)
    dw_out = _plain_mm("gla_out_dw", z, dh3_16, TN, BF16, tm=1024)
    dw_out_sums = _chip_sums("gla_out_dw", dw_out.reshape(N_DEV, val_dim // N_DEV, d))
    d_o, d_gate, d_g_norm = _head_norm_bwd("gla_head_norm_bwd", o_f, o_b, pm, g_norm, dz, val_dim=val_dim,
                                           gate_block=gate_block, tr=tr)
    dq_f, dk_f, dv_f, dla_f = _gla_bwd("gla_scan_bwd_f", pm, log_a_f, st_f, d_o, rev=False, **dims)
    dq_b, dk_b, dv_b, dla_b = _gla_bwd("gla_scan_bwd_b", pm, log_a_b, st_b, d_o, rev=True, **dims)
    dpr, dw_up_f, db_up_f, dw_up_b, db_up_b = _gate_bwd("gla_gate_bwd", pr, w_up_f, b_up_f, w_up_b, b_up_b, dla_f, dla_b, tr)
    dpm = _proj_grad("gla_proj_grad", dq_f, dq_b, dk_f, dk_b, dv_f, dv_b, d_gate, key_dim=key_dim, val_dim=val_dim, tr=tr)
    dw_main, dw_out_landed = _plain_mm("gla_proj_dw", hn2, dpm, TN, BF16, tm=1024, scatters=[dw_out_sums])
    dw_r = _plain_mm("gla_proj_gate_dw", hn2, dpr, TN, BF16, tm=1024)
    dw_in = jnp.concatenate([dw_main, dw_r[:, :2 * GATE_RANK]], axis=1)
    dw_in_sums = _chip_sums("gla_proj_dw", dw_in.reshape(d, N_DEV, proj_shard).transpose(1, 0, 2))
    dhn2, dw_in_landed = _plain_mm("gla_proj_dx", dpm, w_main, NT, F32, tm=tm, scatters=[dw_in_sums])
    dhn2 = _plain_mm("gla_proj_gate_dx", dpr, w_r, NT, F32, tm=tm, extras=[dhn2], epi=lambda acc, res: (res + acc,))
    dh2, dh2_16, d_norm_mix1 = _rms_bwd("rms_mix1_bwd", h2, norm_mix[1:2], dhn2, dh3, tr)

    dh1, _, d_norm_mlp0, dw1_0, dw2_0 = _mlp_bwd("0", h1, norm_mlp[0:1], hn1, act0, dh2, dh2_16, w1_0, w2_0, tm, tr)
    dyp, d_pool_scale = _pool_out_grad("pool_out_grad", dh1, y_pre, pool_scale, tr)
    dpooled = _mm(
        "pool_mix_dx", dyp, w_pool, grid=(tokens // tm, d // tn, group // tk),
        a_spec=_spec2(tm, tk, lambda m, n, k: (m, (n // nper) * kper + k)),
        b_spec=pl.BlockSpec((None, tn, tk), lambda m, n, k: (n // nper, n % nper, k)),
        o_spec=_spec2(tm, tn, lambda m, n, k: (m, n)), acc_shape=(tm, tn), outs=[((tokens, d), F32)], dims=NT)[0]
    tmw, tkt = _tile(group, 1024), _tile(tokens, TILE_K)
    mper = group // tmw
    dw_pool = _mm(
        "pool_mix_dw", pooled, dyp, grid=(d // tmw, group // tn, tokens // tkt),
        a_spec=_spec2(tkt, tmw, lambda m, n, k: (k, m)), b_spec=_spec2(tkt, tn, lambda m, n, k: (k, (m // mper) * nper + n)),
        o_spec=pl.BlockSpec((None, tmw, tn), lambda m, n, k: (m // mper, m % mper, n)), acc_shape=(tmw, tn),
        outs=[((N_POOL_GROUPS, group, group), BF16)], dims=TN)[0]
    dw_pool_sums = _chip_sums("pool_mix_dw", dw_pool.reshape(N_POOL_GROUPS, N_DEV, group // N_DEV, group).transpose(1, 0, 2, 3))
    dw_pool_landed = _exchange("scatter_pool", scatters=[dw_pool_sums])[0]
    dhn0 = _pool_slabs("pool_bwd", dpooled, F32, n_seq=n_seq, seq=seq, backward=True)
    dx, _, d_norm_mix0 = _rms_bwd("rms_mix0_bwd", x2, norm_mix[0:1], dhn0, dh1, tr)

    pack_cols = key_dim
    rows_of = lambda a: a.reshape(-1, pack_cols)
    pieces = [rows_of(t) for t in (d_norm_mix0, d_norm_mix1, d_norm_mlp0, d_norm_mlp1, d_norm_final, d_pool_scale,
                                   dw_up_f[:GATE_RANK], dw_up_b[GATE_RANK:2 * GATE_RANK], db_up_f, db_up_b)]
    pieces += [jnp.pad(d_g_norm, ((0, 0), (0, pack_cols - dv))), jnp.pad(loss_row, ((0, 0), (0, pack_cols - LANES)))]
    n_rows = sum(p.shape[0] for p in pieces)
    pack = jnp.concatenate(pieces + [jnp.zeros((-n_rows % 8, pack_cols), F32)], axis=0)
    red = _all_reduce_small("reduce_small", pack)
    cuts, r0 = [], 0
    for p in pieces:
        cuts.append(red[r0:r0 + p.shape[0]])
        r0 += p.shape[0]
    g_norm_mix = jnp.concatenate([cuts[0].reshape(1, d), cuts[1].reshape(1, d)], axis=0)
    g_norm_mlp = jnp.concatenate([cuts[2].reshape(1, d), cuts[3].reshape(1, d)], axis=0)
    g_norm_final = cuts[4].reshape(d)
    g_pool_scale = cuts[5].reshape(1, d)
    shard_cols = lambda t, width: lax.dynamic_slice_in_dim(t, me * width, width, axis=1)
    g_w_up_f, g_w_up_b = shard_cols(cuts[6], kd_shard)[None], shard_cols(cuts[7], kd_shard)[None]
    g_b_up_f, g_b_up_b = shard_cols(cuts[8], kd_shard), shard_cols(cuts[9], kd_shard)
    g_g_norm = shard_cols(cuts[10][:, :dv], dv // N_DEV)
    loss = cuts[11][0, 0]

    big = {
        "pool_w": _adamw_shard("adamw_pool_w", pool_w, m_pool_w, v_pool_w, (dw_pool_landed, dw_pool_sums)),
        "gla_w_in": _adamw_shard("adamw_gla_w_in", gla_w_in, m_gla_w_in, v_gla_w_in, (dw_in_landed, dw_in_sums)),
        "gla_w_out": _adamw_shard("adamw_gla_w_out", gla_w_out, m_gla_w_out, v_gla_w_out, (dw_out_landed, dw_out_sums)),
    }
    for nm, w, m, v, landed in (("mlp_w_in", mlp_w_in, m_mlp_w_in, v_mlp_w_in, (dw1_0, dw1_1)),
                                ("mlp_w_out", mlp_w_out, m_mlp_w_out, v_mlp_w_out, (dw2_0, dw2_1))):
        layer1 = _adamw_shard(f"adamw_{nm}1", w, m, v, landed[1], layer=1)
        big[nm] = _adamw_shard(f"adamw_{nm}0", w, m, v, landed[0], layer=0, into=layer1)
    grads = {"norm_mix": g_norm_mix, "norm_mlp": g_norm_mlp, "norm_final": g_norm_final, "pool_scale": g_pool_scale,
             "gla_w_up_f": g_w_up_f, "gla_b_up_f": g_b_up_f, "gla_w_up_b": g_w_up_b, "gla_b_up_b": g_b_up_b, "gla_g_norm": g_g_norm}
    state = {"norm_mix": (norm_mix, m_norm_mix, v_norm_mix), "norm_mlp": (norm_mlp, m_norm_mlp, v_norm_mlp),
             "norm_final": (norm_final, m_norm_final, v_norm_final), "pool_scale": (pool_scale, m_pool_scale, v_pool_scale),
             "gla_w_up_f": (gla_w_up_f, m_gla_w_up_f, v_gla_w_up_f), "gla_b_up_f": (gla_b_up_f, m_gla_b_up_f, v_gla_b_up_f),
             "gla_w_up_b": (gla_w_up_b, m_gla_w_up_b, v_gla_w_up_b), "gla_b_up_b": (gla_b_up_b, m_gla_b_up_b, v_gla_b_up_b),
             "gla_g_norm": (gla_g_norm, m_gla_g_norm, v_gla_g_norm)}
    order = ["norm_mix", "norm_mlp", "norm_final", "pool_w", "pool_scale", "gla_w_in", "gla_w_up_f", "gla_b_up_f", "gla_w_up_b",
             "gla_b_up_b", "gla_g_norm", "gla_w_out", "mlp_w_in", "mlp_w_out"]
    out_g, out_d, out_m, out_v = [], [], [], []
    for nm in order:
        if nm in big:
            g, dl, new_m, new_v = big[nm]
        else:
            g = grads[nm]
            dl, new_m, new_v = _adamw_small(f"adamw_{nm}", *state[nm], g)
        out_g.append(g), out_d.append(dl), out_m.append(new_m), out_v.append(new_v)
    return (loss, dx.reshape(x.shape), *out_g, *out_d, *out_m, *out_v)
```

```python
import functools

import jax
import jax.numpy as jnp
from jax import lax
from jax.experimental import pallas as pl
from jax.experimental.pallas import tpu as pltpu

F32 = jnp.float32
BF16 = jnp.bfloat16
N_DEV = 8
N_PEER = N_DEV - 1
LANES = 128
VMEM_LIMIT_BYTES = 48 * 1024 * 1024
MESH_ID = pl.DeviceIdType.MESH
TILE_K = 1024
TILE_M_BIG = 2048

POOL_WINDOWS = (2, 4, 8, 16)
N_POOL_GROUPS = len(POOL_WINDOWS)
GLA_HEADS = 4
GATE_RANK = 16
GATE_TAU = 16.0
CHUNK = 64
EPS = 1e-6
ADAM_LR = 0.001
ADAM_B1 = 0.9
ADAM_B2 = 0.999
ADAM_EPS = 1e-08
ADAM_WD = 0.01
ADAM_STEP = 10

NN = ((1,), (0,))
NT = ((1,), (1,))
TN = ((0,), (0,))


def _tile(dim, pref):
    if dim <= pref:
        return dim
    t = (pref // LANES) * LANES
    while t > LANES and dim % t:
        t -= LANES
    assert dim % t == 0, (dim, pref)
    return t


def _row_tile(rows, pref):
    if rows <= pref:
        return rows
    t = max(8, pref // 8 * 8)
    while t > 8 and rows % t:
        t -= 8
    assert rows % t == 0, (rows, pref)
    return t


def _params(*sem):
    return pltpu.CompilerParams(dimension_semantics=sem, vmem_limit_bytes=VMEM_LIMIT_BYTES)


def _dot(a, b, dims):
    return lax.dot_general(a, b, (dims, ((), ())), preferred_element_type=F32)


def _mm(name, a, b, *, grid, a_spec, b_spec, o_spec, acc_shape, outs, dims, extras=(), epi=None, gathers=(), scatters=()):
    n_ex, n_out, nk = len(extras), len(outs), grid[2]
    n_g, n_s = len(gathers), len(scatters)
    carries = bool(n_g or n_s)

    def body(*refs):
        a_ref, b_ref = refs[0], refs[1]
        ex = refs[2:2 + n_ex]
        r0 = 2 + n_ex
        g_in, s_in = refs[r0:r0 + n_g], refs[r0 + n_g:r0 + n_g + n_s]
        r0 += n_g + n_s
        o_refs = refs[r0:r0 + n_out]
        g_out, s_out = refs[r0 + n_out:r0 + n_out + n_g], refs[r0 + n_out + n_g:r0 + n_out + n_g + n_s]
        acc = refs[r0 + n_out + n_g + n_s]
        i, j, k = pl.program_id(0), pl.program_id(1), pl.program_id(2)
        if carries:
            start, middle, finish = _transfers(g_in, g_out, s_in, s_out, (), (), *refs[-3:])
            step = (i * grid[1] + j) * nk + k

            @pl.when(step == 0)
            def _():
                start()

            if n_g:
                @pl.when(step == (grid[0] * grid[1] * nk * 5) // 8)
                def _():
                    middle()

        def part():
            return _dot(a_ref[...].astype(BF16), b_ref[...].astype(BF16), dims)

        def finish_tile(total):
            res = epi(total, *[e[...] for e in ex]) if epi is not None else (total,)
            for o, r in zip(o_refs, res):
                o[...] = r.astype(o.dtype)

        if nk == 1:
            finish_tile(part())
        else:
            @pl.when(k == 0)
            def _():
                acc[...] = part()

            @pl.when((k > 0) & (k < nk - 1))
            def _():
                acc[...] += part()

            @pl.when(k == nk - 1)
            def _():
                finish_tile(acc[...] + part())

        if carries:
            @pl.when((i == grid[0] - 1) & (j == grid[1] - 1) & (k == nk - 1))
            def _():
                finish()

    hbm = pl.BlockSpec(memory_space=pl.ANY)
    operands = [a, b] + [e for e, _ in extras] + list(gathers) + list(scatters)
    in_specs = [a_spec, b_spec] + [s for _, s in extras] + [hbm] * (n_g + n_s)
    out_shape = [jax.ShapeDtypeStruct(s, d) for s, d in outs]
    out_shape += [jax.ShapeDtypeStruct((N_DEV,) + g.shape, g.dtype) for g in gathers]
    out_shape += [jax.ShapeDtypeStruct((len(CHIP_BITS),) + t.shape[1:], t.dtype) for t in scatters]
    scratch = [pltpu.VMEM(acc_shape, F32)] + (_transfer_sems(n_g, n_s, 0) if carries else [])
    sem = ("arbitrary",) * 3 if carries else ("parallel", "parallel", "arbitrary")
    return pl.pallas_call(
        body, name=name, grid=grid, in_specs=in_specs, out_specs=[o_spec] * n_out + [hbm] * (n_g + n_s),
        out_shape=out_shape, scratch_shapes=scratch, compiler_params=_params(*sem),
    )(*operands)


def _spec2(rows, cols, fn):
    return pl.BlockSpec((rows, cols), fn)


def _rowwise(name, fn, *, n_rows, tr, rows, bcast=(), outs=(), accs=()):
    n_r, n_b, n_o, n_a = len(rows), len(bcast), len(outs), len(accs)

    def body(*refs):
        r_refs, b_refs = refs[:n_r], refs[n_r:n_r + n_b]
        o_refs, a_refs = refs[n_r + n_b:n_r + n_b + n_o], refs[n_r + n_b + n_o:]
        res = fn(*[r[...] for r in r_refs], *[b[...] for b in b_refs])
        for o, r in zip(o_refs, res[:n_o]):
            o[...] = r.astype(o.dtype)
        if n_a:
            @pl.when(pl.program_id(0) == 0)
            def _():
                for acc in a_refs:
                    acc[...] = jnp.zeros_like(acc)

            for acc, r in zip(a_refs, res[n_o:]):
                acc[...] += r

    in_specs = [pl.BlockSpec((tr, w), functools.partial(lambda i, cb: (i, cb), cb=cb)) for _, w, cb in rows]
    in_specs += [pl.BlockSpec(b.shape, lambda i: (0, 0)) for b in bcast]
    out_specs = [pl.BlockSpec((tr, w), lambda i: (i, 0)) for w, _ in outs]
    out_specs += [pl.BlockSpec(s, lambda i: (0, 0)) for s in accs]
    out_shape = [jax.ShapeDtypeStruct((n_rows, w), d) for w, d in outs]
    out_shape += [jax.ShapeDtypeStruct(s, F32) for s in accs]
    return pl.pallas_call(
        body, name=name, grid=(n_rows // tr,), in_specs=in_specs, out_specs=out_specs, out_shape=out_shape,
        compiler_params=_params("arbitrary"),
    )(*[r for r, _, _ in rows], *bcast)


def _colsum(t):
    return jnp.sum(t, axis=0, keepdims=True)


def _rms_fwd(name, h, g, dtype, tr):
    def fn(x, gain):
        r = lax.rsqrt(jnp.mean(x * x, axis=-1, keepdims=True) + EPS)
        return (x * r * gain,)

    d = h.shape[1]
    return _rowwise(name, fn, n_rows=h.shape[0], tr=tr, rows=[(h, d, 0)], bcast=[g], outs=[(d, dtype)])[0]


def _rms_bwd(name, h, g, dhn, dres, tr):
    def fn(x, dy, dr, gain):
        r = lax.rsqrt(jnp.mean(x * x, axis=-1, keepdims=True) + EPS)
        xhat = x * r
        gd = gain * dy
        dh = r * (gd - xhat * jnp.mean(gd * xhat, axis=-1, keepdims=True)) + dr
        return dh, dh, _colsum(dy * xhat)

    d = h.shape[1]
    return _rowwise(name, fn, n_rows=h.shape[0], tr=tr, rows=[(h, d, 0), (dhn, d, 0), (dres, d, 0)], bcast=[g],
                    outs=[(d, F32), (d, BF16)], accs=[(1, d)])


def _loss_head(name, h, target, g, tr):
    d = h.shape[1]

    def fn(x, tgt, gain):
        r = lax.rsqrt(jnp.mean(x * x, axis=-1, keepdims=True) + EPS)
        xhat = x * r
        err = xhat * gain - tgt
        loss = 0.5 * jnp.sum(jnp.mean(err * err, axis=-1, keepdims=True), axis=0, keepdims=True)
        dy = err / d
        gd = gain * dy
        dh = r * (gd - xhat * jnp.mean(gd * xhat, axis=-1, keepdims=True))
        return dh, dh, jnp.broadcast_to(loss, (1, LANES)), _colsum(dy * xhat)

    return _rowwise(name, fn, n_rows=h.shape[0], tr=tr, rows=[(h, d, 0), (target, d, 0)], bcast=[g],
                    outs=[(d, F32), (d, BF16)], accs=[(1, LANES), (1, d)])


def _shift_rows(h, k, seq):
    if k == 0:
        return h
    rolled = pltpu.roll(h, (-k) % seq, 0)
    t = lax.broadcasted_iota(jnp.int32, h.shape, 0)
    return jnp.where((t + k >= 0) & (t + k < seq), rolled, 0.0)


def _window_halves(h, half, seq):
    right = h
    left = _shift_rows(h, -1, seq)
    step = 1
    while step < half:
        right = right + _shift_rows(right, step, seq)
        left = left + _shift_rows(left, -step, seq)
        step *= 2
    return left + right


def _window_count(shape, half, seq):
    t = lax.broadcasted_iota(jnp.int32, shape, 0)
    return (jnp.minimum(t + half, seq) - jnp.maximum(t - half, 0)).astype(F32)


def _pool_slabs(name, src, out_dtype, *, n_seq, seq, backward):
    d = src.shape[1]
    group = d // N_POOL_GROUPS
    tc = _tile(group, 256)

    def body(s_ref, o_ref):
        g = (pl.program_id(1) * tc) // group
        for gi, w in enumerate(POOL_WINDOWS):
            half = w // 2

            @pl.when(g == gi)
            def _():
                v = s_ref[...].astype(F32)
                cnt = _window_count(v.shape, half, seq)
                if backward:
                    u = v / cnt
                    res = (_window_halves(u, half, seq) - _shift_rows(u, -half, seq) + _shift_rows(u, half, seq)) - v
                else:
                    res = _window_halves(v, half, seq) / cnt - v
                o_ref[...] = res.astype(o_ref.dtype)

    spec = pl.BlockSpec((seq, tc), lambda b, j: (b, j))
    return pl.pallas_call(
        body, name=name, grid=(n_seq, d // tc), in_specs=[spec], out_specs=spec,
        out_shape=jax.ShapeDtypeStruct(src.shape, out_dtype), compiler_params=_params("parallel", "parallel"),
    )(src)


def _split3(x):
    x1 = x.astype(BF16)
    r1 = x - x1.astype(F32)
    x2 = r1.astype(BF16)
    x3 = (r1 - x2.astype(F32)).astype(BF16)
    return x1, x2, x3


def _tri_dot(tri, x):
    x1, x2, x3 = _split3(x)
    return _dot(tri, x1, NN) + _dot(tri, x2, NN) + _dot(tri, x3, NN)


def _chunk_terms(q_ref, k_ref, g_ref, rev, scale):
    c = q_ref.shape[0]
    ri = lax.broadcasted_iota(jnp.int32, (c, c), 0)
    ci = lax.broadcasted_iota(jnp.int32, (c, c), 1)
    causal = (ci >= ri) if rev else (ci <= ri)
    mid = c // 2 if rev else c // 2 - 1
    last = 0 if rev else c - 1
    b = _tri_dot(causal.astype(BF16), g_ref[...])
    b_mid, b_last = b[mid:mid + 1, :], b[last:last + 1, :]
    e_qm, e_km, e_qi, e_ks = jnp.exp(b - b_mid), jnp.exp(b_mid - b), jnp.exp(b), jnp.exp(b_last - b)
    q = q_ref[...].astype(F32) * scale
    k = k_ref[...].astype(F32)
    return dict(causal=causal, mid=mid, last=last, decay=jnp.exp(b_last), e_qm=e_qm, e_km=e_km, e_qi=e_qi, e_ks=e_ks,
                qm=q * e_qm, km=k * e_km, qi=q * e_qi, ks=k * e_ks)


HEADS_PER_STEP = 2


def _gla_specs(n_chunks, heads, dk, dv, key_dim, rev, backward):
    hps = HEADS_PER_STEP

    def chunk(n):
        scan_pos = (n_chunks - 1 - n) if backward else n
        return (n_chunks - 1 - scan_pos) if rev else scan_pos

    def at(width, col0):
        return pl.BlockSpec((CHUNK, hps * width), lambda b, h, n: (b * n_chunks + chunk(n), col0 // (hps * width) + h))

    state = pl.BlockSpec((hps, dv, dk), lambda b, h, n: ((b * n_chunks + chunk(n)) * (heads // hps) + h, 0, 0))
    return at(dk, 0), at(dk, key_dim), at(dv, 2 * key_dim), at(dk, 0), at(dv, 0), state


def _gla_fwd(name, pm, log_a, *, rev, n_seq, seq, key_dim, val_dim):
    heads, n_chunks = GLA_HEADS, seq // CHUNK
    dk, dv = key_dim // heads, val_dim // heads
    scale = dk ** -0.5
    q_spec, k_spec, v_spec, g_spec, o_spec, st_spec = _gla_specs(n_chunks, heads, dk, dv, key_dim, rev, False)

    def body(q_ref, k_ref, v_ref, g_ref, o_ref, st_ref, state):
        @pl.when(pl.program_id(2) == 0)
        def _():
            state[...] = jnp.zeros_like(state)

        for i in range(HEADS_PER_STEP):
            kc, vc = pl.ds(i * dk, dk), pl.ds(i * dv, dv)
            t = _chunk_terms(q_ref.at[:, kc], k_ref.at[:, kc], g_ref.at[:, kc], rev, scale)
            v = v_ref[:, vc]
            p = jnp.where(t["causal"], _dot(t["qm"].astype(BF16), t["km"].astype(BF16), NT), 0.0)
            entering = state[i].astype(BF16)
            st_ref[i] = entering
            o_ref[:, vc] = _dot(p.astype(BF16), v, NN) + _dot(t["qi"].astype(BF16), entering, NT)
            state[i] = t["decay"] * state[i] + _dot(v, t["ks"].astype(BF16), TN)

    tokens = n_seq * seq
    return pl.pallas_call(
        body, name=name, grid=(n_seq, heads // HEADS_PER_STEP, n_chunks), in_specs=[q_spec, k_spec, v_spec, g_spec],
        out_specs=[o_spec, st_spec],
        out_shape=[jax.ShapeDtypeStruct((tokens, val_dim), F32),
                   jax.ShapeDtypeStruct((n_seq * n_chunks * heads, dv, dk), BF16)],
        scratch_shapes=[pltpu.VMEM((HEADS_PER_STEP, dv, dk), F32)],
        compiler_params=_params("parallel", "parallel", "arbitrary"),
    )(pm, pm, pm, log_a)


def _gla_bwd(name, pm, log_a, states, d_o, *, rev, n_seq, seq, key_dim, val_dim):
    heads, n_chunks = GLA_HEADS, seq // CHUNK
    dk, dv = key_dim // heads, val_dim // heads
    scale = dk ** -0.5
    q_spec, k_spec, v_spec, g_spec, o_spec, st_spec = _gla_specs(n_chunks, heads, dk, dv, key_dim, rev, True)

    def body(q_ref, k_ref, v_ref, g_ref, do_ref, st_ref, dq_ref, dk_ref, dv_ref, dg_ref, dstate):
        @pl.when(pl.program_id(2) == 0)
        def _():
            dstate[...] = jnp.zeros_like(dstate)

        for i in range(HEADS_PER_STEP):
            kc, vc = pl.ds(i * dk, dk), pl.ds(i * dv, dv)
            t = _chunk_terms(q_ref.at[:, kc], k_ref.at[:, kc], g_ref.at[:, kc], rev, scale)
            causal = t["causal"]
            v, d_out, entering = v_ref[:, vc], do_ref[:, vc], st_ref[i]
            qm, km, qi, ks = (t[n].astype(BF16) for n in ("qm", "km", "qi", "ks"))
            dst = dstate[i]
            dst16 = dst.astype(BF16)
            p = jnp.where(causal, _dot(qm, km, NT), 0.0).astype(BF16)
            dp = jnp.where(causal, _dot(d_out, v, NT), 0.0).astype(BF16)
            dv_ref[:, vc] = _dot(p, d_out, TN) + _dot(ks, dst16, NT)
            dqm, dkm = _dot(dp, km, NN), _dot(dp, qm, TN)
            dqi, dks = _dot(d_out, entering, NN), _dot(v, dst16, NN)
            d_decay = _colsum(dst * entering.astype(F32))
            dq_ref[:, kc] = (dqi * t["e_qi"] + dqm * t["e_qm"]) * scale
            dk_ref[:, kc] = dkm * t["e_km"] + dks * t["e_ks"]
            qm_term, km_term = dqm * t["qm"], dkm * t["km"]
            ks_term = dks * t["ks"]
            db = dqi * t["qi"] + qm_term - km_term - ks_term
            row = lax.broadcasted_iota(jnp.int32, db.shape, 0)
            db = db + jnp.where(row == t["mid"], _colsum(km_term - qm_term), 0.0)
            db = db + jnp.where(row == t["last"], _colsum(ks_term) + d_decay * t["decay"], 0.0)
            ri = lax.broadcasted_iota(jnp.int32, causal.shape, 0)
            ci = lax.broadcasted_iota(jnp.int32, causal.shape, 1)
            anti = (ci <= ri) if rev else (ci >= ri)
            dg_ref[:, kc] = _tri_dot(anti.astype(BF16), db)
            dstate[i] = t["decay"] * dst + _dot(d_out, qi, TN)

    tokens = n_seq * seq
    kd = jax.ShapeDtypeStruct((tokens, key_dim), F32)
    return pl.pallas_call(
        body, name=name, grid=(n_seq, heads // HEADS_PER_STEP, n_chunks),
        in_specs=[q_spec, k_spec, v_spec, g_spec, o_spec, st_spec], out_specs=[g_spec, g_spec, o_spec, g_spec],
        out_shape=[kd, kd, jax.ShapeDtypeStruct((tokens, val_dim), F32), kd],
        scratch_shapes=[pltpu.VMEM((HEADS_PER_STEP, dv, dk), F32)],
        compiler_params=_params("parallel", "parallel", "arbitrary"),
    )(pm, pm, pm, log_a, d_o, states)


def _sigmoid_neg(z):
    e = jnp.exp(-jnp.abs(z))
    return jnp.where(z >= 0, e / (1.0 + e), 1.0 / (1.0 + e))


def _gate_fwd(name, pr, wf, bf, wb, bb, tr):
    kd = wf.shape[1]

    def fn(r, w_f, b_f, w_b, b_b):
        r16 = r.astype(BF16)
        out = []
        for w, b in ((w_f, b_f), (w_b, b_b)):
            z = _dot(r16, w.astype(BF16), NN) + b
            out.append((jnp.minimum(z, 0.0) - jnp.log1p(jnp.exp(-jnp.abs(z)))) / GATE_TAU)
        return tuple(out)

    return _rowwise(name, fn, n_rows=pr.shape[0], tr=tr, rows=[(pr, LANES, 0)], bcast=[wf, bf, wb, bb],
                    outs=[(kd, F32), (kd, F32)])


def _gate_bwd(name, pr, wf, bf, wb, bb, dgf, dgb, tr):
    kd = wf.shape[1]

    def fn(r, d_f, d_b, w_f, b_f, w_b, b_b):
        r16 = r.astype(BF16)
        dpr = jnp.zeros(r.shape, F32)
        sums = []
        for w, b, dg in ((w_f, b_f, d_f), (w_b, b_b, d_b)):
            w16 = w.astype(BF16)
            z = _dot(r16, w16, NN) + b
            dz = dg / GATE_TAU * _sigmoid_neg(z)
            dz16 = dz.astype(BF16)
            dpr = dpr + _dot(dz16, w16, NT)
            sums += [_dot(r16, dz16, TN), _colsum(dz)]
        return (dpr, *sums)

    return _rowwise(name, fn, n_rows=pr.shape[0], tr=tr, rows=[(pr, LANES, 0), (dgf, kd, 0), (dgb, kd, 0)],
                    bcast=[wf, bf, wb, bb], outs=[(LANES, BF16)], accs=[(LANES, kd), (1, kd), (LANES, kd), (1, kd)])


def _head_norm_fwd(name, of, ob, pm, gn, *, val_dim, gate_block, tr):
    dv = val_dim // GLA_HEADS

    def fn(o_f, o_b, gate, gain):
        out = []
        for h in range(GLA_HEADS):
            sl = slice(h * dv, (h + 1) * dv)
            o = o_f[:, sl] + o_b[:, sl]
            gt = gate[:, sl].astype(F32)
            on = o * lax.rsqrt(jnp.mean(o * o, axis=-1, keepdims=True) + EPS) * gain
            out.append(on * (gt * jax.nn.sigmoid(gt)))
        return (jnp.concatenate(out, axis=1),)

    return _rowwise(name, fn, n_rows=of.shape[0], tr=tr, rows=[(of, val_dim, 0), (ob, val_dim, 0), (pm, val_dim, gate_block)],
                    bcast=[gn], outs=[(val_dim, BF16)])[0]


def _head_norm_bwd(name, of, ob, pm, gn, dz, *, val_dim, gate_block, tr):
    dv = val_dim // GLA_HEADS

    def fn(o_f, o_b, gate, d_z, gain):
        d_o, d_gate = [], []
        d_gain = jnp.zeros((1, dv), F32)
        for h in range(GLA_HEADS):
            sl = slice(h * dv, (h + 1) * dv)
            o = o_f[:, sl] + o_b[:, sl]
            gt = gate[:, sl].astype(F32)
            dzh = d_z[:, sl]
            r = lax.rsqrt(jnp.mean(o * o, axis=-1, keepdims=True) + EPS)
            ohat = o * r
            sg = jax.nn.sigmoid(gt)
            d_gate.append(dzh * (ohat * gain) * (sg * (1.0 + gt * (1.0 - sg))))
            don = dzh * (gt * sg)
            d_gain = d_gain + _colsum(don * ohat)
            gd = gain * don
            d_o.append(r * (gd - ohat * jnp.mean(gd * ohat, axis=-1, keepdims=True)))
        return jnp.concatenate(d_o, axis=1), jnp.concatenate(d_gate, axis=1), d_gain

    return _rowwise(name, fn, n_rows=of.shape[0], tr=tr,
                    rows=[(of, val_dim, 0), (ob, val_dim, 0), (pm, val_dim, gate_block), (dz, val_dim, 0)], bcast=[gn],
                    outs=[(val_dim, BF16), (val_dim, BF16)], accs=[(1, dv)])


def _proj_grad(name, dqf, dqb, dkf, dkb, dvf, dvb, dgate, *, key_dim, val_dim, tr):
    def fn(q1, q2, k1, k2, v1, v2, gt):
        return (jnp.concatenate([(q1 + q2).astype(BF16), (k1 + k2).astype(BF16), (v1 + v2).astype(BF16), gt], axis=1),)

    rows = [(dqf, key_dim, 0), (dqb, key_dim, 0), (dkf, key_dim, 0), (dkb, key_dim, 0), (dvf, val_dim, 0),
            (dvb, val_dim, 0), (dgate, val_dim, 0)]
    return _rowwise(name, fn, n_rows=dqf.shape[0], tr=tr, rows=rows, outs=[(2 * key_dim + 2 * val_dim, BF16)])[0]


def _pool_out_grad(name, dh, y_pre, scale, tr):
    def fn(d, y, s):
        return d * s, _colsum(d * y)

    dm = dh.shape[1]
    return _rowwise(name, fn, n_rows=dh.shape[0], tr=tr, rows=[(dh, dm, 0), (y_pre, dm, 0)], bcast=[scale],
                    outs=[(dm, BF16)], accs=[(1, dm)])


def _mesh_place():
    x, y, c = lax.axis_index("x"), lax.axis_index("y"), lax.axis_index("c")
    return x, y, c, 4 * x + 2 * y + c


def _peer(x, y, c, p):
    px, py, pc = (x if not p & 4 else 1 - x), (y if not p & 2 else 1 - y), (c if not p & 1 else 1 - c)
    return (px, py, pc), 4 * px + 2 * py + pc


CHIP_BITS = (2, 4, 6)
N_CHIP = N_DEV // 2
GATHER_COPIES = 8
FLIP_C, FLIP_Y, FLIP_X = 1, 2, 4


def _transfer_sems(n_gather, n_scatter, n_pair):
    n_remote = n_gather * GATHER_COPIES + n_scatter * len(CHIP_BITS) + n_pair * N_CHIP
    return [pltpu.SemaphoreType.DMA((n_remote,)), pltpu.SemaphoreType.DMA((n_remote,)),
            pltpu.SemaphoreType.DMA((max(n_gather, 1),))]


def _transfers(g_in, g_out, s_in, s_out, p_in, p_out, send, recv, loc):
    x, y, c, me = _mesh_place()
    sibling, _ = _peer(x, y, c, FLIP_C)

    def remote(src, dst, k, peer):
        return pltpu.make_async_remote_copy(src_ref=src, dst_ref=dst, send_sem=send.at[k], recv_sem=recv.at[k],
                                            device_id=peer, device_id_type=MESH_ID)

    local, first, arrivals = [], [], []
    passed, passed_arrivals, last, last_arrivals = [], [], [], []
    k = 0
    for src, out in zip(g_in, g_out):
        half = src.shape[0] // 2
        halves = (pl.ds(0, half), pl.ds(half, half))
        x_chip, x_block = _peer(x, y, c, FLIP_X)
        y_chip, y_block = _peer(x, y, c, FLIP_Y)
        _, far_block = _peer(x, y, c, FLIP_X | FLIP_Y)
        local.append(pltpu.make_async_copy(src, out.at[me], loc.at[len(local)]))
        for peer, block in ((sibling, me ^ FLIP_C), (x_chip, x_block), (y_chip, y_block)):
            first.append(remote(src, out.at[me], k, peer))
            arrivals.append(remote(src, out.at[block], k, peer))
            k += 1
        for peer, block, part in ((x_chip, y_block, halves[0]), (y_chip, x_block, halves[1])):
            passed.append(remote(out.at[block, part], out.at[block, part], k, peer))
            passed_arrivals.append(remote(out.at[far_block, part], out.at[far_block, part], k, peer))
            k += 1
        for block in (x_block, y_block):
            passed.append(remote(out.at[block], out.at[block], k, sibling))
            passed_arrivals.append(remote(out.at[block ^ FLIP_C], out.at[block ^ FLIP_C], k, sibling))
            k += 1
        last.append(remote(out.at[far_block], out.at[far_block], k, sibling))
        last_arrivals.append(remote(out.at[far_block ^ FLIP_C], out.at[far_block ^ FLIP_C], k, sibling))
        k += 1
    n_gather_first = len(first)
    for src, out in zip(s_in, s_out):
        for j, p in enumerate(CHIP_BITS):
            peer, _ = _peer(x, y, c, p)
            peer_chip = 2 * peer[0] + peer[1]
            first.append(remote(src.at[peer_chip], out.at[j], k, peer))
            arrivals.append(remote(src.at[peer_chip], out.at[j], k, peer))
            k += 1
    for src, out in zip(p_in, p_out):
        for q in range(N_CHIP):
            first.append(remote(src.at[2 * q + 1 - c], out.at[q], k, sibling))
            arrivals.append(remote(src.at[2 * q + 1 - c], out.at[q], k, sibling))
            k += 1

    def start():
        for cp in local + first:
            cp.start()

    def middle():
        for cp in arrivals[:n_gather_first]:
            cp.wait_recv()
        for cp in passed:
            cp.start()

    def finish():
        for cp in passed_arrivals:
            cp.wait_recv()
        for cp in last:
            cp.start()
        for cp in last_arrivals + arrivals[n_gather_first:]:
            cp.wait_recv()
        for cp in first + passed + last:
            cp.wait_send()
        for cp in local:
            cp.wait()

    return start, middle, finish


def _exchange(name, gathers=(), scatters=(), pairs=()):
    n_g, n_s, n_p = len(gathers), len(scatters), len(pairs)
    n_in = n_g + n_s + n_p

    def body(*refs):
        ins, outs = refs[:n_in], refs[n_in:-3]
        start, middle, finish = _transfers(ins[:n_g], outs[:n_g], ins[n_g:n_g + n_s], outs[n_g:n_g + n_s], ins[n_g + n_s:],
                                           outs[n_g + n_s:], *refs[-3:])
        start()
        middle()
        finish()

    hbm = pl.BlockSpec(memory_space=pl.ANY)
    out_shape = [jax.ShapeDtypeStruct((N_DEV,) + g.shape, g.dtype) for g in gathers]
    out_shape += [jax.ShapeDtypeStruct((len(CHIP_BITS),) + t.shape[1:], t.dtype) for t in scatters]
    out_shape += [jax.ShapeDtypeStruct((N_CHIP,) + t.shape[1:], t.dtype) for t in pairs]
    return pl.pallas_call(
        body, name=name, in_specs=[hbm] * n_in, out_specs=[hbm] * len(out_shape), out_shape=out_shape,
        scratch_shapes=_transfer_sems(n_g, n_s, n_p),
    )(*gathers, *scatters, *pairs)


def _place_scalar(axis_value):
    return jnp.asarray(axis_value, jnp.int32).reshape(1)


def _chip_sums(name, grad):
    received = _exchange(name + "_pair", pairs=[grad])[0]
    cols = grad.shape[-1]
    rows = received.size // cols // N_CHIP
    tr = _row_tile(rows, (512 * 1024) // cols)

    def body(core_ref, mine_ref, got_ref, out_ref):
        out_ref[...] = (mine_ref[...].astype(F32) + got_ref[...].astype(F32)).astype(out_ref.dtype)

    block = (None, tr, cols)
    out = pl.pallas_call(
        body, name=name + "_add", out_shape=jax.ShapeDtypeStruct((N_CHIP, rows, cols), grad.dtype),
        grid_spec=pltpu.PrefetchScalarGridSpec(
            num_scalar_prefetch=1, grid=(N_CHIP, rows // tr),
            in_specs=[pl.BlockSpec(block, lambda q, i, core: (2 * q + core[0], i, 0)),
                      pl.BlockSpec(block, lambda q, i, core: (q, i, 0))],
            out_specs=pl.BlockSpec(block, lambda q, i, core: (q, i, 0))),
        compiler_params=_params("parallel", "parallel"),
    )(_place_scalar(lax.axis_index("c")), grad.reshape(N_DEV, rows, cols), received.reshape(N_CHIP, rows, cols))
    return out.reshape(received.shape)


def _all_reduce_small(name, pack):
    def body(in_ref, out_ref, parts, send_sems, recv_sems):
        x, y, c, me = _mesh_place()
        parts[me] = in_ref[...]

        def copy(p, arriving):
            peer, peer_block = _peer(x, y, c, p)
            return pltpu.make_async_remote_copy(src_ref=in_ref, dst_ref=parts.at[peer_block if arriving else me],
                                                send_sem=send_sems.at[p - 1], recv_sem=recv_sems.at[p - 1],
                                                device_id=peer, device_id_type=MESH_ID)

        sends = [copy(p, False) for p in range(1, N_DEV)]
        for cp in sends:
            cp.start()
        for p in range(1, N_DEV):
            copy(p, True).wait_recv()
        for cp in sends:
            cp.wait_send()
        total = parts[0]
        for j in range(1, N_DEV):
            total = total + parts[j]
        out_ref[...] = total

    vmem = pl.BlockSpec(memory_space=pltpu.VMEM)
    return pl.pallas_call(
        body, name=name, in_specs=[vmem], out_specs=vmem, out_shape=jax.ShapeDtypeStruct(pack.shape, F32),
        scratch_shapes=[pltpu.VMEM((N_DEV,) + pack.shape, F32), pltpu.SemaphoreType.DMA((N_PEER,)),
                        pltpu.SemaphoreType.DMA((N_PEER,))],
        compiler_params=pltpu.CompilerParams(vmem_limit_bytes=VMEM_LIMIT_BYTES),
    )(pack)


def _adamw_math(w, g, m, v):
    m = ADAM_B1 * m + (1.0 - ADAM_B1) * g
    v = ADAM_B2 * v + (1.0 - ADAM_B2) * jnp.square(g)
    m_hat = m / (1.0 - ADAM_B1 ** ADAM_STEP)
    v_hat = v / (1.0 - ADAM_B2 ** ADAM_STEP)
    delta = -ADAM_LR * (m_hat / (jnp.sqrt(v_hat) + ADAM_EPS) + ADAM_WD * w)
    return delta, m, v


def _adamw_shard(name, w, m, v, grad_parts, layer=0, into=None):
    landed, sums = grad_parts
    shape = w.shape
    n_layers, cols = shape[0], shape[-1]
    rows = w.size // cols // n_layers
    tr = _row_tile(rows, (256 * 1024) // cols)
    n_into = 0 if into is None else 4
    n_landed = landed.shape[0]

    def body(chip_ref, w_ref, m_ref, v_ref, mine_ref, got_ref, *refs):
        g_ref, d_ref, nm_ref, nv_ref = refs[n_into:]
        g = mine_ref[...].astype(F32)
        for j in range(n_landed):
            g = g + got_ref[j].astype(F32)
        delta, new_m, new_v = _adamw_math(w_ref[...], g, m_ref[...], v_ref[...])
        g_ref[...], d_ref[...], nm_ref[...], nv_ref[...] = g, delta, new_m, new_v

    if n_layers == 1:
        view = (rows, cols)
        spec = pl.BlockSpec((tr, cols), lambda i, chip: (i, 0))
    else:
        view = (n_layers, rows, cols)
        spec = pl.BlockSpec((None, tr, cols), lambda i, chip: (layer, i, 0))
    operands = [w.reshape(view), m.reshape(view), v.reshape(view), sums.reshape(N_CHIP, rows, cols),
                landed.reshape(n_landed, rows, cols)]
    in_specs = [spec, spec, spec, pl.BlockSpec((None, tr, cols), lambda i, chip: (chip[0], i, 0)),
                pl.BlockSpec((n_landed, tr, cols), lambda i, chip: (0, i, 0))]
    if into is not None:
        operands += [t.reshape(view) for t in into]
        in_specs += [pl.BlockSpec(memory_space=pl.ANY)] * 4
    first_into = 1 + 5
    res = pl.pallas_call(
        body, name=name, out_shape=[jax.ShapeDtypeStruct(view, F32)] * 4,
        grid_spec=pltpu.PrefetchScalarGridSpec(num_scalar_prefetch=1, grid=(rows // tr,), in_specs=in_specs, out_specs=[spec] * 4),
        input_output_aliases={first_into + t: t for t in range(n_into)}, compiler_params=_params("parallel"),
    )(_place_scalar(2 * lax.axis_index("x") + lax.axis_index("y")), *operands)
    return [r.reshape(shape) for r in res]


def _adamw_small(name, w, m, v, g):
    shape = w.shape
    two_d = (1, shape[0]) if len(shape) == 1 else (w.size // shape[-1], shape[-1])

    def body(w_ref, m_ref, v_ref, g_ref, d_ref, nm_ref, nv_ref):
        d_ref[...], nm_ref[...], nv_ref[...] = _adamw_math(w_ref[...], g_ref[...], m_ref[...], v_ref[...])

    res = pl.pallas_call(body, name=name, out_shape=[jax.ShapeDtypeStruct(two_d, F32)] * 3)(
        w.reshape(two_d), m.reshape(two_d), v.reshape(two_d), g.reshape(two_d))
    return [r.reshape(shape) for r in res]


def _mlp_fwd(tag, h_in, gain, w1, w2_mine, down_gathers, tm, tr):
    tokens, d = h_in.shape
    fs = w1.shape[2]
    f = N_DEV * fs
    hn = _rms_fwd(f"rms_mlp{tag}", h_in, gain, BF16, tr)
    tn, tk = _tile(fs, 1024), _tile(d, TILE_K)
    per = fs // tn
    tmu = _tile(tokens, TILE_M_BIG)
    act, w2 = _mm(
        f"mlp_up{tag}", hn, w1, grid=(tokens // tmu, f // tn, d // tk), a_spec=_spec2(tmu, tk, lambda m, n, k: (m, k)),
        b_spec=pl.BlockSpec((None, tk, tn), lambda m, n, k: (n // per, k, n % per)),
        o_spec=_spec2(tmu, tn, lambda m, n, k: (m, n)), acc_shape=(tmu, tn), outs=[((tokens, f), BF16)], dims=NN,
        epi=lambda acc: (jnp.square(jnp.maximum(acc, 0.0)),), gathers=[w2_mine])
    tn2, tk2 = _tile(d, 1024), _tile(fs, TILE_K)
    kper = fs // tk2
    h_out, *got = _mm(
        f"mlp_down{tag}", act, w2, grid=(tokens // tm, d // tn2, f // tk2), a_spec=_spec2(tm, tk2, lambda m, n, k: (m, k)),
        b_spec=pl.BlockSpec((None, tk2, tn2), lambda m, n, k: (k // kper, k % kper, n)),
        o_spec=_spec2(tm, tn2, lambda m, n, k: (m, n)), acc_shape=(tm, tn2), outs=[((tokens, d), F32)], dims=NN,
        extras=[(h_in, _spec2(tm, tn2, lambda m, n, k: (m, n)))], epi=lambda acc, res: (res + acc,), gathers=down_gathers)
    return hn, act, h_out, w2, got


def _mlp_bwd(tag, h_in, gain, hn, act, dh, dh16, w1, w2, tm, tr):
    tokens, d = h_in.shape
    fs = w1.shape[2]
    f = N_DEV * fs
    tn, tk = _tile(fs, 1024), _tile(d, TILE_K)
    nper = fs // tn
    tmb = _tile(tokens, TILE_M_BIG)
    da = _mm(
        f"mlp_dact{tag}", dh16, w2, grid=(tokens // tmb, f // tn, d // tk), a_spec=_spec2(tmb, tk, lambda m, n, k: (m, k)),
        b_spec=pl.BlockSpec((None, tn, tk), lambda m, n, k: (n // nper, n % nper, k)),
        o_spec=_spec2(tmb, tn, lambda m, n, k: (m, n)), acc_shape=(tmb, tn), outs=[((tokens, f), BF16)], dims=NT,
        extras=[(act, _spec2(tmb, tn, lambda m, n, k: (m, n)))],
        epi=lambda acc, a2: (acc * (2.0 * jnp.sqrt(a2.astype(F32))),))[0]
    tkt = _tile(tokens, TILE_K)
    tm2, tn2 = _tile(fs, TILE_M_BIG), _tile(d, 1024)
    mper = fs // tm2
    dw2 = _mm(
        f"mlp_dw_out{tag}", act, dh16, grid=(f // tm2, d // tn2, tokens // tkt), a_spec=_spec2(tkt, tm2, lambda m, n, k: (k, m)),
        b_spec=_spec2(tkt, tn2, lambda m, n, k: (k, n)),
        o_spec=pl.BlockSpec((None, tm2, tn2), lambda m, n, k: (m // mper, m % mper, n)), acc_shape=(tm2, tn2),
        outs=[((N_DEV, fs, d), BF16)], dims=TN)[0]
    dw2_sums = _chip_sums(f"mlp_dw_out{tag}", dw2)
    tm3 = _tile(d, TILE_M_BIG)
    dw1, dw2_landed = _mm(
        f"mlp_dw_in{tag}", hn, da, grid=(d // tm3, f // tn, tokens // tkt), a_spec=_spec2(tkt, tm3, lambda m, n, k: (k, m)),
        b_spec=_spec2(tkt, tn, lambda m, n, k: (k, n)),
        o_spec=pl.BlockSpec((None, tm3, tn), lambda m, n, k: (n // nper, m, n % nper)), acc_shape=(tm3, tn),
        outs=[((N_DEV, d, fs), BF16)], dims=TN, scatters=[dw2_sums])
    dw1_sums = _chip_sums(f"mlp_dw_in{tag}", dw1)
    tn4, tk4 = _tile(d, 1024), _tile(fs, TILE_K)
    kper = fs // tk4
    dhn, dw1_landed = _mm(
        f"mlp_dx{tag}", da, w1, grid=(tokens // tmb, d // tn4, f // tk4), a_spec=_spec2(tmb, tk4, lambda m, n, k: (m, k)),
        b_spec=pl.BlockSpec((None, tn4, tk4), lambda m, n, k: (k // kper, n, k % kper)),
        o_spec=_spec2(tmb, tn4, lambda m, n, k: (m, n)), acc_shape=(tmb, tn4), outs=[((tokens, d), F32)], dims=NT,
        scatters=[dw1_sums])
    dh_in, dh_in16, dgain = _rms_bwd(f"rms_mlp_bwd{tag}", h_in, gain, dhn, dh, tr)
    return dh_in, dh_in16, dgain, (dw1_landed, dw1_sums), (dw2_landed, dw2_sums)


def _plain_mm(name, a, b, dims, out_dtype, *, tm, tn_pref=1024, tk_pref=TILE_K, extras=None, epi=None, gathers=(), scatters=()):
    if dims == NN:
        (m, k), n = a.shape, b.shape[1]
    elif dims == NT:
        (m, k), n = a.shape, b.shape[0]
    else:
        (k, m), n = a.shape, b.shape[1]
    tm, tn, tk = _tile(m, tm), _tile(n, tn_pref), _tile(k, tk_pref)
    a_spec = _spec2(tk, tm, lambda i, j, l: (l, i)) if dims == TN else _spec2(tm, tk, lambda i, j, l: (i, l))
    b_spec = _spec2(tn, tk, lambda i, j, l: (j, l)) if dims == NT else _spec2(tk, tn, lambda i, j, l: (l, j))
    o_spec = _spec2(tm, tn, lambda i, j, l: (i, j))
    ex = [(e, o_spec) for e in (extras or [])]
    res = _mm(name, a, b, grid=(m // tm, n // tn, k // tk), a_spec=a_spec, b_spec=b_spec, o_spec=o_spec, acc_shape=(tm, tn),
              outs=[((m, n), out_dtype)], dims=dims, extras=ex, epi=epi, gathers=gathers, scatters=scatters)
    return res if (gathers or scatters) else res[0]


def kernel(x, norm_mix, norm_mlp, norm_final, pool_w, pool_scale, gla_w_in, gla_w_up_f, gla_b_up_f, gla_w_up_b, gla_b_up_b, gla_g_norm, gla_w_out, mlp_w_in, mlp_w_out, loss_target, m_norm_mix, m_norm_mlp, m_norm_final, m_pool_w, m_pool_scale, m_gla_w_in, m_gla_w_up_f, m_gla_b_up_f, m_gla_w_up_b, m_gla_b_up_b, m_gla_g_norm, m_gla_w_out, m_mlp_w_in, m_mlp_w_out, v_norm_mix, v_norm_mlp, v_norm_final, v_pool_w, v_pool_scale, v_gla_w_in, v_gla_w_up_f, v_gla_b_up_f, v_gla_w_up_b, v_gla_b_up_b, v_gla_g_norm, v_gla_w_out, v_mlp_w_in, v_mlp_w_out):
    n_seq, seq, d = x.shape
    tokens = n_seq * seq
    group = d // N_POOL_GROUPS
    key_dim, val_dim = d // 2, d
    dv = val_dim // GLA_HEADS
    n_main = 2 * key_dim + 2 * val_dim
    proj_shard = gla_w_in.shape[2]
    proj_width = N_DEV * proj_shard
    kd_shard = key_dim // N_DEV
    assert proj_width == n_main + 2 * GATE_RANK and seq % CHUNK == 0 and 2 * GATE_RANK <= LANES
    assert mlp_w_in.shape[0] == 2 and pool_w.shape[0] == 1 and gla_w_in.shape[0] == 1
    me = 4 * lax.axis_index("x") + 2 * lax.axis_index("y") + lax.axis_index("c")
    tm = _tile(tokens, 1024)
    tr = _tile(tokens, 128)
    x2, target = x.reshape(tokens, d), loss_target.reshape(tokens, d)

    small_rows = 2 * GATE_RANK + 4
    small = jnp.concatenate([
        gla_w_up_f[0], gla_w_up_b[0], gla_b_up_f, gla_b_up_b,
        jnp.pad(gla_g_norm, ((0, 0), (0, kd_shard - gla_g_norm.shape[1]))), jnp.zeros((1, kd_shard), F32)], axis=0)
    w1_mine = [mlp_w_in[l].astype(BF16) for l in range(2)]
    w2_mine = [mlp_w_out[l].astype(BF16) for l in range(2)]
    pool_g, small_g, w1_0 = _exchange(
        "gather_first", gathers=[pool_w[0].astype(BF16), small.reshape(2, small_rows // 2, kd_shard), w1_mine[0]])
    small_g = small_g.reshape(N_DEV, small_rows, kd_shard)
    w_pool = pool_g.transpose(1, 0, 2, 3).reshape(N_POOL_GROUPS, group, group)
    small_full = small_g.transpose(1, 0, 2).reshape(small_rows, key_dim)
    pad_rows = lambda w, r0: jnp.pad(w, ((r0, LANES - r0 - GATE_RANK), (0, 0)))
    w_up_f, w_up_b = pad_rows(small_full[:GATE_RANK], 0), pad_rows(small_full[GATE_RANK:2 * GATE_RANK], GATE_RANK)
    b_up_f, b_up_b = small_full[2 * GATE_RANK:2 * GATE_RANK + 1], small_full[2 * GATE_RANK + 1:2 * GATE_RANK + 2]
    g_norm = small_g[:, 2 * GATE_RANK + 2, :gla_g_norm.shape[1]].reshape(1, dv)

    hn0 = _rms_fwd("rms_mix0", x2, norm_mix[0:1], F32, tr)
    pooled = _pool_slabs("pool_fwd", hn0, BF16, n_seq=n_seq, seq=seq, backward=False)
    tn, tk = _tile(group, 1024), _tile(group, TILE_K)
    nper, kper = group // tn, group // tk
    y_pre, h1 = _mm(
        "pool_mix", pooled, w_pool, grid=(tokens // tm, d // tn, group // tk),
        a_spec=_spec2(tm, tk, lambda m, n, k: (m, (n // nper) * kper + k)),
        b_spec=pl.BlockSpec((None, tk, tn), lambda m, n, k: (n // nper, k, n % nper)),
        o_spec=_spec2(tm, tn, lambda m, n, k: (m, n)), acc_shape=(tm, tn), outs=[((tokens, d), F32)] * 2, dims=NN,
        extras=[(x2, _spec2(tm, tn, lambda m, n, k: (m, n))), (pool_scale, _spec2(1, tn, lambda m, n, k: (0, n)))],
        epi=lambda acc, res, s: (acc, res + acc * s))
    hn1, act0, h2, w2_0, (win_g, wout_g) = _mlp_fwd(
        "0", h1, norm_mlp[0:1], w1_0, w2_mine[0], [gla_w_in[0].astype(BF16), gla_w_out[0].astype(BF16)], tm, tr)
    w_in = win_g.transpose(1, 0, 2).reshape(d, proj_width)
    w_main = w_in[:, :n_main]
    w_r = jnp.pad(w_in[:, n_main:], ((0, 0), (0, LANES - 2 * GATE_RANK)))
    w_out = wout_g.reshape(val_dim, d)

    hn2 = _rms_fwd("rms_mix1", h2, norm_mix[1:2], BF16, tr)
    pm, w1_1 = _plain_mm("gla_proj", hn2, w_main, NN, BF16, tm=TILE_M_BIG, gathers=[w1_mine[1]])
    pr = _plain_mm("gla_proj_gate", hn2, w_r, NN, F32, tm=tm)
    log_a_f, log_a_b = _gate_fwd("gla_gate", pr, w_up_f, b_up_f, w_up_b, b_up_b, tr)
    dims = dict(n_seq=n_seq, seq=seq, key_dim=key_dim, val_dim=val_dim)
    o_f, st_f = _gla_fwd("gla_scan_f", pm, log_a_f, rev=False, **dims)
    o_b, st_b = _gla_fwd("gla_scan_b", pm, log_a_b, rev=True, **dims)
    gate_block = (2 * key_dim + val_dim) // val_dim
    z = _head_norm_fwd("gla_head_norm", o_f, o_b, pm, g_norm, val_dim=val_dim, gate_block=gate_block, tr=tr)
    h3 = _plain_mm("gla_out", z, w_out, NN, F32, tm=tm, extras=[h2], epi=lambda acc, res: (res + acc,))
    hn3, act1, h4, w2_1, _ = _mlp_fwd("1", h3, norm_mlp[1:2], w1_1, w2_mine[1], [], tm, tr)

    dh4, dh4_16, loss_row, d_norm_final = _loss_head("loss_head", h4, target, norm_final.reshape(1, d), tr)

    dh3, dh3_16, d_norm_mlp1, dw1_1, dw2_1 = _mlp_bwd("1", h3, norm_mlp[1:2], hn3, act1, dh4, dh4_16, w1_1, w2_1, tm, tr)
    dz = _plain_mm("gla_out_dx", dh3_16, w_out, NT, F32, tm=TILE_M_BIG)
    dw_out = _plain_mm("gla_out_dw", z, dh3_16, TN, BF16, tm=TILE_M_BIG)
    dw_out_sums = _chip_sums("gla_out_dw", dw_out.reshape(N_DEV, val_dim // N_DEV, d))
    d_o, d_gate, d_g_norm = _head_norm_bwd("gla_head_norm_bwd", o_f, o_b, pm, g_norm, dz, val_dim=val_dim,
                                           gate_block=gate_block, tr=tr)
    dq_f, dk_f, dv_f, dla_f = _gla_bwd("gla_scan_bwd_f", pm, log_a_f, st_f, d_o, rev=False, **dims)
    dq_b, dk_b, dv_b, dla_b = _gla_bwd("gla_scan_bwd_b", pm, log_a_b, st_b, d_o, rev=True, **dims)
    dpr, dw_up_f, db_up_f, dw_up_b, db_up_b = _gate_bwd("gla_gate_bwd", pr, w_up_f, b_up_f, w_up_b, b_up_b, dla_f, dla_b, tr)
    dpm = _proj_grad("gla_proj_grad", dq_f, dq_b, dk_f, dk_b, dv_f, dv_b, d_gate, key_dim=key_dim, val_dim=val_dim, tr=tr)
    dw_main, dw_out_landed = _plain_mm("gla_proj_dw", hn2, dpm, TN, BF16, tm=TILE_M_BIG, scatters=[dw_out_sums])
    dw_r = _plain_mm("gla_proj_gate_dw", hn2, dpr, TN, BF16, tm=1024)
    dw_in = jnp.concatenate([dw_main, dw_r[:, :2 * GATE_RANK]], axis=1)
    dw_in_sums = _chip_sums("gla_proj_dw", dw_in.reshape(d, N_DEV, proj_shard).transpose(1, 0, 2))
    dhn2, dw_in_landed = _plain_mm("gla_proj_dx", dpm, w_main, NT, F32, tm=TILE_M_BIG, scatters=[dw_in_sums])
    dhn2 = _plain_mm("gla_proj_gate_dx", dpr, w_r, NT, F32, tm=tm, extras=[dhn2], epi=lambda acc, res: (res + acc,))
    dh2, dh2_16, d_norm_mix1 = _rms_bwd("rms_mix1_bwd", h2, norm_mix[1:2], dhn2, dh3, tr)

    dh1, _, d_norm_mlp0, dw1_0, dw2_0 = _mlp_bwd("0", h1, norm_mlp[0:1], hn1, act0, dh2, dh2_16, w1_0, w2_0, tm, tr)
    dyp, d_pool_scale = _pool_out_grad("pool_out_grad", dh1, y_pre, pool_scale, tr)
    dpooled = _mm(
        "pool_mix_dx", dyp, w_pool, grid=(tokens // tm, d // tn, group // tk),
        a_spec=_spec2(tm, tk, lambda m, n, k: (m, (n // nper) * kper + k)),
        b_spec=pl.BlockSpec((None, tn, tk), lambda m, n, k: (n // nper, n % nper, k)),
        o_spec=_spec2(tm, tn, lambda m, n, k: (m, n)), acc_shape=(tm, tn), outs=[((tokens, d), F32)], dims=NT)[0]
    tmw, tkt = _tile(group, 1024), _tile(tokens, TILE_K)
    mper = group // tmw
    dw_pool = _mm(
        "pool_mix_dw", pooled, dyp, grid=(d // tmw, group // tn, tokens // tkt),
        a_spec=_spec2(tkt, tmw, lambda m, n, k: (k, m)), b_spec=_spec2(tkt, tn, lambda m, n, k: (k, (m // mper) * nper + n)),
        o_spec=pl.BlockSpec((None, tmw, tn), lambda m, n, k: (m // mper, m % mper, n)), acc_shape=(tmw, tn),
        outs=[((N_POOL_GROUPS, group, group), BF16)], dims=TN)[0]
    dw_pool_sums = _chip_sums("pool_mix_dw", dw_pool.reshape(N_POOL_GROUPS, N_DEV, group // N_DEV, group).transpose(1, 0, 2, 3))
    dw_pool_landed = _exchange("scatter_pool", scatters=[dw_pool_sums])[0]
    dhn0 = _pool_slabs("pool_bwd", dpooled, F32, n_seq=n_seq, seq=seq, backward=True)
    dx, _, d_norm_mix0 = _rms_bwd("rms_mix0_bwd", x2, norm_mix[0:1], dhn0, dh1, tr)

    pack_cols = key_dim
    rows_of = lambda a: a.reshape(-1, pack_cols)
    pieces = [rows_of(t) for t in (d_norm_mix0, d_norm_mix1, d_norm_mlp0, d_norm_mlp1, d_norm_final, d_pool_scale,
                                   dw_up_f[:GATE_RANK], dw_up_b[GATE_RANK:2 * GATE_RANK], db_up_f, db_up_b)]
    pieces += [jnp.pad(d_g_norm, ((0, 0), (0, pack_cols - dv))), jnp.pad(loss_row, ((0, 0), (0, pack_cols - LANES)))]
    n_rows = sum(p.shape[0] for p in pieces)
    pack = jnp.concatenate(pieces + [jnp.zeros((-n_rows % 8, pack_cols), F32)], axis=0)
    red = _all_reduce_small("reduce_small", pack)
    cuts, r0 = [], 0
    for p in pieces:
        cuts.append(red[r0:r0 + p.shape[0]])
        r0 += p.shape[0]
    g_norm_mix = jnp.concatenate([cuts[0].reshape(1, d), cuts[1].reshape(1, d)], axis=0)
    g_norm_mlp = jnp.concatenate([cuts[2].reshape(1, d), cuts[3].reshape(1, d)], axis=0)
    g_norm_final = cuts[4].reshape(d)
    g_pool_scale = cuts[5].reshape(1, d)
    shard_cols = lambda t, width: lax.dynamic_slice_in_dim(t, me * width, width, axis=1)
    g_w_up_f, g_w_up_b = shard_cols(cuts[6], kd_shard)[None], shard_cols(cuts[7], kd_shard)[None]
    g_b_up_f, g_b_up_b = shard_cols(cuts[8], kd_shard), shard_cols(cuts[9], kd_shard)
    g_g_norm = shard_cols(cuts[10][:, :dv], dv // N_DEV)
    loss = cuts[11][0, 0]

    big = {
        "pool_w": _adamw_shard("adamw_pool_w", pool_w, m_pool_w, v_pool_w, (dw_pool_landed, dw_pool_sums)),
        "gla_w_in": _adamw_shard("adamw_gla_w_in", gla_w_in, m_gla_w_in, v_gla_w_in, (dw_in_landed, dw_in_sums)),
        "gla_w_out": _adamw_shard("adamw_gla_w_out", gla_w_out, m_gla_w_out, v_gla_w_out, (dw_out_landed, dw_out_sums)),
    }
    for nm, w, m, v, landed in (("mlp_w_in", mlp_w_in, m_mlp_w_in, v_mlp_w_in, (dw1_0, dw1_1)),
                                ("mlp_w_out", mlp_w_out, m_mlp_w_out, v_mlp_w_out, (dw2_0, dw2_1))):
        layer1 = _adamw_shard(f"adamw_{nm}1", w, m, v, landed[1], layer=1)
        big[nm] = _adamw_shard(f"adamw_{nm}0", w, m, v, landed[0], layer=0, into=layer1)
    grads = {"norm_mix": g_norm_mix, "norm_mlp": g_norm_mlp, "norm_final": g_norm_final, "pool_scale": g_pool_scale,
             "gla_w_up_f": g_w_up_f, "gla_b_up_f": g_b_up_f, "gla_w_up_b": g_w_up_b, "gla_b_up_b": g_b_up_b, "gla_g_norm": g_g_norm}
    state = {"norm_mix": (norm_mix, m_norm_mix, v_norm_mix), "norm_mlp": (norm_mlp, m_norm_mlp, v_norm_mlp),
             "norm_final": (norm_final, m_norm_final, v_norm_final), "pool_scale": (pool_scale, m_pool_scale, v_pool_scale),
             "gla_w_up_f": (gla_w_up_f, m_gla_w_up_f, v_gla_w_up_f), "gla_b_up_f": (gla_b_up_f, m_gla_b_up_f, v_gla_b_up_f),
             "gla_w_up_b": (gla_w_up_b, m_gla_w_up_b, v_gla_w_up_b), "gla_b_up_b": (gla_b_up_b, m_gla_b_up_b, v_gla_b_up_b),
             "gla_g_norm": (gla_g_norm, m_gla_g_norm, v_gla_g_norm)}
    order = ["norm_mix", "norm_mlp", "norm_final", "pool_w", "pool_scale", "gla_w_in", "gla_w_up_f", "gla_b_up_f", "gla_w_up_b",
             "gla_b_up_b", "gla_g_norm", "gla_w_out", "mlp_w_in", "mlp_w_out"]
    out_g, out_d, out_m, out_v = [], [], [], []
    for nm in order:
        if nm in big:
            g, dl, new_m, new_v = big[nm]
        else:
            g = grads[nm]
            dl, new_m, new_v = _adamw_small(f"adamw_{nm}", *state[nm], g)
        out_g.append(g), out_d.append(dl), out_m.append(new_m), out_v.append(new_v)
    return (loss, dx.reshape(x.shape), *out_g, *out_d, *out_m, *out_v)
```

```python
import functools

import jax
import jax.numpy as jnp
from jax import lax
from jax.experimental import pallas as pl
from jax.experimental.pallas import tpu as pltpu

F32 = jnp.float32
BF16 = jnp.bfloat16
N_DEV = 8
N_PEER = N_DEV - 1
LANES = 128
VMEM_LIMIT_BYTES = 48 * 1024 * 1024
MESH_ID = pl.DeviceIdType.MESH
TILE_K = 1024
TILE_M_BIG = 2048

POOL_WINDOWS = (2, 4, 8, 16)
N_POOL_GROUPS = len(POOL_WINDOWS)
GLA_HEADS = 4
GATE_RANK = 16
GATE_TAU = 16.0
CHUNK = 64
EPS = 1e-6
ADAM_LR = 0.001
ADAM_B1 = 0.9
ADAM_B2 = 0.999
ADAM_EPS = 1e-08
ADAM_WD = 0.01
ADAM_STEP = 10

NN = ((1,), (0,))
NT = ((1,), (1,))
TN = ((0,), (0,))


def _tile(dim, pref):
    if dim <= pref:
        return dim
    t = (pref // LANES) * LANES
    while t > LANES and dim % t:
        t -= LANES
    assert dim % t == 0, (dim, pref)
    return t


def _row_tile(rows, pref):
    if rows <= pref:
        return rows
    t = max(8, pref // 8 * 8)
    while t > 8 and rows % t:
        t -= 8
    assert rows % t == 0, (rows, pref)
    return t


def _params(*sem):
    return pltpu.CompilerParams(dimension_semantics=sem, vmem_limit_bytes=VMEM_LIMIT_BYTES)


def _dot(a, b, dims):
    return lax.dot_general(a, b, (dims, ((), ())), preferred_element_type=F32)


def _mm(name, a, b, *, grid, a_spec, b_spec, o_spec, acc_shape, outs, dims, extras=(), epi=None, gathers=(), scatters=()):
    n_ex, n_out, nk = len(extras), len(outs), grid[2]
    n_g, n_s = len(gathers), len(scatters)
    carries = bool(n_g or n_s)

    def body(*refs):
        a_ref, b_ref = refs[0], refs[1]
        ex = refs[2:2 + n_ex]
        r0 = 2 + n_ex
        g_in, s_in = refs[r0:r0 + n_g], refs[r0 + n_g:r0 + n_g + n_s]
        r0 += n_g + n_s
        o_refs = refs[r0:r0 + n_out]
        g_out, s_out = refs[r0 + n_out:r0 + n_out + n_g], refs[r0 + n_out + n_g:r0 + n_out + n_g + n_s]
        acc = refs[r0 + n_out + n_g + n_s]
        i, j, k = pl.program_id(0), pl.program_id(1), pl.program_id(2)
        if carries:
            start, middle, finish = _transfers(g_in, g_out, s_in, s_out, (), (), *refs[-3:])
            step = (i * grid[1] + j) * nk + k

            @pl.when(step == 0)
            def _():
                start()

            if n_g:
                @pl.when(step == (grid[0] * grid[1] * nk * 5) // 8)
                def _():
                    middle()

        def part():
            return _dot(a_ref[...].astype(BF16), b_ref[...].astype(BF16), dims)

        def finish_tile(total):
            res = epi(total, *[e[...] for e in ex]) if epi is not None else (total,)
            for o, r in zip(o_refs, res):
                o[...] = r.astype(o.dtype)

        if nk == 1:
            finish_tile(part())
        else:
            @pl.when(k == 0)
            def _():
                acc[...] = part()

            @pl.when((k > 0) & (k < nk - 1))
            def _():
                acc[...] += part()

            @pl.when(k == nk - 1)
            def _():
                finish_tile(acc[...] + part())

        if carries:
            @pl.when((i == grid[0] - 1) & (j == grid[1] - 1) & (k == nk - 1))
            def _():
                finish()

    hbm = pl.BlockSpec(memory_space=pl.ANY)
    operands = [a, b] + [e for e, _ in extras] + list(gathers) + list(scatters)
    in_specs = [a_spec, b_spec] + [s for _, s in extras] + [hbm] * (n_g + n_s)
    out_shape = [jax.ShapeDtypeStruct(s, d) for s, d in outs]
    out_shape += [jax.ShapeDtypeStruct((N_DEV,) + g.shape, g.dtype) for g in gathers]
    out_shape += [jax.ShapeDtypeStruct((len(CHIP_BITS),) + t.shape[1:], t.dtype) for t in scatters]
    scratch = [pltpu.VMEM(acc_shape, F32)] + (_transfer_sems(n_g, n_s, 0) if carries else [])
    sem = ("arbitrary",) * 3 if carries else ("parallel", "parallel", "arbitrary")
    return pl.pallas_call(
        body, name=name, grid=grid, in_specs=in_specs, out_specs=[o_spec] * n_out + [hbm] * (n_g + n_s),
        out_shape=out_shape, scratch_shapes=scratch, compiler_params=_params(*sem),
    )(*operands)


def _spec2(rows, cols, fn):
    return pl.BlockSpec((rows, cols), fn)


def _rowwise(name, fn, *, n_rows, tr, rows, bcast=(), outs=(), accs=()):
    n_r, n_b, n_o, n_a = len(rows), len(bcast), len(outs), len(accs)

    def body(*refs):
        r_refs, b_refs = refs[:n_r], refs[n_r:n_r + n_b]
        o_refs, a_refs = refs[n_r + n_b:n_r + n_b + n_o], refs[n_r + n_b + n_o:]
        res = fn(*[r[...] for r in r_refs], *[b[...] for b in b_refs])
        for o, r in zip(o_refs, res[:n_o]):
            o[...] = r.astype(o.dtype)
        if n_a:
            @pl.when(pl.program_id(0) == 0)
            def _():
                for acc in a_refs:
                    acc[...] = jnp.zeros_like(acc)

            for acc, r in zip(a_refs, res[n_o:]):
                acc[...] += r

    in_specs = [pl.BlockSpec((tr, w), functools.partial(lambda i, cb: (i, cb), cb=cb)) for _, w, cb in rows]
    in_specs += [pl.BlockSpec(b.shape, lambda i: (0, 0)) for b in bcast]
    out_specs = [pl.BlockSpec((tr, w), lambda i: (i, 0)) for w, _ in outs]
    out_specs += [pl.BlockSpec(s, lambda i: (0, 0)) for s in accs]
    out_shape = [jax.ShapeDtypeStruct((n_rows, w), d) for w, d in outs]
    out_shape += [jax.ShapeDtypeStruct(s, F32) for s in accs]
    return pl.pallas_call(
        body, name=name, grid=(n_rows // tr,), in_specs=in_specs, out_specs=out_specs, out_shape=out_shape,
        compiler_params=_params("arbitrary"),
    )(*[r for r, _, _ in rows], *bcast)


def _colsum(t):
    return jnp.sum(t, axis=0, keepdims=True)


def _rms_fwd(name, h, g, dtype, tr):
    def fn(x, gain):
        r = lax.rsqrt(jnp.mean(x * x, axis=-1, keepdims=True) + EPS)
        return (x * r * gain,)

    d = h.shape[1]
    return _rowwise(name, fn, n_rows=h.shape[0], tr=tr, rows=[(h, d, 0)], bcast=[g], outs=[(d, dtype)])[0]


def _rms_bwd(name, h, g, dhn, dres, tr):
    def fn(x, dy, dr, gain):
        r = lax.rsqrt(jnp.mean(x * x, axis=-1, keepdims=True) + EPS)
        xhat = x * r
        gd = gain * dy
        dh = r * (gd - xhat * jnp.mean(gd * xhat, axis=-1, keepdims=True)) + dr
        return dh, dh, _colsum(dy * xhat)

    d = h.shape[1]
    return _rowwise(name, fn, n_rows=h.shape[0], tr=tr, rows=[(h, d, 0), (dhn, d, 0), (dres, d, 0)], bcast=[g],
                    outs=[(d, F32), (d, BF16)], accs=[(1, d)])


def _loss_head(name, h, target, g, tr):
    d = h.shape[1]

    def fn(x, tgt, gain):
        r = lax.rsqrt(jnp.mean(x * x, axis=-1, keepdims=True) + EPS)
        xhat = x * r
        err = xhat * gain - tgt
        loss = 0.5 * jnp.sum(jnp.mean(err * err, axis=-1, keepdims=True), axis=0, keepdims=True)
        dy = err / d
        gd = gain * dy
        dh = r * (gd - xhat * jnp.mean(gd * xhat, axis=-1, keepdims=True))
        return dh, dh, jnp.broadcast_to(loss, (1, LANES)), _colsum(dy * xhat)

    return _rowwise(name, fn, n_rows=h.shape[0], tr=tr, rows=[(h, d, 0), (target, d, 0)], bcast=[g],
                    outs=[(d, F32), (d, BF16)], accs=[(1, LANES), (1, d)])


def _shift_rows(h, k, seq):
    if k == 0:
        return h
    rolled = pltpu.roll(h, (-k) % seq, 0)
    t = lax.broadcasted_iota(jnp.int32, h.shape, 0)
    return jnp.where((t + k >= 0) & (t + k < seq), rolled, 0.0)


def _window_halves(h, half, seq):
    right = h
    left = _shift_rows(h, -1, seq)
    step = 1
    while step < half:
        right = right + _shift_rows(right, step, seq)
        left = left + _shift_rows(left, -step, seq)
        step *= 2
    return left + right


def _window_count(shape, half, seq):
    t = lax.broadcasted_iota(jnp.int32, shape, 0)
    return (jnp.minimum(t + half, seq) - jnp.maximum(t - half, 0)).astype(F32)


def _pool_slabs(name, src, out_dtype, *, n_seq, seq, backward):
    d = src.shape[1]
    group = d // N_POOL_GROUPS
    tc = _tile(group, 256)

    def body(s_ref, o_ref):
        g = (pl.program_id(1) * tc) // group
        for gi, w in enumerate(POOL_WINDOWS):
            half = w // 2

            @pl.when(g == gi)
            def _():
                v = s_ref[...].astype(F32)
                cnt = _window_count(v.shape, half, seq)
                if backward:
                    u = v / cnt
                    res = (_window_halves(u, half, seq) - _shift_rows(u, -half, seq) + _shift_rows(u, half, seq)) - v
                else:
                    res = _window_halves(v, half, seq) / cnt - v
                o_ref[...] = res.astype(o_ref.dtype)

    spec = pl.BlockSpec((seq, tc), lambda b, j: (b, j))
    return pl.pallas_call(
        body, name=name, grid=(n_seq, d // tc), in_specs=[spec], out_specs=spec,
        out_shape=jax.ShapeDtypeStruct(src.shape, out_dtype), compiler_params=_params("parallel", "parallel"),
    )(src)


def _split3(x):
    x1 = x.astype(BF16)
    r1 = x - x1.astype(F32)
    x2 = r1.astype(BF16)
    x3 = (r1 - x2.astype(F32)).astype(BF16)
    return x1, x2, x3


def _tri_dot(tri, x):
    x1, x2, x3 = _split3(x)
    return _dot(tri, x1, NN) + _dot(tri, x2, NN) + _dot(tri, x3, NN)


def _chunk_terms(q_ref, k_ref, g_ref, rev, scale):
    c = q_ref.shape[0]
    ri = lax.broadcasted_iota(jnp.int32, (c, c), 0)
    ci = lax.broadcasted_iota(jnp.int32, (c, c), 1)
    causal = (ci >= ri) if rev else (ci <= ri)
    mid = c // 2 if rev else c // 2 - 1
    last = 0 if rev else c - 1
    b = _tri_dot(causal.astype(BF16), g_ref[...])
    b_mid, b_last = b[mid:mid + 1, :], b[last:last + 1, :]
    e_qm, e_km, e_qi, e_ks = jnp.exp(b - b_mid), jnp.exp(b_mid - b), jnp.exp(b), jnp.exp(b_last - b)
    q = q_ref[...].astype(F32) * scale
    k = k_ref[...].astype(F32)
    return dict(causal=causal, mid=mid, last=last, decay=jnp.exp(b_last), e_qm=e_qm, e_km=e_km, e_qi=e_qi, e_ks=e_ks,
                qm=q * e_qm, km=k * e_km, qi=q * e_qi, ks=k * e_ks)


HEADS_PER_STEP = 2


def _gla_specs(n_chunks, heads, dk, dv, key_dim, rev, backward):
    hps = HEADS_PER_STEP

    def chunk(n):
        scan_pos = (n_chunks - 1 - n) if backward else n
        return (n_chunks - 1 - scan_pos) if rev else scan_pos

    def at(width, col0):
        return pl.BlockSpec((CHUNK, hps * width), lambda b, h, n: (b * n_chunks + chunk(n), col0 // (hps * width) + h))

    state = pl.BlockSpec((hps, dv, dk), lambda b, h, n: ((b * n_chunks + chunk(n)) * (heads // hps) + h, 0, 0))
    return at(dk, 0), at(dk, key_dim), at(dv, 2 * key_dim), at(dk, 0), at(dv, 0), state


def _gla_fwd(name, pm, log_a, *, rev, n_seq, seq, key_dim, val_dim):
    heads, n_chunks = GLA_HEADS, seq // CHUNK
    dk, dv = key_dim // heads, val_dim // heads
    scale = dk ** -0.5
    q_spec, k_spec, v_spec, g_spec, o_spec, st_spec = _gla_specs(n_chunks, heads, dk, dv, key_dim, rev, False)

    def body(q_ref, k_ref, v_ref, g_ref, o_ref, st_ref, state):
        @pl.when(pl.program_id(2) == 0)
        def _():
            state[...] = jnp.zeros_like(state)

        for i in range(HEADS_PER_STEP):
            kc, vc = pl.ds(i * dk, dk), pl.ds(i * dv, dv)
            t = _chunk_terms(q_ref.at[:, kc], k_ref.at[:, kc], g_ref.at[:, kc], rev, scale)
            v = v_ref[:, vc]
            p = jnp.where(t["causal"], _dot(t["qm"].astype(BF16), t["km"].astype(BF16), NT), 0.0)
            entering = state[i].astype(BF16)
            st_ref[i] = entering
            o_ref[:, vc] = _dot(p.astype(BF16), v, NN) + _dot(t["qi"].astype(BF16), entering, NT)
            state[i] = t["decay"] * state[i] + _dot(v, t["ks"].astype(BF16), TN)

    tokens = n_seq * seq
    return pl.pallas_call(
        body, name=name, grid=(n_seq, heads // HEADS_PER_STEP, n_chunks), in_specs=[q_spec, k_spec, v_spec, g_spec],
        out_specs=[o_spec, st_spec],
        out_shape=[jax.ShapeDtypeStruct((tokens, val_dim), F32),
                   jax.ShapeDtypeStruct((n_seq * n_chunks * heads, dv, dk), BF16)],
        scratch_shapes=[pltpu.VMEM((HEADS_PER_STEP, dv, dk), F32)],
        compiler_params=_params("parallel", "parallel", "arbitrary"),
    )(pm, pm, pm, log_a)


def _gla_bwd(name, pm, log_a, states, d_o, *, rev, n_seq, seq, key_dim, val_dim):
    heads, n_chunks = GLA_HEADS, seq // CHUNK
    dk, dv = key_dim // heads, val_dim // heads
    scale = dk ** -0.5
    q_spec, k_spec, v_spec, g_spec, o_spec, st_spec = _gla_specs(n_chunks, heads, dk, dv, key_dim, rev, True)

    def body(q_ref, k_ref, v_ref, g_ref, do_ref, st_ref, dq_ref, dk_ref, dv_ref, dg_ref, dstate):
        @pl.when(pl.program_id(2) == 0)
        def _():
            dstate[...] = jnp.zeros_like(dstate)

        for i in range(HEADS_PER_STEP):
            kc, vc = pl.ds(i * dk, dk), pl.ds(i * dv, dv)
            t = _chunk_terms(q_ref.at[:, kc], k_ref.at[:, kc], g_ref.at[:, kc], rev, scale)
            causal = t["causal"]
            v, d_out, entering = v_ref[:, vc], do_ref[:, vc], st_ref[i]
            qm, km, qi, ks = (t[n].astype(BF16) for n in ("qm", "km", "qi", "ks"))
            dst = dstate[i]
            dst16 = dst.astype(BF16)
            p = jnp.where(causal, _dot(qm, km, NT), 0.0).astype(BF16)
            dp = jnp.where(causal, _dot(d_out, v, NT), 0.0).astype(BF16)
            dv_ref[:, vc] = _dot(p, d_out, TN) + _dot(ks, dst16, NT)
            dqm, dkm = _dot(dp, km, NN), _dot(dp, qm, TN)
            dqi, dks = _dot(d_out, entering, NN), _dot(v, dst16, NN)
            d_decay = _colsum(dst * entering.astype(F32))
            dq_ref[:, kc] = (dqi * t["e_qi"] + dqm * t["e_qm"]) * scale
            dk_ref[:, kc] = dkm * t["e_km"] + dks * t["e_ks"]
            qm_term, km_term = dqm * t["qm"], dkm * t["km"]
            ks_term = dks * t["ks"]
            db = dqi * t["qi"] + qm_term - km_term - ks_term
            row = lax.broadcasted_iota(jnp.int32, db.shape, 0)
            db = db + jnp.where(row == t["mid"], _colsum(km_term - qm_term), 0.0)
            db = db + jnp.where(row == t["last"], _colsum(ks_term) + d_decay * t["decay"], 0.0)
            ri = lax.broadcasted_iota(jnp.int32, causal.shape, 0)
            ci = lax.broadcasted_iota(jnp.int32, causal.shape, 1)
            anti = (ci <= ri) if rev else (ci >= ri)
            dg_ref[:, kc] = _tri_dot(anti.astype(BF16), db)
            dstate[i] = t["decay"] * dst + _dot(d_out, qi, TN)

    tokens = n_seq * seq
    kd = jax.ShapeDtypeStruct((tokens, key_dim), F32)
    return pl.pallas_call(
        body, name=name, grid=(n_seq, heads // HEADS_PER_STEP, n_chunks),
        in_specs=[q_spec, k_spec, v_spec, g_spec, o_spec, st_spec], out_specs=[g_spec, g_spec, o_spec, g_spec],
        out_shape=[kd, kd, jax.ShapeDtypeStruct((tokens, val_dim), F32), kd],
        scratch_shapes=[pltpu.VMEM((HEADS_PER_STEP, dv, dk), F32)],
        compiler_params=_params("parallel", "parallel", "arbitrary"),
    )(pm, pm, pm, log_a, d_o, states)


def _sigmoid_neg(z):
    e = jnp.exp(-jnp.abs(z))
    return jnp.where(z >= 0, e / (1.0 + e), 1.0 / (1.0 + e))


def _gate_fwd(name, pr, wf, bf, wb, bb, tr):
    kd = wf.shape[1]

    def fn(r, w_f, b_f, w_b, b_b):
        r16 = r.astype(BF16)
        out = []
        for w, b in ((w_f, b_f), (w_b, b_b)):
            z = _dot(r16, w.astype(BF16), NN) + b
            out.append((jnp.minimum(z, 0.0) - jnp.log1p(jnp.exp(-jnp.abs(z)))) / GATE_TAU)
        return tuple(out)

    return _rowwise(name, fn, n_rows=pr.shape[0], tr=tr, rows=[(pr, LANES, 0)], bcast=[wf, bf, wb, bb],
                    outs=[(kd, F32), (kd, F32)])


def _gate_bwd(name, pr, wf, bf, wb, bb, dgf, dgb, tr):
    kd = wf.shape[1]

    def fn(r, d_f, d_b, w_f, b_f, w_b, b_b):
        r16 = r.astype(BF16)
        dpr = jnp.zeros(r.shape, F32)
        sums = []
        for w, b, dg in ((w_f, b_f, d_f), (w_b, b_b, d_b)):
            w16 = w.astype(BF16)
            z = _dot(r16, w16, NN) + b
            dz = dg / GATE_TAU * _sigmoid_neg(z)
            dz16 = dz.astype(BF16)
            dpr = dpr + _dot(dz16, w16, NT)
            sums += [_dot(r16, dz16, TN), _colsum(dz)]
        return (dpr, *sums)

    return _rowwise(name, fn, n_rows=pr.shape[0], tr=tr, rows=[(pr, LANES, 0), (dgf, kd, 0), (dgb, kd, 0)],
                    bcast=[wf, bf, wb, bb], outs=[(LANES, BF16)], accs=[(LANES, kd), (1, kd), (LANES, kd), (1, kd)])


def _head_norm_fwd(name, of, ob, pm, gn, *, val_dim, gate_block, tr):
    dv = val_dim // GLA_HEADS

    def fn(o_f, o_b, gate, gain):
        out = []
        for h in range(GLA_HEADS):
            sl = slice(h * dv, (h + 1) * dv)
            o = o_f[:, sl] + o_b[:, sl]
            gt = gate[:, sl].astype(F32)
            on = o * lax.rsqrt(jnp.mean(o * o, axis=-1, keepdims=True) + EPS) * gain
            out.append(on * (gt * jax.nn.sigmoid(gt)))
        return (jnp.concatenate(out, axis=1),)

    return _rowwise(name, fn, n_rows=of.shape[0], tr=tr, rows=[(of, val_dim, 0), (ob, val_dim, 0), (pm, val_dim, gate_block)],
                    bcast=[gn], outs=[(val_dim, BF16)])[0]


def _head_norm_bwd(name, of, ob, pm, gn, dz, *, val_dim, gate_block, tr):
    dv = val_dim // GLA_HEADS

    def fn(o_f, o_b, gate, d_z, gain):
        d_o, d_gate = [], []
        d_gain = jnp.zeros((1, dv), F32)
        for h in range(GLA_HEADS):
            sl = slice(h * dv, (h + 1) * dv)
            o = o_f[:, sl] + o_b[:, sl]
            gt = gate[:, sl].astype(F32)
            dzh = d_z[:, sl]
            r = lax.rsqrt(jnp.mean(o * o, axis=-1, keepdims=True) + EPS)
            ohat = o * r
            sg = jax.nn.sigmoid(gt)
            d_gate.append(dzh * (ohat * gain) * (sg * (1.0 + gt * (1.0 - sg))))
            don = dzh * (gt * sg)
            d_gain = d_gain + _colsum(don * ohat)
            gd = gain * don
            d_o.append(r * (gd - ohat * jnp.mean(gd * ohat, axis=-1, keepdims=True)))
        return jnp.concatenate(d_o, axis=1), jnp.concatenate(d_gate, axis=1), d_gain

    return _rowwise(name, fn, n_rows=of.shape[0], tr=tr,
                    rows=[(of, val_dim, 0), (ob, val_dim, 0), (pm, val_dim, gate_block), (dz, val_dim, 0)], bcast=[gn],
                    outs=[(val_dim, BF16), (val_dim, BF16)], accs=[(1, dv)])


def _proj_grad(name, dqf, dqb, dkf, dkb, dvf, dvb, dgate, *, key_dim, val_dim, tr):
    def fn(q1, q2, k1, k2, v1, v2, gt):
        return (jnp.concatenate([(q1 + q2).astype(BF16), (k1 + k2).astype(BF16), (v1 + v2).astype(BF16), gt], axis=1),)

    rows = [(dqf, key_dim, 0), (dqb, key_dim, 0), (dkf, key_dim, 0), (dkb, key_dim, 0), (dvf, val_dim, 0),
            (dvb, val_dim, 0), (dgate, val_dim, 0)]
    return _rowwise(name, fn, n_rows=dqf.shape[0], tr=tr, rows=rows, outs=[(2 * key_dim + 2 * val_dim, BF16)])[0]


def _pool_out_grad(name, dh, y_pre, scale, tr):
    def fn(d, y, s):
        return d * s, _colsum(d * y)

    dm = dh.shape[1]
    return _rowwise(name, fn, n_rows=dh.shape[0], tr=tr, rows=[(dh, dm, 0), (y_pre, dm, 0)], bcast=[scale],
                    outs=[(dm, BF16)], accs=[(1, dm)])


def _mesh_place():
    x, y, c = lax.axis_index("x"), lax.axis_index("y"), lax.axis_index("c")
    return x, y, c, 4 * x + 2 * y + c


def _peer(x, y, c, p):
    px, py, pc = (x if not p & 4 else 1 - x), (y if not p & 2 else 1 - y), (c if not p & 1 else 1 - c)
    return (px, py, pc), 4 * px + 2 * py + pc


CHIP_BITS = (2, 4, 6)
N_CHIP = N_DEV // 2
GATHER_COPIES = 8
FLIP_C, FLIP_Y, FLIP_X = 1, 2, 4


def _transfer_sems(n_gather, n_scatter, n_pair):
    n_remote = n_gather * GATHER_COPIES + n_scatter * len(CHIP_BITS) + n_pair * N_CHIP
    return [pltpu.SemaphoreType.DMA((n_remote,)), pltpu.SemaphoreType.DMA((n_remote,)),
            pltpu.SemaphoreType.DMA((max(n_gather, 1),))]


def _transfers(g_in, g_out, s_in, s_out, p_in, p_out, send, recv, loc):
    x, y, c, me = _mesh_place()
    sibling, _ = _peer(x, y, c, FLIP_C)

    def remote(src, dst, k, peer):
        return pltpu.make_async_remote_copy(src_ref=src, dst_ref=dst, send_sem=send.at[k], recv_sem=recv.at[k],
                                            device_id=peer, device_id_type=MESH_ID)

    local, first, arrivals = [], [], []
    passed, passed_arrivals, last, last_arrivals = [], [], [], []
    k = 0
    for src, out in zip(g_in, g_out):
        half = src.shape[0] // 2
        halves = (pl.ds(0, half), pl.ds(half, half))
        x_chip, x_block = _peer(x, y, c, FLIP_X)
        y_chip, y_block = _peer(x, y, c, FLIP_Y)
        _, far_block = _peer(x, y, c, FLIP_X | FLIP_Y)
        local.append(pltpu.make_async_copy(src, out.at[me], loc.at[len(local)]))
        for peer, block in ((sibling, me ^ FLIP_C), (x_chip, x_block), (y_chip, y_block)):
            first.append(remote(src, out.at[me], k, peer))
            arrivals.append(remote(src, out.at[block], k, peer))
            k += 1
        for peer, block, part in ((x_chip, y_block, halves[0]), (y_chip, x_block, halves[1])):
            passed.append(remote(out.at[block, part], out.at[block, part], k, peer))
            passed_arrivals.append(remote(out.at[far_block, part], out.at[far_block, part], k, peer))
            k += 1
        for block in (x_block, y_block):
            passed.append(remote(out.at[block], out.at[block], k, sibling))
            passed_arrivals.append(remote(out.at[block ^ FLIP_C], out.at[block ^ FLIP_C], k, sibling))
            k += 1
        last.append(remote(out.at[far_block], out.at[far_block], k, sibling))
        last_arrivals.append(remote(out.at[far_block ^ FLIP_C], out.at[far_block ^ FLIP_C], k, sibling))
        k += 1
    n_gather_first = len(first)
    for src, out in zip(s_in, s_out):
        for j, p in enumerate(CHIP_BITS):
            peer, _ = _peer(x, y, c, p)
            peer_chip = 2 * peer[0] + peer[1]
            first.append(remote(src.at[peer_chip], out.at[j], k, peer))
            arrivals.append(remote(src.at[peer_chip], out.at[j], k, peer))
            k += 1
    for src, out in zip(p_in, p_out):
        for q in range(N_CHIP):
            first.append(remote(src.at[2 * q + 1 - c], out.at[q], k, sibling))
            arrivals.append(remote(src.at[2 * q + 1 - c], out.at[q], k, sibling))
            k += 1

    def start():
        for cp in local + first:
            cp.start()

    def middle():
        for cp in arrivals[:n_gather_first]:
            cp.wait_recv()
        for cp in passed:
            cp.start()

    def finish():
        for cp in passed_arrivals:
            cp.wait_recv()
        for cp in last:
            cp.start()
        for cp in last_arrivals + arrivals[n_gather_first:]:
            cp.wait_recv()
        for cp in first + passed + last:
            cp.wait_send()
        for cp in local:
            cp.wait()

    return start, middle, finish


def _exchange(name, gathers=(), scatters=(), pairs=()):
    n_g, n_s, n_p = len(gathers), len(scatters), len(pairs)
    n_in = n_g + n_s + n_p

    def body(*refs):
        ins, outs = refs[:n_in], refs[n_in:-3]
        start, middle, finish = _transfers(ins[:n_g], outs[:n_g], ins[n_g:n_g + n_s], outs[n_g:n_g + n_s], ins[n_g + n_s:],
                                           outs[n_g + n_s:], *refs[-3:])
        start()
        middle()
        finish()

    hbm = pl.BlockSpec(memory_space=pl.ANY)
    out_shape = [jax.ShapeDtypeStruct((N_DEV,) + g.shape, g.dtype) for g in gathers]
    out_shape += [jax.ShapeDtypeStruct((len(CHIP_BITS),) + t.shape[1:], t.dtype) for t in scatters]
    out_shape += [jax.ShapeDtypeStruct((N_CHIP,) + t.shape[1:], t.dtype) for t in pairs]
    return pl.pallas_call(
        body, name=name, in_specs=[hbm] * n_in, out_specs=[hbm] * len(out_shape), out_shape=out_shape,
        scratch_shapes=_transfer_sems(n_g, n_s, n_p),
    )(*gathers, *scatters, *pairs)


def _place_scalar(axis_value):
    return jnp.asarray(axis_value, jnp.int32).reshape(1)


def _chip_sums(name, grad):
    received = _exchange(name + "_pair", pairs=[grad])[0]
    cols = grad.shape[-1]
    rows = received.size // cols // N_CHIP
    tr = _row_tile(rows, (512 * 1024) // cols)

    def body(core_ref, mine_ref, got_ref, out_ref):
        out_ref[...] = (mine_ref[...].astype(F32) + got_ref[...].astype(F32)).astype(out_ref.dtype)

    block = (None, tr, cols)
    out = pl.pallas_call(
        body, name=name + "_add", out_shape=jax.ShapeDtypeStruct((N_CHIP, rows, cols), grad.dtype),
        grid_spec=pltpu.PrefetchScalarGridSpec(
            num_scalar_prefetch=1, grid=(N_CHIP, rows // tr),
            in_specs=[pl.BlockSpec(block, lambda q, i, core: (2 * q + core[0], i, 0)),
                      pl.BlockSpec(block, lambda q, i, core: (q, i, 0))],
            out_specs=pl.BlockSpec(block, lambda q, i, core: (q, i, 0))),
        compiler_params=_params("parallel", "parallel"),
    )(_place_scalar(lax.axis_index("c")), grad.reshape(N_DEV, rows, cols), received.reshape(N_CHIP, rows, cols))
    return out.reshape(received.shape)


def _all_reduce_small(name, pack):
    def body(in_ref, out_ref, parts, send_sems, recv_sems):
        x, y, c, me = _mesh_place()
        parts[me] = in_ref[...]

        def copy(p, arriving):
            peer, peer_block = _peer(x, y, c, p)
            return pltpu.make_async_remote_copy(src_ref=in_ref, dst_ref=parts.at[peer_block if arriving else me],
                                                send_sem=send_sems.at[p - 1], recv_sem=recv_sems.at[p - 1],
                                                device_id=peer, device_id_type=MESH_ID)

        sends = [copy(p, False) for p in range(1, N_DEV)]
        for cp in sends:
            cp.start()
        for p in range(1, N_DEV):
            copy(p, True).wait_recv()
        for cp in sends:
            cp.wait_send()
        total = parts[0]
        for j in range(1, N_DEV):
            total = total + parts[j]
        out_ref[...] = total

    vmem = pl.BlockSpec(memory_space=pltpu.VMEM)
    return pl.pallas_call(
        body, name=name, in_specs=[vmem], out_specs=vmem, out_shape=jax.ShapeDtypeStruct(pack.shape, F32),
        scratch_shapes=[pltpu.VMEM((N_DEV,) + pack.shape, F32), pltpu.SemaphoreType.DMA((N_PEER,)),
                        pltpu.SemaphoreType.DMA((N_PEER,))],
        compiler_params=pltpu.CompilerParams(vmem_limit_bytes=VMEM_LIMIT_BYTES),
    )(pack)


def _adamw_math(w, g, m, v):
    m = ADAM_B1 * m + (1.0 - ADAM_B1) * g
    v = ADAM_B2 * v + (1.0 - ADAM_B2) * jnp.square(g)
    m_hat = m / (1.0 - ADAM_B1 ** ADAM_STEP)
    v_hat = v / (1.0 - ADAM_B2 ** ADAM_STEP)
    delta = -ADAM_LR * (m_hat / (jnp.sqrt(v_hat) + ADAM_EPS) + ADAM_WD * w)
    return delta, m, v


def _adamw_shard(name, w, m, v, grad_parts, layer=0, into=None):
    landed, sums = grad_parts
    shape = w.shape
    n_layers, cols = shape[0], shape[-1]
    rows = w.size // cols // n_layers
    tr = _row_tile(rows, (256 * 1024) // cols)
    n_into = 0 if into is None else 4
    n_landed = landed.shape[0]

    def body(chip_ref, w_ref, m_ref, v_ref, mine_ref, got_ref, *refs):
        g_ref, d_ref, nm_ref, nv_ref = refs[n_into:]
        g = mine_ref[...].astype(F32)
        for j in range(n_landed):
            g = g + got_ref[j].astype(F32)
        delta, new_m, new_v = _adamw_math(w_ref[...], g, m_ref[...], v_ref[...])
        g_ref[...], d_ref[...], nm_ref[...], nv_ref[...] = g, delta, new_m, new_v

    if n_layers == 1:
        view = (rows, cols)
        spec = pl.BlockSpec((tr, cols), lambda i, chip: (i, 0))
    else:
        view = (n_layers, rows, cols)
        spec = pl.BlockSpec((None, tr, cols), lambda i, chip: (layer, i, 0))
    operands = [w.reshape(view), m.reshape(view), v.reshape(view), sums.reshape(N_CHIP, rows, cols),
                landed.reshape(n_landed, rows, cols)]
    in_specs = [spec, spec, spec, pl.BlockSpec((None, tr, cols), lambda i, chip: (chip[0], i, 0)),
                pl.BlockSpec((n_landed, tr, cols), lambda i, chip: (0, i, 0))]
    if into is not None:
        operands += [t.reshape(view) for t in into]
        in_specs += [pl.BlockSpec(memory_space=pl.ANY)] * 4
    first_into = 1 + 5
    res = pl.pallas_call(
        body, name=name, out_shape=[jax.ShapeDtypeStruct(view, F32)] * 4,
        grid_spec=pltpu.PrefetchScalarGridSpec(num_scalar_prefetch=1, grid=(rows // tr,), in_specs=in_specs, out_specs=[spec] * 4),
        input_output_aliases={first_into + t: t for t in range(n_into)}, compiler_params=_params("parallel"),
    )(_place_scalar(2 * lax.axis_index("x") + lax.axis_index("y")), *operands)
    return [r.reshape(shape) for r in res]


def _adamw_small(name, w, m, v, g):
    shape = w.shape
    two_d = (1, shape[0]) if len(shape) == 1 else (w.size // shape[-1], shape[-1])

    def body(w_ref, m_ref, v_ref, g_ref, d_ref, nm_ref, nv_ref):
        d_ref[...], nm_ref[...], nv_ref[...] = _adamw_math(w_ref[...], g_ref[...], m_ref[...], v_ref[...])

    res = pl.pallas_call(body, name=name, out_shape=[jax.ShapeDtypeStruct(two_d, F32)] * 3)(
        w.reshape(two_d), m.reshape(two_d), v.reshape(two_d), g.reshape(two_d))
    return [r.reshape(shape) for r in res]


def _mlp_fwd(tag, h_in, gain, w1, w2_mine, down_gathers, tm, tr):
    tokens, d = h_in.shape
    fs = w1.shape[2]
    f = N_DEV * fs
    hn = _rms_fwd(f"rms_mlp{tag}", h_in, gain, BF16, tr)
    tn, tk = _tile(fs, 1024), _tile(d, TILE_K)
    per = fs // tn
    tmu = _tile(tokens, TILE_M_BIG)
    act, w2 = _mm(
        f"mlp_up{tag}", hn, w1, grid=(tokens // tmu, f // tn, d // tk), a_spec=_spec2(tmu, tk, lambda m, n, k: (m, k)),
        b_spec=pl.BlockSpec((None, tk, tn), lambda m, n, k: (n // per, k, n % per)),
        o_spec=_spec2(tmu, tn, lambda m, n, k: (m, n)), acc_shape=(tmu, tn), outs=[((tokens, f), BF16)], dims=NN,
        epi=lambda acc: (jnp.square(jnp.maximum(acc, 0.0)),), gathers=[w2_mine])
    tn2, tk2 = _tile(d, 1024), _tile(fs, TILE_K)
    kper = fs // tk2
    h_out, *got = _mm(
        f"mlp_down{tag}", act, w2, grid=(tokens // tm, d // tn2, f // tk2), a_spec=_spec2(tm, tk2, lambda m, n, k: (m, k)),
        b_spec=pl.BlockSpec((None, tk2, tn2), lambda m, n, k: (k // kper, k % kper, n)),
        o_spec=_spec2(tm, tn2, lambda m, n, k: (m, n)), acc_shape=(tm, tn2), outs=[((tokens, d), F32)], dims=NN,
        extras=[(h_in, _spec2(tm, tn2, lambda m, n, k: (m, n)))], epi=lambda acc, res: (res + acc,), gathers=down_gathers)
    return hn, act, h_out, w2, got


def _mlp_bwd(tag, h_in, gain, hn, act, dh, dh16, w1, w2, tm, tr, pending=()):
    tokens, d = h_in.shape
    fs = w1.shape[2]
    f = N_DEV * fs
    tn, tk = _tile(fs, 1024), _tile(d, TILE_K)
    nper = fs // tn
    tmb = _tile(tokens, TILE_M_BIG)
    tkt = _tile(tokens, TILE_K)
    tm2, tn2 = _tile(fs, TILE_M_BIG), _tile(d, 1024)
    mper = fs // tm2
    dw2, *pending_landed = _mm(
        f"mlp_dw_out{tag}", act, dh16, grid=(f // tm2, d // tn2, tokens // tkt), a_spec=_spec2(tkt, tm2, lambda m, n, k: (k, m)),
        b_spec=_spec2(tkt, tn2, lambda m, n, k: (k, n)),
        o_spec=pl.BlockSpec((None, tm2, tn2), lambda m, n, k: (m // mper, m % mper, n)), acc_shape=(tm2, tn2),
        outs=[((N_DEV, fs, d), BF16)], dims=TN, scatters=pending)
    dw2_sums = _chip_sums(f"mlp_dw_out{tag}", dw2)
    da, dw2_landed = _mm(
        f"mlp_dact{tag}", dh16, w2, grid=(tokens // tmb, f // tn, d // tk), a_spec=_spec2(tmb, tk, lambda m, n, k: (m, k)),
        b_spec=pl.BlockSpec((None, tn, tk), lambda m, n, k: (n // nper, n % nper, k)),
        o_spec=_spec2(tmb, tn, lambda m, n, k: (m, n)), acc_shape=(tmb, tn), outs=[((tokens, f), BF16)], dims=NT,
        extras=[(act, _spec2(tmb, tn, lambda m, n, k: (m, n)))],
        epi=lambda acc, a2: (acc * (2.0 * jnp.sqrt(a2.astype(F32))),), scatters=[dw2_sums])
    tm3 = _tile(d, TILE_M_BIG)
    dw1 = _mm(
        f"mlp_dw_in{tag}", hn, da, grid=(d // tm3, f // tn, tokens // tkt), a_spec=_spec2(tkt, tm3, lambda m, n, k: (k, m)),
        b_spec=_spec2(tkt, tn, lambda m, n, k: (k, n)),
        o_spec=pl.BlockSpec((None, tm3, tn), lambda m, n, k: (n // nper, m, n % nper)), acc_shape=(tm3, tn),
        outs=[((N_DEV, d, fs), BF16)], dims=TN)[0]
    dw1_sums = _chip_sums(f"mlp_dw_in{tag}", dw1)
    tn4, tk4 = _tile(d, 1024), _tile(fs, TILE_K)
    kper = fs // tk4
    dhn, dw1_landed = _mm(
        f"mlp_dx{tag}", da, w1, grid=(tokens // tmb, d // tn4, f // tk4), a_spec=_spec2(tmb, tk4, lambda m, n, k: (m, k)),
        b_spec=pl.BlockSpec((None, tn4, tk4), lambda m, n, k: (k // kper, n, k % kper)),
        o_spec=_spec2(tmb, tn4, lambda m, n, k: (m, n)), acc_shape=(tmb, tn4), outs=[((tokens, d), F32)], dims=NT,
        scatters=[dw1_sums])
    dh_in, dh_in16, dgain = _rms_bwd(f"rms_mlp_bwd{tag}", h_in, gain, dhn, dh, tr)
    return dh_in, dh_in16, dgain, (dw1_landed, dw1_sums), (dw2_landed, dw2_sums), pending_landed


def _plain_mm(name, a, b, dims, out_dtype, *, tm, tn_pref=1024, tk_pref=TILE_K, extras=None, epi=None, gathers=(), scatters=()):
    if dims == NN:
        (m, k), n = a.shape, b.shape[1]
    elif dims == NT:
        (m, k), n = a.shape, b.shape[0]
    else:
        (k, m), n = a.shape, b.shape[1]
    tm, tn, tk = _tile(m, tm), _tile(n, tn_pref), _tile(k, tk_pref)
    a_spec = _spec2(tk, tm, lambda i, j, l: (l, i)) if dims == TN else _spec2(tm, tk, lambda i, j, l: (i, l))
    b_spec = _spec2(tn, tk, lambda i, j, l: (j, l)) if dims == NT else _spec2(tk, tn, lambda i, j, l: (l, j))
    o_spec = _spec2(tm, tn, lambda i, j, l: (i, j))
    ex = [(e, o_spec) for e in (extras or [])]
    res = _mm(name, a, b, grid=(m // tm, n // tn, k // tk), a_spec=a_spec, b_spec=b_spec, o_spec=o_spec, acc_shape=(tm, tn),
              outs=[((m, n), out_dtype)], dims=dims, extras=ex, epi=epi, gathers=gathers, scatters=scatters)
    return res if (gathers or scatters) else res[0]


def kernel(x, norm_mix, norm_mlp, norm_final, pool_w, pool_scale, gla_w_in, gla_w_up_f, gla_b_up_f, gla_w_up_b, gla_b_up_b, gla_g_norm, gla_w_out, mlp_w_in, mlp_w_out, loss_target, m_norm_mix, m_norm_mlp, m_norm_final, m_pool_w, m_pool_scale, m_gla_w_in, m_gla_w_up_f, m_gla_b_up_f, m_gla_w_up_b, m_gla_b_up_b, m_gla_g_norm, m_gla_w_out, m_mlp_w_in, m_mlp_w_out, v_norm_mix, v_norm_mlp, v_norm_final, v_pool_w, v_pool_scale, v_gla_w_in, v_gla_w_up_f, v_gla_b_up_f, v_gla_w_up_b, v_gla_b_up_b, v_gla_g_norm, v_gla_w_out, v_mlp_w_in, v_mlp_w_out):
    n_seq, seq, d = x.shape
    tokens = n_seq * seq
    group = d // N_POOL_GROUPS
    key_dim, val_dim = d // 2, d
    dv = val_dim // GLA_HEADS
    n_main = 2 * key_dim + 2 * val_dim
    proj_shard = gla_w_in.shape[2]
    proj_width = N_DEV * proj_shard
    kd_shard = key_dim // N_DEV
    assert proj_width == n_main + 2 * GATE_RANK and seq % CHUNK == 0 and 2 * GATE_RANK <= LANES
    assert mlp_w_in.shape[0] == 2 and pool_w.shape[0] == 1 and gla_w_in.shape[0] == 1
    me = 4 * lax.axis_index("x") + 2 * lax.axis_index("y") + lax.axis_index("c")
    tm = _tile(tokens, 1024)
    tr = _tile(tokens, 128)
    x2, target = x.reshape(tokens, d), loss_target.reshape(tokens, d)

    small_rows = 2 * GATE_RANK + 4
    small = jnp.concatenate([
        gla_w_up_f[0], gla_w_up_b[0], gla_b_up_f, gla_b_up_b,
        jnp.pad(gla_g_norm, ((0, 0), (0, kd_shard - gla_g_norm.shape[1]))), jnp.zeros((1, kd_shard), F32)], axis=0)
    w1_mine = [mlp_w_in[l].astype(BF16) for l in range(2)]
    w2_mine = [mlp_w_out[l].astype(BF16) for l in range(2)]
    pool_g, small_g, w1_0 = _exchange(
        "gather_first", gathers=[pool_w[0].astype(BF16), small.reshape(2, small_rows // 2, kd_shard), w1_mine[0]])
    small_g = small_g.reshape(N_DEV, small_rows, kd_shard)
    w_pool = pool_g.transpose(1, 0, 2, 3).reshape(N_POOL_GROUPS, group, group)
    small_full = small_g.transpose(1, 0, 2).reshape(small_rows, key_dim)
    pad_rows = lambda w, r0: jnp.pad(w, ((r0, LANES - r0 - GATE_RANK), (0, 0)))
    w_up_f, w_up_b = pad_rows(small_full[:GATE_RANK], 0), pad_rows(small_full[GATE_RANK:2 * GATE_RANK], GATE_RANK)
    b_up_f, b_up_b = small_full[2 * GATE_RANK:2 * GATE_RANK + 1], small_full[2 * GATE_RANK + 1:2 * GATE_RANK + 2]
    g_norm = small_g[:, 2 * GATE_RANK + 2, :gla_g_norm.shape[1]].reshape(1, dv)

    hn0 = _rms_fwd("rms_mix0", x2, norm_mix[0:1], F32, tr)
    pooled = _pool_slabs("pool_fwd", hn0, BF16, n_seq=n_seq, seq=seq, backward=False)
    tn, tk = _tile(group, 1024), _tile(group, TILE_K)
    nper, kper = group // tn, group // tk
    y_pre, h1 = _mm(
        "pool_mix", pooled, w_pool, grid=(tokens // tm, d // tn, group // tk),
        a_spec=_spec2(tm, tk, lambda m, n, k: (m, (n // nper) * kper + k)),
        b_spec=pl.BlockSpec((None, tk, tn), lambda m, n, k: (n // nper, k, n % nper)),
        o_spec=_spec2(tm, tn, lambda m, n, k: (m, n)), acc_shape=(tm, tn), outs=[((tokens, d), F32)] * 2, dims=NN,
        extras=[(x2, _spec2(tm, tn, lambda m, n, k: (m, n))), (pool_scale, _spec2(1, tn, lambda m, n, k: (0, n)))],
        epi=lambda acc, res, s: (acc, res + acc * s))
    hn1, act0, h2, w2_0, (win_g, wout_g) = _mlp_fwd(
        "0", h1, norm_mlp[0:1], w1_0, w2_mine[0], [gla_w_in[0].astype(BF16), gla_w_out[0].astype(BF16)], tm, tr)
    w_in = win_g.transpose(1, 0, 2).reshape(d, proj_width)
    w_main = w_in[:, :n_main]
    w_r = jnp.pad(w_in[:, n_main:], ((0, 0), (0, LANES - 2 * GATE_RANK)))
    w_out = wout_g.reshape(val_dim, d)

    hn2 = _rms_fwd("rms_mix1", h2, norm_mix[1:2], BF16, tr)
    pm, w1_1 = _plain_mm("gla_proj", hn2, w_main, NN, BF16, tm=TILE_M_BIG, gathers=[w1_mine[1]])
    pr = _plain_mm("gla_proj_gate", hn2, w_r, NN, F32, tm=tm)
    log_a_f, log_a_b = _gate_fwd("gla_gate", pr, w_up_f, b_up_f, w_up_b, b_up_b, tr)
    dims = dict(n_seq=n_seq, seq=seq, key_dim=key_dim, val_dim=val_dim)
    o_f, st_f = _gla_fwd("gla_scan_f", pm, log_a_f, rev=False, **dims)
    o_b, st_b = _gla_fwd("gla_scan_b", pm, log_a_b, rev=True, **dims)
    gate_block = (2 * key_dim + val_dim) // val_dim
    z = _head_norm_fwd("gla_head_norm", o_f, o_b, pm, g_norm, val_dim=val_dim, gate_block=gate_block, tr=tr)
    h3 = _plain_mm("gla_out", z, w_out, NN, F32, tm=tm, extras=[h2], epi=lambda acc, res: (res + acc,))
    hn3, act1, h4, w2_1, _ = _mlp_fwd("1", h3, norm_mlp[1:2], w1_1, w2_mine[1], [], tm, tr)

    dh4, dh4_16, loss_row, d_norm_final = _loss_head("loss_head", h4, target, norm_final.reshape(1, d), tr)

    dh3, dh3_16, d_norm_mlp1, dw1_1, dw2_1, _ = _mlp_bwd("1", h3, norm_mlp[1:2], hn3, act1, dh4, dh4_16, w1_1, w2_1, tm, tr)
    dz = _plain_mm("gla_out_dx", dh3_16, w_out, NT, F32, tm=TILE_M_BIG)
    dw_out = _plain_mm("gla_out_dw", z, dh3_16, TN, BF16, tm=TILE_M_BIG)
    dw_out_sums = _chip_sums("gla_out_dw", dw_out.reshape(N_DEV, val_dim // N_DEV, d))
    d_o, d_gate, d_g_norm = _head_norm_bwd("gla_head_norm_bwd", o_f, o_b, pm, g_norm, dz, val_dim=val_dim,
                                           gate_block=gate_block, tr=tr)
    dq_f, dk_f, dv_f, dla_f = _gla_bwd("gla_scan_bwd_f", pm, log_a_f, st_f, d_o, rev=False, **dims)
    dq_b, dk_b, dv_b, dla_b = _gla_bwd("gla_scan_bwd_b", pm, log_a_b, st_b, d_o, rev=True, **dims)
    dpr, dw_up_f, db_up_f, dw_up_b, db_up_b = _gate_bwd("gla_gate_bwd", pr, w_up_f, b_up_f, w_up_b, b_up_b, dla_f, dla_b, tr)
    dpm = _proj_grad("gla_proj_grad", dq_f, dq_b, dk_f, dk_b, dv_f, dv_b, d_gate, key_dim=key_dim, val_dim=val_dim, tr=tr)
    dw_main, dw_out_landed = _plain_mm("gla_proj_dw", hn2, dpm, TN, BF16, tm=TILE_M_BIG, scatters=[dw_out_sums])
    dw_r = _plain_mm("gla_proj_gate_dw", hn2, dpr, TN, BF16, tm=1024)
    dw_in = jnp.concatenate([dw_main, dw_r[:, :2 * GATE_RANK]], axis=1)
    dw_in_sums = _chip_sums("gla_proj_dw", dw_in.reshape(d, N_DEV, proj_shard).transpose(1, 0, 2))
    dhn2 = _plain_mm("gla_proj_dx", dpm, w_main, NT, F32, tm=TILE_M_BIG)
    dhn2 = _plain_mm("gla_proj_gate_dx", dpr, w_r, NT, F32, tm=tm, extras=[dhn2], epi=lambda acc, res: (res + acc,))
    dh2, dh2_16, d_norm_mix1 = _rms_bwd("rms_mix1_bwd", h2, norm_mix[1:2], dhn2, dh3, tr)

    dh1, _, d_norm_mlp0, dw1_0, dw2_0, (dw_in_landed,) = _mlp_bwd(
        "0", h1, norm_mlp[0:1], hn1, act0, dh2, dh2_16, w1_0, w2_0, tm, tr, pending=[dw_in_sums])
    dyp, d_pool_scale = _pool_out_grad("pool_out_grad", dh1, y_pre, pool_scale, tr)
    dpooled = _mm(
        "pool_mix_dx", dyp, w_pool, grid=(tokens // tm, d // tn, group // tk),
        a_spec=_spec2(tm, tk, lambda m, n, k: (m, (n // nper) * kper + k)),
        b_spec=pl.BlockSpec((None, tn, tk), lambda m, n, k: (n // nper, n % nper, k)),
        o_spec=_spec2(tm, tn, lambda m, n, k: (m, n)), acc_shape=(tm, tn), outs=[((tokens, d), F32)], dims=NT)[0]
    tmw, tkt = _tile(group, 1024), _tile(tokens, TILE_K)
    mper = group // tmw
    dw_pool = _mm(
        "pool_mix_dw", pooled, dyp, grid=(d // tmw, group // tn, tokens // tkt),
        a_spec=_spec2(tkt, tmw, lambda m, n, k: (k, m)), b_spec=_spec2(tkt, tn, lambda m, n, k: (k, (m // mper) * nper + n)),
        o_spec=pl.BlockSpec((None, tmw, tn), lambda m, n, k: (m // mper, m % mper, n)), acc_shape=(tmw, tn),
        outs=[((N_POOL_GROUPS, group, group), BF16)], dims=TN)[0]
    dw_pool_sums = _chip_sums("pool_mix_dw", dw_pool.reshape(N_POOL_GROUPS, N_DEV, group // N_DEV, group).transpose(1, 0, 2, 3))
    dw_pool_landed = _exchange("scatter_pool", scatters=[dw_pool_sums])[0]
    dhn0 = _pool_slabs("pool_bwd", dpooled, F32, n_seq=n_seq, seq=seq, backward=True)
    dx, _, d_norm_mix0 = _rms_bwd("rms_mix0_bwd", x2, norm_mix[0:1], dhn0, dh1, tr)

    pack_cols = key_dim
    rows_of = lambda a: a.reshape(-1, pack_cols)
    pieces = [rows_of(t) for t in (d_norm_mix0, d_norm_mix1, d_norm_mlp0, d_norm_mlp1, d_norm_final, d_pool_scale,
                                   dw_up_f[:GATE_RANK], dw_up_b[GATE_RANK:2 * GATE_RANK], db_up_f, db_up_b)]
    pieces += [jnp.pad(d_g_norm, ((0, 0), (0, pack_cols - dv))), jnp.pad(loss_row, ((0, 0), (0, pack_cols - LANES)))]
    n_rows = sum(p.shape[0] for p in pieces)
    pack = jnp.concatenate(pieces + [jnp.zeros((-n_rows % 8, pack_cols), F32)], axis=0)
    red = _all_reduce_small("reduce_small", pack)
    cuts, r0 = [], 0
    for p in pieces:
        cuts.append(red[r0:r0 + p.shape[0]])
        r0 += p.shape[0]
    g_norm_mix = jnp.concatenate([cuts[0].reshape(1, d), cuts[1].reshape(1, d)], axis=0)
    g_norm_mlp = jnp.concatenate([cuts[2].reshape(1, d), cuts[3].reshape(1, d)], axis=0)
    g_norm_final = cuts[4].reshape(d)
    g_pool_scale = cuts[5].reshape(1, d)
    shard_cols = lambda t, width: lax.dynamic_slice_in_dim(t, me * width, width, axis=1)
    g_w_up_f, g_w_up_b = shard_cols(cuts[6], kd_shard)[None], shard_cols(cuts[7], kd_shard)[None]
    g_b_up_f, g_b_up_b = shard_cols(cuts[8], kd_shard), shard_cols(cuts[9], kd_shard)
    g_g_norm = shard_cols(cuts[10][:, :dv], dv // N_DEV)
    loss = cuts[11][0, 0]

    big = {
        "pool_w": _adamw_shard("adamw_pool_w", pool_w, m_pool_w, v_pool_w, (dw_pool_landed, dw_pool_sums)),
        "gla_w_in": _adamw_shard("adamw_gla_w_in", gla_w_in, m_gla_w_in, v_gla_w_in, (dw_in_landed, dw_in_sums)),
        "gla_w_out": _adamw_shard("adamw_gla_w_out", gla_w_out, m_gla_w_out, v_gla_w_out, (dw_out_landed, dw_out_sums)),
    }
    for nm, w, m, v, landed in (("mlp_w_in", mlp_w_in, m_mlp_w_in, v_mlp_w_in, (dw1_0, dw1_1)),
                                ("mlp_w_out", mlp_w_out, m_mlp_w_out, v_mlp_w_out, (dw2_0, dw2_1))):
        layer1 = _adamw_shard(f"adamw_{nm}1", w, m, v, landed[1], layer=1)
        big[nm] = _adamw_shard(f"adamw_{nm}0", w, m, v, landed[0], layer=0, into=layer1)
    grads = {"norm_mix": g_norm_mix, "norm_mlp": g_norm_mlp, "norm_final": g_norm_final, "pool_scale": g_pool_scale,
             "gla_w_up_f": g_w_up_f, "gla_b_up_f": g_b_up_f, "gla_w_up_b": g_w_up_b, "gla_b_up_b": g_b_up_b, "gla_g_norm": g_g_norm}
    state = {"norm_mix": (norm_mix, m_norm_mix, v_norm_mix), "norm_mlp": (norm_mlp, m_norm_mlp, v_norm_mlp),
             "norm_final": (norm_final, m_norm_final, v_norm_final), "pool_scale": (pool_scale, m_pool_scale, v_pool_scale),
             "gla_w_up_f": (gla_w_up_f, m_gla_w_up_f, v_gla_w_up_f), "gla_b_up_f": (gla_b_up_f, m_gla_b_up_f, v_gla_b_up_f),
             "gla_w_up_b": (gla_w_up_b, m_gla_w_up_b, v_gla_w_up_b), "gla_b_up_b": (gla_b_up_b, m_gla_b_up_b, v_gla_b_up_b),
             "gla_g_norm": (gla_g_norm, m_gla_g_norm, v_gla_g_norm)}
    order = ["norm_mix", "norm_mlp", "norm_final", "pool_w", "pool_scale", "gla_w_in", "gla_w_up_f", "gla_b_up_f", "gla_w_up_b",
             "gla_b_up_b", "gla_g_norm", "gla_w_out", "mlp_w_in", "mlp_w_out"]
    out_g, out_d, out_m, out_v = [], [], [], []
    for nm in order:
        if nm in big:
            g, dl, new_m, new_v = big[nm]
        else:
            g = grads[nm]
            dl, new_m, new_v = _adamw_small(f"adamw_{nm}", *state[nm], g)
        out_g.append(g), out_d.append(dl), out_m.append(new_m), out_v.append(new_v)
    return (loss, dx.reshape(x.shape), *out_g, *out_d, *out_m, *out_v)
```

```python
import functools

import jax
import jax.numpy as jnp
from jax import lax
from jax.experimental import pallas as pl
from jax.experimental.pallas import tpu as pltpu

F32 = jnp.float32
BF16 = jnp.bfloat16
N_DEV = 8
N_PEER = N_DEV - 1
LANES = 128
VMEM_LIMIT_BYTES = 48 * 1024 * 1024
MESH_ID = pl.DeviceIdType.MESH
TILE_K = 1024
TILE_M_BIG = 2048

POOL_WINDOWS = (2, 4, 8, 16)
N_POOL_GROUPS = len(POOL_WINDOWS)
GLA_HEADS = 4
GATE_RANK = 16
GATE_TAU = 16.0
CHUNK = 64
EPS = 1e-6
ADAM_LR = 0.001
ADAM_B1 = 0.9
ADAM_B2 = 0.999
ADAM_EPS = 1e-08
ADAM_WD = 0.01
ADAM_STEP = 10

NN = ((1,), (0,))
NT = ((1,), (1,))
TN = ((0,), (0,))


def _tile(dim, pref):
    if dim <= pref:
        return dim
    t = (pref // LANES) * LANES
    while t > LANES and dim % t:
        t -= LANES
    assert dim % t == 0, (dim, pref)
    return t


def _row_tile(rows, pref):
    if rows <= pref:
        return rows
    t = max(8, pref // 8 * 8)
    while t > 8 and rows % t:
        t -= 8
    assert rows % t == 0, (rows, pref)
    return t


def _params(*sem):
    return pltpu.CompilerParams(dimension_semantics=sem, vmem_limit_bytes=VMEM_LIMIT_BYTES)


def _dot(a, b, dims):
    return lax.dot_general(a, b, (dims, ((), ())), preferred_element_type=F32)


def _mm(name, a, b, *, grid, a_spec, b_spec, o_spec, acc_shape, outs, dims, extras=(), epi=None, gathers=(), scatters=()):
    n_ex, n_out, nk = len(extras), len(outs), grid[2]
    n_g, n_s = len(gathers), len(scatters)
    carries = bool(n_g or n_s)

    def body(*refs):
        a_ref, b_ref = refs[0], refs[1]
        ex = refs[2:2 + n_ex]
        r0 = 2 + n_ex
        g_in, s_in = refs[r0:r0 + n_g], refs[r0 + n_g:r0 + n_g + n_s]
        r0 += n_g + n_s
        o_refs = refs[r0:r0 + n_out]
        g_out, s_out = refs[r0 + n_out:r0 + n_out + n_g], refs[r0 + n_out + n_g:r0 + n_out + n_g + n_s]
        acc = refs[r0 + n_out + n_g + n_s]
        i, j, k = pl.program_id(0), pl.program_id(1), pl.program_id(2)
        if carries:
            start, middle, finish = _transfers(g_in, g_out, s_in, s_out, (), (), *refs[-3:])
            step = (i * grid[1] + j) * nk + k

            @pl.when(step == 0)
            def _():
                start()

            if n_g:
                @pl.when(step == (grid[0] * grid[1] * nk * 5) // 8)
                def _():
                    middle()

        def part():
            return _dot(a_ref[...].astype(BF16), b_ref[...].astype(BF16), dims)

        def finish_tile(total):
            res = epi(total, *[e[...] for e in ex]) if epi is not None else (total,)
            for o, r in zip(o_refs, res):
                o[...] = r.astype(o.dtype)

        if nk == 1:
            finish_tile(part())
        else:
            @pl.when(k == 0)
            def _():
                acc[...] = part()

            @pl.when((k > 0) & (k < nk - 1))
            def _():
                acc[...] += part()

            @pl.when(k == nk - 1)
            def _():
                finish_tile(acc[...] + part())

        if carries:
            @pl.when((i == grid[0] - 1) & (j == grid[1] - 1) & (k == nk - 1))
            def _():
                finish()

    hbm = pl.BlockSpec(memory_space=pl.ANY)
    operands = [a, b] + [e for e, _ in extras] + list(gathers) + list(scatters)
    in_specs = [a_spec, b_spec] + [s for _, s in extras] + [hbm] * (n_g + n_s)
    out_shape = [jax.ShapeDtypeStruct(s, d) for s, d in outs]
    out_shape += [jax.ShapeDtypeStruct((N_DEV,) + g.shape, g.dtype) for g in gathers]
    out_shape += [jax.ShapeDtypeStruct((len(CHIP_BITS),) + t.shape[1:], t.dtype) for t in scatters]
    scratch = [pltpu.VMEM(acc_shape, F32)] + (_transfer_sems(n_g, n_s, 0) if carries else [])
    sem = ("arbitrary",) * 3 if carries else ("parallel", "parallel", "arbitrary")
    return pl.pallas_call(
        body, name=name, grid=grid, in_specs=in_specs, out_specs=[o_spec] * n_out + [hbm] * (n_g + n_s),
        out_shape=out_shape, scratch_shapes=scratch, compiler_params=_params(*sem),
    )(*operands)


def _spec2(rows, cols, fn):
    return pl.BlockSpec((rows, cols), fn)


def _rowwise(name, fn, *, n_rows, tr, rows, bcast=(), outs=(), accs=()):
    n_r, n_b, n_o, n_a = len(rows), len(bcast), len(outs), len(accs)

    def body(*refs):
        r_refs, b_refs = refs[:n_r], refs[n_r:n_r + n_b]
        o_refs, a_refs = refs[n_r + n_b:n_r + n_b + n_o], refs[n_r + n_b + n_o:]
        res = fn(*[r[...] for r in r_refs], *[b[...] for b in b_refs])
        for o, r in zip(o_refs, res[:n_o]):
            o[...] = r.astype(o.dtype)
        if n_a:
            @pl.when(pl.program_id(0) == 0)
            def _():
                for acc in a_refs:
                    acc[...] = jnp.zeros_like(acc)

            for acc, r in zip(a_refs, res[n_o:]):
                acc[...] += r

    in_specs = [pl.BlockSpec((tr, w), functools.partial(lambda i, cb: (i, cb), cb=cb)) for _, w, cb in rows]
    in_specs += [pl.BlockSpec(b.shape, lambda i: (0, 0)) for b in bcast]
    out_specs = [pl.BlockSpec((tr, w), lambda i: (i, 0)) for w, _ in outs]
    out_specs += [pl.BlockSpec(s, lambda i: (0, 0)) for s in accs]
    out_shape = [jax.ShapeDtypeStruct((n_rows, w), d) for w, d in outs]
    out_shape += [jax.ShapeDtypeStruct(s, F32) for s in accs]
    return pl.pallas_call(
        body, name=name, grid=(n_rows // tr,), in_specs=in_specs, out_specs=out_specs, out_shape=out_shape,
        compiler_params=_params("arbitrary"),
    )(*[r for r, _, _ in rows], *bcast)


def _colsum(t):
    return jnp.sum(t, axis=0, keepdims=True)


def _rms_fwd(name, h, g, dtype, tr):
    def fn(x, gain):
        r = lax.rsqrt(jnp.mean(x * x, axis=-1, keepdims=True) + EPS)
        return (x * r * gain,)

    d = h.shape[1]
    return _rowwise(name, fn, n_rows=h.shape[0], tr=tr, rows=[(h, d, 0)], bcast=[g], outs=[(d, dtype)])[0]


def _rms_bwd(name, h, g, dhn, dres, tr):
    def fn(x, dy, dr, gain):
        r = lax.rsqrt(jnp.mean(x * x, axis=-1, keepdims=True) + EPS)
        xhat = x * r
        gd = gain * dy
        dh = r * (gd - xhat * jnp.mean(gd * xhat, axis=-1, keepdims=True)) + dr
        return dh, dh, _colsum(dy * xhat)

    d = h.shape[1]
    return _rowwise(name, fn, n_rows=h.shape[0], tr=tr, rows=[(h, d, 0), (dhn, d, 0), (dres, d, 0)], bcast=[g],
                    outs=[(d, F32), (d, BF16)], accs=[(1, d)])


def _loss_head(name, h, target, g, tr):
    d = h.shape[1]

    def fn(x, tgt, gain):
        r = lax.rsqrt(jnp.mean(x * x, axis=-1, keepdims=True) + EPS)
        xhat = x * r
        err = xhat * gain - tgt
        loss = 0.5 * jnp.sum(jnp.mean(err * err, axis=-1, keepdims=True), axis=0, keepdims=True)
        dy = err / d
        gd = gain * dy
        dh = r * (gd - xhat * jnp.mean(gd * xhat, axis=-1, keepdims=True))
        return dh, dh, jnp.broadcast_to(loss, (1, LANES)), _colsum(dy * xhat)

    return _rowwise(name, fn, n_rows=h.shape[0], tr=tr, rows=[(h, d, 0), (target, d, 0)], bcast=[g],
                    outs=[(d, F32), (d, BF16)], accs=[(1, LANES), (1, d)])


def _shift_rows(h, k, seq):
    if k == 0:
        return h
    rolled = pltpu.roll(h, (-k) % seq, 0)
    t = lax.broadcasted_iota(jnp.int32, h.shape, 0)
    return jnp.where((t + k >= 0) & (t + k < seq), rolled, 0.0)


def _window_halves(h, half, seq):
    right = h
    left = _shift_rows(h, -1, seq)
    step = 1
    while step < half:
        right = right + _shift_rows(right, step, seq)
        left = left + _shift_rows(left, -step, seq)
        step *= 2
    return left + right


def _window_count(shape, half, seq):
    t = lax.broadcasted_iota(jnp.int32, shape, 0)
    return (jnp.minimum(t + half, seq) - jnp.maximum(t - half, 0)).astype(F32)


def _pool_slabs(name, src, out_dtype, *, n_seq, seq, backward):
    d = src.shape[1]
    group = d // N_POOL_GROUPS
    tc = _tile(group, 256)

    def body(s_ref, o_ref):
        g = (pl.program_id(1) * tc) // group
        for gi, w in enumerate(POOL_WINDOWS):
            half = w // 2

            @pl.when(g == gi)
            def _():
                v = s_ref[...].astype(F32)
                cnt = _window_count(v.shape, half, seq)
                if backward:
                    u = v / cnt
                    res = (_window_halves(u, half, seq) - _shift_rows(u, -half, seq) + _shift_rows(u, half, seq)) - v
                else:
                    res = _window_halves(v, half, seq) / cnt - v
                o_ref[...] = res.astype(o_ref.dtype)

    spec = pl.BlockSpec((seq, tc), lambda b, j: (b, j))
    return pl.pallas_call(
        body, name=name, grid=(n_seq, d // tc), in_specs=[spec], out_specs=spec,
        out_shape=jax.ShapeDtypeStruct(src.shape, out_dtype), compiler_params=_params("parallel", "parallel"),
    )(src)


def _split3(x):
    x1 = x.astype(BF16)
    r1 = x - x1.astype(F32)
    x2 = r1.astype(BF16)
    x3 = (r1 - x2.astype(F32)).astype(BF16)
    return x1, x2, x3


def _tri_dot(tri, x):
    x1, x2, x3 = _split3(x)
    return _dot(tri, x1, NN) + _dot(tri, x2, NN) + _dot(tri, x3, NN)


def _chunk_terms(q_ref, k_ref, g_ref, rev, scale):
    c = q_ref.shape[0]
    ri = lax.broadcasted_iota(jnp.int32, (c, c), 0)
    ci = lax.broadcasted_iota(jnp.int32, (c, c), 1)
    causal = (ci >= ri) if rev else (ci <= ri)
    mid = c // 2 if rev else c // 2 - 1
    last = 0 if rev else c - 1
    b = _tri_dot(causal.astype(BF16), g_ref[...])
    b_mid, b_last = b[mid:mid + 1, :], b[last:last + 1, :]
    e_qm, e_km, e_qi, e_ks = jnp.exp(b - b_mid), jnp.exp(b_mid - b), jnp.exp(b), jnp.exp(b_last - b)
    q = q_ref[...].astype(F32) * scale
    k = k_ref[...].astype(F32)
    return dict(causal=causal, mid=mid, last=last, decay=jnp.exp(b_last), e_qm=e_qm, e_km=e_km, e_qi=e_qi, e_ks=e_ks,
                qm=q * e_qm, km=k * e_km, qi=q * e_qi, ks=k * e_ks)


HEADS_PER_STEP = 2


def _gla_specs(n_chunks, heads, dk, dv, key_dim, rev, backward):
    hps = HEADS_PER_STEP

    def chunk(n):
        scan_pos = (n_chunks - 1 - n) if backward else n
        return (n_chunks - 1 - scan_pos) if rev else scan_pos

    def at(width, col0):
        return pl.BlockSpec((CHUNK, hps * width), lambda b, h, n: (b * n_chunks + chunk(n), col0 // (hps * width) + h))

    state = pl.BlockSpec((hps, dv, dk), lambda b, h, n: ((b * n_chunks + chunk(n)) * (heads // hps) + h, 0, 0))
    return at(dk, 0), at(dk, key_dim), at(dv, 2 * key_dim), at(dk, 0), at(dv, 0), state


def _gla_fwd(name, pm, log_a, *, rev, n_seq, seq, key_dim, val_dim):
    heads, n_chunks = GLA_HEADS, seq // CHUNK
    dk, dv = key_dim // heads, val_dim // heads
    scale = dk ** -0.5
    q_spec, k_spec, v_spec, g_spec, o_spec, st_spec = _gla_specs(n_chunks, heads, dk, dv, key_dim, rev, False)

    def body(q_ref, k_ref, v_ref, g_ref, o_ref, st_ref, state):
        @pl.when(pl.program_id(2) == 0)
        def _():
            state[...] = jnp.zeros_like(state)

        for i in range(HEADS_PER_STEP):
            kc, vc = pl.ds(i * dk, dk), pl.ds(i * dv, dv)
            t = _chunk_terms(q_ref.at[:, kc], k_ref.at[:, kc], g_ref.at[:, kc], rev, scale)
            v = v_ref[:, vc]
            p = jnp.where(t["causal"], _dot(t["qm"].astype(BF16), t["km"].astype(BF16), NT), 0.0)
            entering = state[i].astype(BF16)
            st_ref[i] = entering
            o_ref[:, vc] = _dot(p.astype(BF16), v, NN) + _dot(t["qi"].astype(BF16), entering, NT)
            state[i] = t["decay"] * state[i] + _dot(v, t["ks"].astype(BF16), TN)

    tokens = n_seq * seq
    return pl.pallas_call(
        body, name=name, grid=(n_seq, heads // HEADS_PER_STEP, n_chunks), in_specs=[q_spec, k_spec, v_spec, g_spec],
        out_specs=[o_spec, st_spec],
        out_shape=[jax.ShapeDtypeStruct((tokens, val_dim), F32),
                   jax.ShapeDtypeStruct((n_seq * n_chunks * heads, dv, dk), BF16)],
        scratch_shapes=[pltpu.VMEM((HEADS_PER_STEP, dv, dk), F32)],
        compiler_params=_params("parallel", "parallel", "arbitrary"),
    )(pm, pm, pm, log_a)


def _gla_bwd(name, pm, log_a, states, d_o, *, rev, n_seq, seq, key_dim, val_dim):
    heads, n_chunks = GLA_HEADS, seq // CHUNK
    dk, dv = key_dim // heads, val_dim // heads
    scale = dk ** -0.5
    q_spec, k_spec, v_spec, g_spec, o_spec, st_spec = _gla_specs(n_chunks, heads, dk, dv, key_dim, rev, True)

    def body(q_ref, k_ref, v_ref, g_ref, do_ref, st_ref, dq_ref, dk_ref, dv_ref, dg_ref, dstate):
        @pl.when(pl.program_id(2) == 0)
        def _():
            dstate[...] = jnp.zeros_like(dstate)

        for i in range(HEADS_PER_STEP):
            kc, vc = pl.ds(i * dk, dk), pl.ds(i * dv, dv)
            t = _chunk_terms(q_ref.at[:, kc], k_ref.at[:, kc], g_ref.at[:, kc], rev, scale)
            causal = t["causal"]
            v, d_out, entering = v_ref[:, vc], do_ref[:, vc], st_ref[i]
            qm, km, qi, ks = (t[n].astype(BF16) for n in ("qm", "km", "qi", "ks"))
            dst = dstate[i]
            dst16 = dst.astype(BF16)
            p = jnp.where(causal, _dot(qm, km, NT), 0.0).astype(BF16)
            dp = jnp.where(causal, _dot(d_out, v, NT), 0.0).astype(BF16)
            dv_ref[:, vc] = _dot(p, d_out, TN) + _dot(ks, dst16, NT)
            dqm, dkm = _dot(dp, km, NN), _dot(dp, qm, TN)
            dqi, dks = _dot(d_out, entering, NN), _dot(v, dst16, NN)
            d_decay = _colsum(dst * entering.astype(F32))
            dq_ref[:, kc] = (dqi * t["e_qi"] + dqm * t["e_qm"]) * scale
            dk_ref[:, kc] = dkm * t["e_km"] + dks * t["e_ks"]
            qm_term, km_term = dqm * t["qm"], dkm * t["km"]
            ks_term = dks * t["ks"]
            db = dqi * t["qi"] + qm_term - km_term - ks_term
            row = lax.broadcasted_iota(jnp.int32, db.shape, 0)
            db = db + jnp.where(row == t["mid"], _colsum(km_term - qm_term), 0.0)
            db = db + jnp.where(row == t["last"], _colsum(ks_term) + d_decay * t["decay"], 0.0)
            ri = lax.broadcasted_iota(jnp.int32, causal.shape, 0)
            ci = lax.broadcasted_iota(jnp.int32, causal.shape, 1)
            anti = (ci <= ri) if rev else (ci >= ri)
            dg_ref[:, kc] = _tri_dot(anti.astype(BF16), db)
            dstate[i] = t["decay"] * dst + _dot(d_out, qi, TN)

    tokens = n_seq * seq
    kd = jax.ShapeDtypeStruct((tokens, key_dim), F32)
    return pl.pallas_call(
        body, name=name, grid=(n_seq, heads // HEADS_PER_STEP, n_chunks),
        in_specs=[q_spec, k_spec, v_spec, g_spec, o_spec, st_spec], out_specs=[g_spec, g_spec, o_spec, g_spec],
        out_shape=[kd, kd, jax.ShapeDtypeStruct((tokens, val_dim), F32), kd],
        scratch_shapes=[pltpu.VMEM((HEADS_PER_STEP, dv, dk), F32)],
        compiler_params=_params("parallel", "parallel", "arbitrary"),
    )(pm, pm, pm, log_a, d_o, states)


def _sigmoid_neg(z):
    e = jnp.exp(-jnp.abs(z))
    return jnp.where(z >= 0, e / (1.0 + e), 1.0 / (1.0 + e))


def _gate_fwd(name, pr, wf, bf, wb, bb, tr):
    kd = wf.shape[1]

    def fn(r, w_f, b_f, w_b, b_b):
        r16 = r.astype(BF16)
        out = []
        for w, b in ((w_f, b_f), (w_b, b_b)):
            z = _dot(r16, w.astype(BF16), NN) + b
            out.append((jnp.minimum(z, 0.0) - jnp.log1p(jnp.exp(-jnp.abs(z)))) / GATE_TAU)
        return tuple(out)

    return _rowwise(name, fn, n_rows=pr.shape[0], tr=tr, rows=[(pr, LANES, 0)], bcast=[wf, bf, wb, bb],
                    outs=[(kd, F32), (kd, F32)])


def _gate_bwd(name, pr, wf, bf, wb, bb, dgf, dgb, tr):
    kd = wf.shape[1]

    def fn(r, d_f, d_b, w_f, b_f, w_b, b_b):
        r16 = r.astype(BF16)
        dpr = jnp.zeros(r.shape, F32)
        sums = []
        for w, b, dg in ((w_f, b_f, d_f), (w_b, b_b, d_b)):
            w16 = w.astype(BF16)
            z = _dot(r16, w16, NN) + b
            dz = dg / GATE_TAU * _sigmoid_neg(z)
            dz16 = dz.astype(BF16)
            dpr = dpr + _dot(dz16, w16, NT)
            sums += [_dot(r16, dz16, TN), _colsum(dz)]
        return (dpr, *sums)

    return _rowwise(name, fn, n_rows=pr.shape[0], tr=tr, rows=[(pr, LANES, 0), (dgf, kd, 0), (dgb, kd, 0)],
                    bcast=[wf, bf, wb, bb], outs=[(LANES, BF16)], accs=[(LANES, kd), (1, kd), (LANES, kd), (1, kd)])


def _head_norm_fwd(name, of, ob, pm, gn, *, val_dim, gate_block, tr):
    dv = val_dim // GLA_HEADS

    def fn(o_f, o_b, gate, gain):
        out = []
        for h in range(GLA_HEADS):
            sl = slice(h * dv, (h + 1) * dv)
            o = o_f[:, sl] + o_b[:, sl]
            gt = gate[:, sl].astype(F32)
            on = o * lax.rsqrt(jnp.mean(o * o, axis=-1, keepdims=True) + EPS) * gain
            out.append(on * (gt * jax.nn.sigmoid(gt)))
        return (jnp.concatenate(out, axis=1),)

    return _rowwise(name, fn, n_rows=of.shape[0], tr=tr, rows=[(of, val_dim, 0), (ob, val_dim, 0), (pm, val_dim, gate_block)],
                    bcast=[gn], outs=[(val_dim, BF16)])[0]


def _head_norm_bwd(name, of, ob, pm, gn, dz, *, val_dim, gate_block, tr):
    dv = val_dim // GLA_HEADS

    def fn(o_f, o_b, gate, d_z, gain):
        d_o, d_gate = [], []
        d_gain = jnp.zeros((1, dv), F32)
        for h in range(GLA_HEADS):
            sl = slice(h * dv, (h + 1) * dv)
            o = o_f[:, sl] + o_b[:, sl]
            gt = gate[:, sl].astype(F32)
            dzh = d_z[:, sl]
            r = lax.rsqrt(jnp.mean(o * o, axis=-1, keepdims=True) + EPS)
            ohat = o * r
            sg = jax.nn.sigmoid(gt)
            d_gate.append(dzh * (ohat * gain) * (sg * (1.0 + gt * (1.0 - sg))))
            don = dzh * (gt * sg)
            d_gain = d_gain + _colsum(don * ohat)
            gd = gain * don
            d_o.append(r * (gd - ohat * jnp.mean(gd * ohat, axis=-1, keepdims=True)))
        return jnp.concatenate(d_o, axis=1), jnp.concatenate(d_gate, axis=1), d_gain

    return _rowwise(name, fn, n_rows=of.shape[0], tr=tr,
                    rows=[(of, val_dim, 0), (ob, val_dim, 0), (pm, val_dim, gate_block), (dz, val_dim, 0)], bcast=[gn],
                    outs=[(val_dim, BF16), (val_dim, BF16)], accs=[(1, dv)])


def _proj_grad(name, dqf, dqb, dkf, dkb, dvf, dvb, dgate, *, key_dim, val_dim, tr):
    def fn(q1, q2, k1, k2, v1, v2, gt):
        return (jnp.concatenate([(q1 + q2).astype(BF16), (k1 + k2).astype(BF16), (v1 + v2).astype(BF16), gt], axis=1),)

    rows = [(dqf, key_dim, 0), (dqb, key_dim, 0), (dkf, key_dim, 0), (dkb, key_dim, 0), (dvf, val_dim, 0),
            (dvb, val_dim, 0), (dgate, val_dim, 0)]
    return _rowwise(name, fn, n_rows=dqf.shape[0], tr=tr, rows=rows, outs=[(2 * key_dim + 2 * val_dim, BF16)])[0]


def _pool_out_grad(name, dh, y_pre, scale, tr):
    def fn(d, y, s):
        return d * s, _colsum(d * y)

    dm = dh.shape[1]
    return _rowwise(name, fn, n_rows=dh.shape[0], tr=tr, rows=[(dh, dm, 0), (y_pre, dm, 0)], bcast=[scale],
                    outs=[(dm, BF16)], accs=[(1, dm)])


def _mesh_place():
    x, y, c = lax.axis_index("x"), lax.axis_index("y"), lax.axis_index("c")
    return x, y, c, 4 * x + 2 * y + c


def _peer(x, y, c, p):
    px, py, pc = (x if not p & 4 else 1 - x), (y if not p & 2 else 1 - y), (c if not p & 1 else 1 - c)
    return (px, py, pc), 4 * px + 2 * py + pc


CHIP_BITS = (2, 4, 6)
N_CHIP = N_DEV // 2
GATHER_COPIES = 8
FLIP_C, FLIP_Y, FLIP_X = 1, 2, 4


def _transfer_sems(n_gather, n_scatter, n_pair):
    n_remote = n_gather * GATHER_COPIES + n_scatter * len(CHIP_BITS) + n_pair * N_CHIP
    return [pltpu.SemaphoreType.DMA((n_remote,)), pltpu.SemaphoreType.DMA((n_remote,)),
            pltpu.SemaphoreType.DMA((max(n_gather, 1),))]


def _transfers(g_in, g_out, s_in, s_out, p_in, p_out, send, recv, loc):
    x, y, c, me = _mesh_place()
    sibling, _ = _peer(x, y, c, FLIP_C)

    def remote(src, dst, k, peer):
        return pltpu.make_async_remote_copy(src_ref=src, dst_ref=dst, send_sem=send.at[k], recv_sem=recv.at[k],
                                            device_id=peer, device_id_type=MESH_ID)

    local, first, arrivals = [], [], []
    passed, passed_arrivals, last, last_arrivals = [], [], [], []
    k = 0
    for src, out in zip(g_in, g_out):
        half = src.shape[0] // 2
        halves = (pl.ds(0, half), pl.ds(half, half))
        x_chip, x_block = _peer(x, y, c, FLIP_X)
        y_chip, y_block = _peer(x, y, c, FLIP_Y)
        _, far_block = _peer(x, y, c, FLIP_X | FLIP_Y)
        local.append(pltpu.make_async_copy(src, out.at[me], loc.at[len(local)]))
        for peer, block in ((sibling, me ^ FLIP_C), (x_chip, x_block), (y_chip, y_block)):
            first.append(remote(src, out.at[me], k, peer))
            arrivals.append(remote(src, out.at[block], k, peer))
            k += 1
        for peer, block, part in ((x_chip, y_block, halves[0]), (y_chip, x_block, halves[1])):
            passed.append(remote(out.at[block, part], out.at[block, part], k, peer))
            passed_arrivals.append(remote(out.at[far_block, part], out.at[far_block, part], k, peer))
            k += 1
        for block in (x_block, y_block):
            passed.append(remote(out.at[block], out.at[block], k, sibling))
            passed_arrivals.append(remote(out.at[block ^ FLIP_C], out.at[block ^ FLIP_C], k, sibling))
            k += 1
        last.append(remote(out.at[far_block], out.at[far_block], k, sibling))
        last_arrivals.append(remote(out.at[far_block ^ FLIP_C], out.at[far_block ^ FLIP_C], k, sibling))
        k += 1
    n_gather_first = len(first)
    for src, out in zip(s_in, s_out):
        for j, p in enumerate(CHIP_BITS):
            peer, _ = _peer(x, y, c, p)
            peer_chip = 2 * peer[0] + peer[1]
            first.append(remote(src.at[peer_chip], out.at[j], k, peer))
            arrivals.append(remote(src.at[peer_chip], out.at[j], k, peer))
            k += 1
    for src, out in zip(p_in, p_out):
        for q in range(N_CHIP):
            first.append(remote(src.at[2 * q + 1 - c], out.at[q], k, sibling))
            arrivals.append(remote(src.at[2 * q + 1 - c], out.at[q], k, sibling))
            k += 1

    def start():
        for cp in local + first:
            cp.start()

    def middle():
        for cp in arrivals[:n_gather_first]:
            cp.wait_recv()
        for cp in passed:
            cp.start()

    def finish():
        for cp in passed_arrivals:
            cp.wait_recv()
        for cp in last:
            cp.start()
        for cp in last_arrivals + arrivals[n_gather_first:]:
            cp.wait_recv()
        for cp in first + passed + last:
            cp.wait_send()
        for cp in local:
            cp.wait()

    return start, middle, finish


def _exchange(name, gathers=(), scatters=(), pairs=()):
    n_g, n_s, n_p = len(gathers), len(scatters), len(pairs)
    n_in = n_g + n_s + n_p

    def body(*refs):
        ins, outs = refs[:n_in], refs[n_in:-3]
        start, middle, finish = _transfers(ins[:n_g], outs[:n_g], ins[n_g:n_g + n_s], outs[n_g:n_g + n_s], ins[n_g + n_s:],
                                           outs[n_g + n_s:], *refs[-3:])
        start()
        middle()
        finish()

    hbm = pl.BlockSpec(memory_space=pl.ANY)
    out_shape = [jax.ShapeDtypeStruct((N_DEV,) + g.shape, g.dtype) for g in gathers]
    out_shape += [jax.ShapeDtypeStruct((len(CHIP_BITS),) + t.shape[1:], t.dtype) for t in scatters]
    out_shape += [jax.ShapeDtypeStruct((N_CHIP,) + t.shape[1:], t.dtype) for t in pairs]
    return pl.pallas_call(
        body, name=name, in_specs=[hbm] * n_in, out_specs=[hbm] * len(out_shape), out_shape=out_shape,
        scratch_shapes=_transfer_sems(n_g, n_s, n_p),
    )(*gathers, *scatters, *pairs)


def _place_scalar(axis_value):
    return jnp.asarray(axis_value, jnp.int32).reshape(1)


def _chip_sums(name, grad):
    received = _exchange(name + "_pair", pairs=[grad])[0]
    cols = grad.shape[-1]
    rows = received.size // cols // N_CHIP
    tr = _row_tile(rows, (512 * 1024) // cols)

    def body(core_ref, mine_ref, got_ref, out_ref):
        out_ref[...] = (mine_ref[...].astype(F32) + got_ref[...].astype(F32)).astype(out_ref.dtype)

    block = (None, tr, cols)
    out = pl.pallas_call(
        body, name=name + "_add", out_shape=jax.ShapeDtypeStruct((N_CHIP, rows, cols), grad.dtype),
        grid_spec=pltpu.PrefetchScalarGridSpec(
            num_scalar_prefetch=1, grid=(N_CHIP, rows // tr),
            in_specs=[pl.BlockSpec(block, lambda q, i, core: (2 * q + core[0], i, 0)),
                      pl.BlockSpec(block, lambda q, i, core: (q, i, 0))],
            out_specs=pl.BlockSpec(block, lambda q, i, core: (q, i, 0))),
        compiler_params=_params("parallel", "parallel"),
    )(_place_scalar(lax.axis_index("c")), grad.reshape(N_DEV, rows, cols), received.reshape(N_CHIP, rows, cols))
    return out.reshape(received.shape)


def _all_reduce_small(name, pack):
    def body(in_ref, out_ref, parts, send_sems, recv_sems):
        x, y, c, me = _mesh_place()
        parts[me] = in_ref[...]

        def copy(p, arriving):
            peer, peer_block = _peer(x, y, c, p)
            return pltpu.make_async_remote_copy(src_ref=in_ref, dst_ref=parts.at[peer_block if arriving else me],
                                                send_sem=send_sems.at[p - 1], recv_sem=recv_sems.at[p - 1],
                                                device_id=peer, device_id_type=MESH_ID)

        sends = [copy(p, False) for p in range(1, N_DEV)]
        for cp in sends:
            cp.start()
        for p in range(1, N_DEV):
            copy(p, True).wait_recv()
        for cp in sends:
            cp.wait_send()
        total = parts[0]
        for j in range(1, N_DEV):
            total = total + parts[j]
        out_ref[...] = total

    vmem = pl.BlockSpec(memory_space=pltpu.VMEM)
    return pl.pallas_call(
        body, name=name, in_specs=[vmem], out_specs=vmem, out_shape=jax.ShapeDtypeStruct(pack.shape, F32),
        scratch_shapes=[pltpu.VMEM((N_DEV,) + pack.shape, F32), pltpu.SemaphoreType.DMA((N_PEER,)),
                        pltpu.SemaphoreType.DMA((N_PEER,))],
        compiler_params=pltpu.CompilerParams(vmem_limit_bytes=VMEM_LIMIT_BYTES),
    )(pack)


def _adamw_math(w, g, m, v):
    m = ADAM_B1 * m + (1.0 - ADAM_B1) * g
    v = ADAM_B2 * v + (1.0 - ADAM_B2) * jnp.square(g)
    m_hat = m / (1.0 - ADAM_B1 ** ADAM_STEP)
    v_hat = v / (1.0 - ADAM_B2 ** ADAM_STEP)
    delta = -ADAM_LR * (m_hat / (jnp.sqrt(v_hat) + ADAM_EPS) + ADAM_WD * w)
    return delta, m, v


def _adamw_shard(name, w, m, v, grad_parts, layer=0, into=None):
    landed, sums = grad_parts
    shape = w.shape
    n_layers, cols = shape[0], shape[-1]
    rows = w.size // cols // n_layers
    tr = _row_tile(rows, (256 * 1024) // cols)
    n_into = 0 if into is None else 4
    n_landed = landed.shape[0]

    def body(chip_ref, w_ref, m_ref, v_ref, mine_ref, got_ref, *refs):
        g_ref, d_ref, nm_ref, nv_ref = refs[n_into:]
        g = mine_ref[...].astype(F32)
        for j in range(n_landed):
            g = g + got_ref[j].astype(F32)
        delta, new_m, new_v = _adamw_math(w_ref[...], g, m_ref[...], v_ref[...])
        g_ref[...], d_ref[...], nm_ref[...], nv_ref[...] = g, delta, new_m, new_v

    if n_layers == 1:
        view = (rows, cols)
        spec = pl.BlockSpec((tr, cols), lambda i, chip: (i, 0))
    else:
        view = (n_layers, rows, cols)
        spec = pl.BlockSpec((None, tr, cols), lambda i, chip: (layer, i, 0))
    operands = [w.reshape(view), m.reshape(view), v.reshape(view), sums.reshape(N_CHIP, rows, cols),
                landed.reshape(n_landed, rows, cols)]
    in_specs = [spec, spec, spec, pl.BlockSpec((None, tr, cols), lambda i, chip: (chip[0], i, 0)),
                pl.BlockSpec((n_landed, tr, cols), lambda i, chip: (0, i, 0))]
    if into is not None:
        operands += [t.reshape(view) for t in into]
        in_specs += [pl.BlockSpec(memory_space=pl.ANY)] * 4
    first_into = 1 + 5
    res = pl.pallas_call(
        body, name=name, out_shape=[jax.ShapeDtypeStruct(view, F32)] * 4,
        grid_spec=pltpu.PrefetchScalarGridSpec(num_scalar_prefetch=1, grid=(rows // tr,), in_specs=in_specs, out_specs=[spec] * 4),
        input_output_aliases={first_into + t: t for t in range(n_into)}, compiler_params=_params("parallel"),
    )(_place_scalar(2 * lax.axis_index("x") + lax.axis_index("y")), *operands)
    return [r.reshape(shape) for r in res]


def _adamw_small(name, w, m, v, g):
    shape = w.shape
    two_d = (1, shape[0]) if len(shape) == 1 else (w.size // shape[-1], shape[-1])

    def body(w_ref, m_ref, v_ref, g_ref, d_ref, nm_ref, nv_ref):
        d_ref[...], nm_ref[...], nv_ref[...] = _adamw_math(w_ref[...], g_ref[...], m_ref[...], v_ref[...])

    res = pl.pallas_call(body, name=name, out_shape=[jax.ShapeDtypeStruct(two_d, F32)] * 3)(
        w.reshape(two_d), m.reshape(two_d), v.reshape(two_d), g.reshape(two_d))
    return [r.reshape(shape) for r in res]


def _mlp_fwd(tag, h_in, gain, w1, w2_mine, down_gathers, tm, tr):
    tokens, d = h_in.shape
    fs = w1.shape[2]
    f = N_DEV * fs
    hn = _rms_fwd(f"rms_mlp{tag}", h_in, gain, BF16, tr)
    tn, tk = _tile(fs, 1024), _tile(d, TILE_K)
    per = fs // tn
    tmu = _tile(tokens, TILE_M_BIG)
    act, w2 = _mm(
        f"mlp_up{tag}", hn, w1, grid=(tokens // tmu, f // tn, d // tk), a_spec=_spec2(tmu, tk, lambda m, n, k: (m, k)),
        b_spec=pl.BlockSpec((None, tk, tn), lambda m, n, k: (n // per, k, n % per)),
        o_spec=_spec2(tmu, tn, lambda m, n, k: (m, n)), acc_shape=(tmu, tn), outs=[((tokens, f), BF16)], dims=NN,
        epi=lambda acc: (jnp.square(jnp.maximum(acc, 0.0)),), gathers=[w2_mine])
    tn2, tk2 = _tile(d, 1024), _tile(fs, TILE_K)
    kper = fs // tk2
    h_out, *got = _mm(
        f"mlp_down{tag}", act, w2, grid=(tokens // tm, d // tn2, f // tk2), a_spec=_spec2(tm, tk2, lambda m, n, k: (m, k)),
        b_spec=pl.BlockSpec((None, tk2, tn2), lambda m, n, k: (k // kper, k % kper, n)),
        o_spec=_spec2(tm, tn2, lambda m, n, k: (m, n)), acc_shape=(tm, tn2), outs=[((tokens, d), F32)], dims=NN,
        extras=[(h_in, _spec2(tm, tn2, lambda m, n, k: (m, n)))], epi=lambda acc, res: (res + acc,), gathers=down_gathers)
    return hn, act, h_out, w2, got


def _mlp_bwd(tag, h_in, gain, hn, act, dh, dh16, w1, w2, tm, tr, pending=()):
    tokens, d = h_in.shape
    fs = w1.shape[2]
    f = N_DEV * fs
    tn, tk = _tile(fs, 1024), _tile(d, TILE_K)
    nper = fs // tn
    tmb = _tile(tokens, TILE_M_BIG)
    tkt = _tile(tokens, TILE_K)
    tm2, tn2 = _tile(fs, TILE_M_BIG), _tile(d, 1024)
    mper = fs // tm2
    dw2, *pending_landed = _mm(
        f"mlp_dw_out{tag}", act, dh16, grid=(f // tm2, d // tn2, tokens // tkt), a_spec=_spec2(tkt, tm2, lambda m, n, k: (k, m)),
        b_spec=_spec2(tkt, tn2, lambda m, n, k: (k, n)),
        o_spec=pl.BlockSpec((None, tm2, tn2), lambda m, n, k: (m // mper, m % mper, n)), acc_shape=(tm2, tn2),
        outs=[((N_DEV, fs, d), BF16)], dims=TN, scatters=pending)
    dw2_sums = _chip_sums(f"mlp_dw_out{tag}", dw2)
    da, dw2_landed = _mm(
        f"mlp_dact{tag}", dh16, w2, grid=(tokens // tmb, f // tn, d // tk), a_spec=_spec2(tmb, tk, lambda m, n, k: (m, k)),
        b_spec=pl.BlockSpec((None, tn, tk), lambda m, n, k: (n // nper, n % nper, k)),
        o_spec=_spec2(tmb, tn, lambda m, n, k: (m, n)), acc_shape=(tmb, tn), outs=[((tokens, f), BF16)], dims=NT,
        extras=[(act, _spec2(tmb, tn, lambda m, n, k: (m, n)))],
        epi=lambda acc, a2: (acc * (2.0 * jnp.sqrt(a2.astype(F32))),), scatters=[dw2_sums])
    tm3 = _tile(d, TILE_M_BIG)
    dw1 = _mm(
        f"mlp_dw_in{tag}", hn, da, grid=(d // tm3, f // tn, tokens // tkt), a_spec=_spec2(tkt, tm3, lambda m, n, k: (k, m)),
        b_spec=_spec2(tkt, tn, lambda m, n, k: (k, n)),
        o_spec=pl.BlockSpec((None, tm3, tn), lambda m, n, k: (n // nper, m, n % nper)), acc_shape=(tm3, tn),
        outs=[((N_DEV, d, fs), BF16)], dims=TN)[0]
    dw1_sums = _chip_sums(f"mlp_dw_in{tag}", dw1)
    tn4, tk4 = _tile(d, 1024), _tile(fs, TILE_K)
    kper = fs // tk4
    dhn, dw1_landed = _mm(
        f"mlp_dx{tag}", da, w1, grid=(tokens // tmb, d // tn4, f // tk4), a_spec=_spec2(tmb, tk4, lambda m, n, k: (m, k)),
        b_spec=pl.BlockSpec((None, tn4, tk4), lambda m, n, k: (k // kper, n, k % kper)),
        o_spec=_spec2(tmb, tn4, lambda m, n, k: (m, n)), acc_shape=(tmb, tn4), outs=[((tokens, d), F32)], dims=NT,
        scatters=[dw1_sums])
    dh_in, dh_in16, dgain = _rms_bwd(f"rms_mlp_bwd{tag}", h_in, gain, dhn, dh, tr)
    return dh_in, dh_in16, dgain, (dw1_landed, dw1_sums), (dw2_landed, dw2_sums), pending_landed


def _plain_mm(name, a, b, dims, out_dtype, *, tm, tn_pref=1024, tk_pref=TILE_K, extras=None, epi=None, gathers=(), scatters=(),
              n_cols=None):
    if dims == NN:
        (m, k), n = a.shape, n_cols or b.shape[1]
    elif dims == NT:
        (m, k), n = a.shape, b.shape[0]
    else:
        (k, m), n = a.shape, b.shape[1]
    tm, tn, tk = _tile(m, tm), _tile(n, tn_pref), _tile(k, tk_pref)
    a_spec = _spec2(tk, tm, lambda i, j, l: (l, i)) if dims == TN else _spec2(tm, tk, lambda i, j, l: (i, l))
    b_spec = _spec2(tn, tk, lambda i, j, l: (j, l)) if dims == NT else _spec2(tk, tn, lambda i, j, l: (l, j))
    o_spec = _spec2(tm, tn, lambda i, j, l: (i, j))
    ex = [(e, o_spec) for e in (extras or [])]
    res = _mm(name, a, b, grid=(m // tm, n // tn, k // tk), a_spec=a_spec, b_spec=b_spec, o_spec=o_spec, acc_shape=(tm, tn),
              outs=[((m, n), out_dtype)], dims=dims, extras=ex, epi=epi, gathers=gathers, scatters=scatters)
    return res if (gathers or scatters) else res[0]


def kernel(x, norm_mix, norm_mlp, norm_final, pool_w, pool_scale, gla_w_in, gla_w_up_f, gla_b_up_f, gla_w_up_b, gla_b_up_b, gla_g_norm, gla_w_out, mlp_w_in, mlp_w_out, loss_target, m_norm_mix, m_norm_mlp, m_norm_final, m_pool_w, m_pool_scale, m_gla_w_in, m_gla_w_up_f, m_gla_b_up_f, m_gla_w_up_b, m_gla_b_up_b, m_gla_g_norm, m_gla_w_out, m_mlp_w_in, m_mlp_w_out, v_norm_mix, v_norm_mlp, v_norm_final, v_pool_w, v_pool_scale, v_gla_w_in, v_gla_w_up_f, v_gla_b_up_f, v_gla_w_up_b, v_gla_b_up_b, v_gla_g_norm, v_gla_w_out, v_mlp_w_in, v_mlp_w_out):
    n_seq, seq, d = x.shape
    tokens = n_seq * seq
    group = d // N_POOL_GROUPS
    key_dim, val_dim = d // 2, d
    dv = val_dim // GLA_HEADS
    n_main = 2 * key_dim + 2 * val_dim
    proj_shard = gla_w_in.shape[2]
    proj_width = N_DEV * proj_shard
    kd_shard = key_dim // N_DEV
    assert proj_width == n_main + 2 * GATE_RANK and seq % CHUNK == 0 and 2 * GATE_RANK <= LANES
    assert mlp_w_in.shape[0] == 2 and pool_w.shape[0] == 1 and gla_w_in.shape[0] == 1
    me = 4 * lax.axis_index("x") + 2 * lax.axis_index("y") + lax.axis_index("c")
    tm = _tile(tokens, 1024)
    tr = _tile(tokens, 128)
    x2, target = x.reshape(tokens, d), loss_target.reshape(tokens, d)

    small_rows = 2 * GATE_RANK + 4
    small = jnp.concatenate([
        gla_w_up_f[0], gla_w_up_b[0], gla_b_up_f, gla_b_up_b,
        jnp.pad(gla_g_norm, ((0, 0), (0, kd_shard - gla_g_norm.shape[1]))), jnp.zeros((1, kd_shard), F32)], axis=0)
    w1_mine = [mlp_w_in[l].astype(BF16) for l in range(2)]
    w2_mine = [mlp_w_out[l].astype(BF16) for l in range(2)]
    pool_g, small_g, w1_0 = _exchange(
        "gather_first", gathers=[pool_w[0].astype(BF16), small.reshape(2, small_rows // 2, kd_shard), w1_mine[0]])
    small_g = small_g.reshape(N_DEV, small_rows, kd_shard)
    w_pool = pool_g.transpose(1, 0, 2, 3).reshape(N_POOL_GROUPS, group, group)
    small_full = small_g.transpose(1, 0, 2).reshape(small_rows, key_dim)
    pad_rows = lambda w, r0: jnp.pad(w, ((r0, LANES - r0 - GATE_RANK), (0, 0)))
    w_up_f, w_up_b = pad_rows(small_full[:GATE_RANK], 0), pad_rows(small_full[GATE_RANK:2 * GATE_RANK], GATE_RANK)
    b_up_f, b_up_b = small_full[2 * GATE_RANK:2 * GATE_RANK + 1], small_full[2 * GATE_RANK + 1:2 * GATE_RANK + 2]
    g_norm = small_g[:, 2 * GATE_RANK + 2, :gla_g_norm.shape[1]].reshape(1, dv)

    hn0 = _rms_fwd("rms_mix0", x2, norm_mix[0:1], F32, tr)
    pooled = _pool_slabs("pool_fwd", hn0, BF16, n_seq=n_seq, seq=seq, backward=False)
    tn, tk = _tile(group, 1024), _tile(group, TILE_K)
    nper, kper = group // tn, group // tk
    y_pre, h1 = _mm(
        "pool_mix", pooled, w_pool, grid=(tokens // tm, d // tn, group // tk),
        a_spec=_spec2(tm, tk, lambda m, n, k: (m, (n // nper) * kper + k)),
        b_spec=pl.BlockSpec((None, tk, tn), lambda m, n, k: (n // nper, k, n % nper)),
        o_spec=_spec2(tm, tn, lambda m, n, k: (m, n)), acc_shape=(tm, tn), outs=[((tokens, d), F32)] * 2, dims=NN,
        extras=[(x2, _spec2(tm, tn, lambda m, n, k: (m, n))), (pool_scale, _spec2(1, tn, lambda m, n, k: (0, n)))],
        epi=lambda acc, res, s: (acc, res + acc * s))
    hn1, act0, h2, w2_0, (win_g, wout_g) = _mlp_fwd(
        "0", h1, norm_mlp[0:1], w1_0, w2_mine[0], [gla_w_in[0].astype(BF16), gla_w_out[0].astype(BF16)], tm, tr)
    w_in = win_g.transpose(1, 0, 2).reshape(d, proj_width)
    w_r = jnp.pad(w_in[:, n_main:], ((0, 0), (0, LANES - 2 * GATE_RANK)))
    w_out = wout_g.reshape(val_dim, d)

    hn2 = _rms_fwd("rms_mix1", h2, norm_mix[1:2], BF16, tr)
    pm, w1_1 = _plain_mm("gla_proj", hn2, w_in, NN, BF16, tm=TILE_M_BIG, gathers=[w1_mine[1]], n_cols=n_main)
    pr = _plain_mm("gla_proj_gate", hn2, w_r, NN, F32, tm=tm)
    log_a_f, log_a_b = _gate_fwd("gla_gate", pr, w_up_f, b_up_f, w_up_b, b_up_b, tr)
    dims = dict(n_seq=n_seq, seq=seq, key_dim=key_dim, val_dim=val_dim)
    o_f, st_f = _gla_fwd("gla_scan_f", pm, log_a_f, rev=False, **dims)
    o_b, st_b = _gla_fwd("gla_scan_b", pm, log_a_b, rev=True, **dims)
    gate_block = (2 * key_dim + val_dim) // val_dim
    z = _head_norm_fwd("gla_head_norm", o_f, o_b, pm, g_norm, val_dim=val_dim, gate_block=gate_block, tr=tr)
    h3 = _plain_mm("gla_out", z, w_out, NN, F32, tm=tm, extras=[h2], epi=lambda acc, res: (res + acc,))
    hn3, act1, h4, w2_1, _ = _mlp_fwd("1", h3, norm_mlp[1:2], w1_1, w2_mine[1], [], tm, tr)

    dh4, dh4_16, loss_row, d_norm_final = _loss_head("loss_head", h4, target, norm_final.reshape(1, d), tr)

    dh3, dh3_16, d_norm_mlp1, dw1_1, dw2_1, _ = _mlp_bwd("1", h3, norm_mlp[1:2], hn3, act1, dh4, dh4_16, w1_1, w2_1, tm, tr)
    dz = _plain_mm("gla_out_dx", dh3_16, w_out, NT, F32, tm=TILE_M_BIG)
    dw_out = _plain_mm("gla_out_dw", z, dh3_16, TN, BF16, tm=TILE_M_BIG)
    dw_out_sums = _chip_sums("gla_out_dw", dw_out.reshape(N_DEV, val_dim // N_DEV, d))
    d_o, d_gate, d_g_norm = _head_norm_bwd("gla_head_norm_bwd", o_f, o_b, pm, g_norm, dz, val_dim=val_dim,
                                           gate_block=gate_block, tr=tr)
    dq_f, dk_f, dv_f, dla_f = _gla_bwd("gla_scan_bwd_f", pm, log_a_f, st_f, d_o, rev=False, **dims)
    dq_b, dk_b, dv_b, dla_b = _gla_bwd("gla_scan_bwd_b", pm, log_a_b, st_b, d_o, rev=True, **dims)
    dpr, dw_up_f, db_up_f, dw_up_b, db_up_b = _gate_bwd("gla_gate_bwd", pr, w_up_f, b_up_f, w_up_b, b_up_b, dla_f, dla_b, tr)
    dpm = _proj_grad("gla_proj_grad", dq_f, dq_b, dk_f, dk_b, dv_f, dv_b, d_gate, key_dim=key_dim, val_dim=val_dim, tr=tr)
    dw_main, dw_out_landed = _plain_mm("gla_proj_dw", hn2, dpm, TN, BF16, tm=TILE_M_BIG, scatters=[dw_out_sums])
    dw_r = _plain_mm("gla_proj_gate_dw", hn2, dpr, TN, BF16, tm=1024)
    dw_in = jnp.concatenate([dw_main, dw_r[:, :2 * GATE_RANK]], axis=1)
    dw_in_sums = _chip_sums("gla_proj_dw", dw_in.reshape(d, N_DEV, proj_shard).transpose(1, 0, 2))
    dhn2 = _plain_mm("gla_proj_dx", dpm, w_in, NT, F32, tm=TILE_M_BIG)
    dhn2 = _plain_mm("gla_proj_gate_dx", dpr, w_r, NT, F32, tm=tm, extras=[dhn2], epi=lambda acc, res: (res + acc,))
    dh2, dh2_16, d_norm_mix1 = _rms_bwd("rms_mix1_bwd", h2, norm_mix[1:2], dhn2, dh3, tr)

    dh1, _, d_norm_mlp0, dw1_0, dw2_0, (dw_in_landed,) = _mlp_bwd(
        "0", h1, norm_mlp[0:1], hn1, act0, dh2, dh2_16, w1_0, w2_0, tm, tr, pending=[dw_in_sums])
    dyp, d_pool_scale = _pool_out_grad("pool_out_grad", dh1, y_pre, pool_scale, tr)
    dpooled = _mm(
        "pool_mix_dx", dyp, w_pool, grid=(tokens // tm, d // tn, group // tk),
        a_spec=_spec2(tm, tk, lambda m, n, k: (m, (n // nper) * kper + k)),
        b_spec=pl.BlockSpec((None, tn, tk), lambda m, n, k: (n // nper, n % nper, k)),
        o_spec=_spec2(tm, tn, lambda m, n, k: (m, n)), acc_shape=(tm, tn), outs=[((tokens, d), F32)], dims=NT)[0]
    tmw, tkt = _tile(group, 1024), _tile(tokens, TILE_K)
    mper = group // tmw
    dw_pool = _mm(
        "pool_mix_dw", pooled, dyp, grid=(d // tmw, group // tn, tokens // tkt),
        a_spec=_spec2(tkt, tmw, lambda m, n, k: (k, m)), b_spec=_spec2(tkt, tn, lambda m, n, k: (k, (m // mper) * nper + n)),
        o_spec=pl.BlockSpec((None, tmw, tn), lambda m, n, k: (m // mper, m % mper, n)), acc_shape=(tmw, tn),
        outs=[((N_POOL_GROUPS, group, group), BF16)], dims=TN)[0]
    dw_pool_sums = _chip_sums("pool_mix_dw", dw_pool.reshape(N_POOL_GROUPS, N_DEV, group // N_DEV, group).transpose(1, 0, 2, 3))
    dw_pool_landed = _exchange("scatter_pool", scatters=[dw_pool_sums])[0]
    dhn0 = _pool_slabs("pool_bwd", dpooled, F32, n_seq=n_seq, seq=seq, backward=True)
    dx, _, d_norm_mix0 = _rms_bwd("rms_mix0_bwd", x2, norm_mix[0:1], dhn0, dh1, tr)

    pack_cols = key_dim
    rows_of = lambda a: a.reshape(-1, pack_cols)
    pieces = [rows_of(t) for t in (d_norm_mix0, d_norm_mix1, d_norm_mlp0, d_norm_mlp1, d_norm_final, d_pool_scale,
                                   dw_up_f[:GATE_RANK], dw_up_b[GATE_RANK:2 * GATE_RANK], db_up_f, db_up_b)]
    pieces += [jnp.pad(d_g_norm, ((0, 0), (0, pack_cols - dv))), jnp.pad(loss_row, ((0, 0), (0, pack_cols - LANES)))]
    n_rows = sum(p.shape[0] for p in pieces)
    pack = jnp.concatenate(pieces + [jnp.zeros((-n_rows % 8, pack_cols), F32)], axis=0)
    red = _all_reduce_small("reduce_small", pack)
    cuts, r0 = [], 0
    for p in pieces:
        cuts.append(red[r0:r0 + p.shape[0]])
        r0 += p.shape[0]
    g_norm_mix = jnp.concatenate([cuts[0].reshape(1, d), cuts[1].reshape(1, d)], axis=0)
    g_norm_mlp = jnp.concatenate([cuts[2].reshape(1, d), cuts[3].reshape(1, d)], axis=0)
    g_norm_final = cuts[4].reshape(d)
    g_pool_scale = cuts[5].reshape(1, d)
    shard_cols = lambda t, width: lax.dynamic_slice_in_dim(t, me * width, width, axis=1)
    g_w_up_f, g_w_up_b = shard_cols(cuts[6], kd_shard)[None], shard_cols(cuts[7], kd_shard)[None]
    g_b_up_f, g_b_up_b = shard_cols(cuts[8], kd_shard), shard_cols(cuts[9], kd_shard)
    g_g_norm = shard_cols(cuts[10][:, :dv], dv // N_DEV)
    loss = cuts[11][0, 0]

    big = {
        "pool_w": _adamw_shard("adamw_pool_w", pool_w, m_pool_w, v_pool_w, (dw_pool_landed, dw_pool_sums)),
        "gla_w_in": _adamw_shard("adamw_gla_w_in", gla_w_in, m_gla_w_in, v_gla_w_in, (dw_in_landed, dw_in_sums)),
        "gla_w_out": _adamw_shard("adamw_gla_w_out", gla_w_out, m_gla_w_out, v_gla_w_out, (dw_out_landed, dw_out_sums)),
    }
    for nm, w, m, v, landed in (("mlp_w_in", mlp_w_in, m_mlp_w_in, v_mlp_w_in, (dw1_0, dw1_1)),
                                ("mlp_w_out", mlp_w_out, m_mlp_w_out, v_mlp_w_out, (dw2_0, dw2_1))):
        layer1 = _adamw_shard(f"adamw_{nm}1", w, m, v, landed[1], layer=1)
        big[nm] = _adamw_shard(f"adamw_{nm}0", w, m, v, landed[0], layer=0, into=layer1)
    grads = {"norm_mix": g_norm_mix, "norm_mlp": g_norm_mlp, "norm_final": g_norm_final, "pool_scale": g_pool_scale,
             "gla_w_up_f": g_w_up_f, "gla_b_up_f": g_b_up_f, "gla_w_up_b": g_w_up_b, "gla_b_up_b": g_b_up_b, "gla_g_norm": g_g_norm}
    state = {"norm_mix": (norm_mix, m_norm_mix, v_norm_mix), "norm_mlp": (norm_mlp, m_norm_mlp, v_norm_mlp),
             "norm_final": (norm_final, m_norm_final, v_norm_final), "pool_scale": (pool_scale, m_pool_scale, v_pool_scale),
             "gla_w_up_f": (gla_w_up_f, m_gla_w_up_f, v_gla_w_up_f), "gla_b_up_f": (gla_b_up_f, m_gla_b_up_f, v_gla_b_up_f),
             "gla_w_up_b": (gla_w_up_b, m_gla_w_up_b, v_gla_w_up_b), "gla_b_up_b": (gla_b_up_b, m_gla_b_up_b, v_gla_b_up_b),
             "gla_g_norm": (gla_g_norm, m_gla_g_norm, v_gla_g_norm)}
    order = ["norm_mix", "norm_mlp", "norm_final", "pool_w", "pool_scale", "gla_w_in", "gla_w_up_f", "gla_b_up_f", "gla_w_up_b",
             "gla_b_up_b", "gla_g_norm", "gla_w_out", "mlp_w_in", "mlp_w_out"]
    out_g, out_d, out_m, out_v = [], [], [], []
    for nm in order:
        if nm in big:
            g, dl, new_m, new_v = big[nm]
        else:
            g = grads[nm]
            dl, new_m, new_v = _adamw_small(f"adamw_{nm}", *state[nm], g)
        out_g.append(g), out_d.append(dl), out_m.append(new_m), out_v.append(new_v)
    return (loss, dx.reshape(x.shape), *out_g, *out_d, *out_m, *out_v)
```
